```python
import math
import jax, jax.numpy as jnp
from jax import lax
import numpy as np

D_MODEL = 2048
BATCH = 8
SEQ = 2048
DEPTH = 1
DEC_BATCH = 32
DEC_SEQ = 1
PAST_LEN = 16384
PAGE_SIZE = 128

DIFF_HEADS = 8
DIFF_QK_DIM = 64
DIFF_V_DIM = 2 * DIFF_QK_DIM
DIFF_WIDTH = DIFF_HEADS * DIFF_V_DIM
NSA_HEADS = 8
NSA_KV_HEADS = 2
NSA_GROUP = NSA_HEADS // NSA_KV_HEADS
NSA_HEAD_DIM = 128
NSA_WIDTH = NSA_HEADS * NSA_HEAD_DIM
CMP_BLOCK = 64
SEL_BLOCK = CMP_BLOCK
N_SEL = 16
WINDOW = 512
FORCE_BONUS = 1e4
D_FF = ((8 * D_MODEL // 3 + 127) // 128) * 128
CONV_W = 3
ROPE_THETA = 10000.0
EPS = 1e-6
NEG = -1e30
TINY = 1e-30
Q_BLOCK = 128
SEL_Q_CHUNK = 16

IN_WIDTHS = (DIFF_HEADS * 2 * DIFF_QK_DIM,
             DIFF_HEADS * 2 * DIFF_QK_DIM,
             DIFF_HEADS * DIFF_V_DIM,
             NSA_HEADS * NSA_HEAD_DIM,
             6 * NSA_KV_HEADS * NSA_HEAD_DIM,
             3 * NSA_HEADS,
             2 * D_MODEL)
SPLIT_POINTS = tuple(int(v) for v in np.cumsum(IN_WIDTHS)[:-1])
N_IN = sum(IN_WIDTHS)

kernel_name = 'diff_nsa_hybrid_decode_step'


def rmsnorm(x, g):
    xf = x.astype(jnp.float32)
    y = xf * lax.rsqrt(jnp.mean(xf * xf, axis=-1, keepdims=True) + EPS)
    return (y * g.astype(jnp.float32)).astype(x.dtype)


def rope(x, pos):
    d = x.shape[-1]
    inv = 1.0 / (ROPE_THETA ** (jnp.arange(0, d, 2, dtype=jnp.float32) / d))
    ang = pos.astype(jnp.float32)[:, None] * inv[None, :]
    shp = (1, x.shape[1]) + (1,) * (x.ndim - 3) + (d // 2,)
    cos = jnp.cos(ang).reshape(shp)
    sin = jnp.sin(ang).reshape(shp)
    xf = x.astype(jnp.float32)
    x1, x2 = xf[..., : d // 2], xf[..., d // 2:]
    return jnp.concatenate([x1 * cos - x2 * sin, x2 * cos + x1 * sin], axis=-1).astype(x.dtype)


def masked_softmax(s, mask):
    s = jnp.where(mask, s.astype(jnp.float32), NEG)
    m = jnp.max(s, axis=-1, keepdims=True)
    p = jnp.exp(s - m) * mask
    return p / jnp.maximum(jnp.sum(p, axis=-1, keepdims=True), TINY)


def project_inputs(h, w_in, pos):
    B, T, _ = h.shape
    z = jnp.einsum('btd,dn->btn', h, w_in)
    dq, dk, dv, nq, nkv, ng, mg = jnp.split(z, SPLIT_POINTS, axis=-1)
    dq = rope(dq.reshape(B, T, DIFF_HEADS, 2, DIFF_QK_DIM), pos).reshape(B, T, DIFF_HEADS, 2 * DIFF_QK_DIM)
    dk = rope(dk.reshape(B, T, DIFF_HEADS, 2, DIFF_QK_DIM), pos).reshape(B, T, DIFF_HEADS, 2 * DIFF_QK_DIM)
    dv = dv.reshape(B, T, DIFF_HEADS, DIFF_V_DIM)
    nq = rope(nq.reshape(B, T, NSA_HEADS, NSA_HEAD_DIM), pos)
    nkv = nkv.reshape(B, T, 3, 2, NSA_KV_HEADS, NSA_HEAD_DIM)
    nkv = jnp.stack([rope(nkv[:, :, :, 0], pos), nkv[:, :, :, 1]], axis=3)
    nkv = nkv.reshape(B, T, 6, NSA_KV_HEADS, NSA_HEAD_DIM)
    ng = jax.nn.sigmoid(ng.reshape(B, T, 3, NSA_KV_HEADS, NSA_GROUP))
    mg = jax.nn.sigmoid(mg.reshape(B, T, 2, D_MODEL))
    return dq, dk, dv, nq, nkv, ng, mg


def diff_lambda_value(lam, lam_init):
    lf = lam.astype(jnp.float32)
    return jnp.exp(jnp.sum(lf[0] * lf[1])) - jnp.exp(jnp.sum(lf[2] * lf[3])) + lam_init


def diff_head_out(o, subln, lam_init):
    o = rmsnorm(o, subln) * (1.0 - lam_init)
    return o.reshape(o.shape[:2] + (DIFF_WIDTH,))


def diff_attn_prompt(q, k, v, lam):
    B, S, H, _ = q.shape
    nqb = S // Q_BLOCK
    scale = DIFF_QK_DIM ** -0.5
    k1, k2 = k[..., :DIFF_QK_DIM], k[..., DIFF_QK_DIM:]
    kpos = jnp.arange(S)
    qb = q.reshape(B, nqb, Q_BLOCK, H, 2 * DIFF_QK_DIM).transpose(1, 0, 2, 3, 4)

    def one_block(args):
        qi, i = args
        qpos = i * Q_BLOCK + jnp.arange(Q_BLOCK)
        mask = kpos[None, :] <= qpos[:, None]
        s1 = jnp.einsum('bqhd,bkhd->bhqk', qi[..., :DIFF_QK_DIM], k1).astype(jnp.float32) * scale
        s2 = jnp.einsum('bqhd,bkhd->bhqk', qi[..., DIFF_QK_DIM:], k2).astype(jnp.float32) * scale
        p = masked_softmax(s1, mask) - lam * masked_softmax(s2, mask)
        return jnp.einsum('bhqk,bkhd->bqhd', p.astype(v.dtype), v)

    out = lax.map(one_block, (qb, jnp.arange(nqb)))
    return out.transpose(1, 0, 2, 3, 4).reshape(B, S, H, DIFF_V_DIM)


def diff_attn_sample(q, k_new, v_new, cache, page_table, layer, lam):
    B, T, H, _ = q.shape
    scale = DIFF_QK_DIM ** -0.5
    q1, q2 = q[..., :DIFF_QK_DIM], q[..., DIFF_QK_DIM:]

    def upd(m, lsum, acc, s, v):
        m_new = jnp.maximum(m, jnp.max(s, axis=-1))
        corr = jnp.exp(m - m_new)
        p = jnp.exp(s - m_new[..., None])
        acc = acc * corr[..., None] + jnp.einsum('bhqk,bkhd->bhqd', p, v.astype(jnp.float32))
        return m_new, lsum * corr + jnp.sum(p, axis=-1), acc

    def attend(carry, k, v, mask):
        m1, l1, a1, m2, l2, a2 = carry
        s1 = jnp.einsum('bqhd,bkhd->bhqk', q1, k[..., :DIFF_QK_DIM]).astype(jnp.float32) * scale
        s2 = jnp.einsum('bqhd,bkhd->bhqk', q2, k[..., DIFF_QK_DIM:]).astype(jnp.float32) * scale
        s1 = jnp.where(mask, s1, NEG)
        s2 = jnp.where(mask, s2, NEG)
        m1, l1, a1 = upd(m1, l1, a1, s1, v)
        m2, l2, a2 = upd(m2, l2, a2, s2, v)
        return (m1, l1, a1, m2, l2, a2)

    def page_step(carry, p):
        rows = cache[layer, page_table[:, p]]
        return attend(carry, rows[:, :, 0], rows[:, :, 1], True), None

    m0 = jnp.full((B, H, T), NEG, jnp.float32)
    l0 = jnp.zeros((B, H, T), jnp.float32)
    a0 = jnp.zeros((B, H, T, DIFF_V_DIM), jnp.float32)
    carry, _ = lax.scan(page_step, (m0, l0, a0, m0, l0, a0), jnp.arange(page_table.shape[1]))
    causal = jnp.arange(T)[None, :] <= jnp.arange(T)[:, None]
    m1, l1, a1, m2, l2, a2 = attend(carry, k_new, v_new, causal)
    o = a1 / l1[..., None] - lam * (a2 / l2[..., None])
    return o.transpose(0, 2, 1, 3).astype(q.dtype)


def compress(rows, w):
    return jnp.einsum('...jhd,jde->...he', rows, w)


def nsa_compressed(qg, kc, vc, qpos):
    nb = kc.shape[1]
    s = jnp.einsum('btkgd,bnkd->btkgn', qg, kc).astype(jnp.float32) * (NSA_HEAD_DIM ** -0.5)
    end = (jnp.arange(nb) + 1) * CMP_BLOCK - 1
    mask = (end[None, :] <= qpos[:, None])[None, :, None, None, :]
    p = masked_softmax(s, mask)
    o = jnp.einsum('btkgn,bnkd->btkgd', p.astype(vc.dtype), vc)
    return o, jnp.sum(p, axis=3)


def nsa_select(p_slc, qpos):
    nb = p_slc.shape[-1]
    blk = jnp.arange(nb)[None, :]
    cur = (qpos // SEL_BLOCK)[:, None]
    valid = blk * SEL_BLOCK <= qpos[:, None]
    forced = (blk == 0) | (blk == cur) | (blk == cur - 1)
    score = jnp.where(valid[None, :, None, :], p_slc + FORCE_BONUS * forced[None, :, None, :], NEG)
    top, idx = lax.top_k(score, min(N_SEL, nb))
    return idx, top > 0.5 * NEG


def nsa_selected(qg, ks, vs, idx, sel_valid, qpos):
    B, T, K, G, D = qg.shape
    n = idx.shape[-1]
    s = jnp.einsum('btkgd,btknjd->btkgnj', qg, ks).astype(jnp.float32) * (NSA_HEAD_DIM ** -0.5)
    kpos = idx[..., None] * SEL_BLOCK + jnp.arange(SEL_BLOCK)
    mask = (kpos <= qpos[None, :, None, None, None]) & sel_valid[..., None]
    p = masked_softmax(s.reshape(B, T, K, G, n * SEL_BLOCK), mask.reshape(B, T, K, 1, n * SEL_BLOCK))
    return jnp.einsum('btkgm,btkmd->btkgd', p.astype(vs.dtype), vs.reshape(B, T, K, n * SEL_BLOCK, D))


def nsa_selected_prompt(qg, k_slc, v_slc, idx, sel_valid):
    B, S, K, G, D = qg.shape
    nb = S // SEL_BLOCK
    kb = k_slc.reshape(B, nb, SEL_BLOCK, K, D).transpose(0, 3, 1, 2, 4)
    vb = v_slc.reshape(B, nb, SEL_BLOCK, K, D).transpose(0, 3, 1, 2, 4)
    nc = S // SEL_Q_CHUNK
    bi = jnp.arange(B)[:, None, None, None]
    hi = jnp.arange(K)[None, None, :, None]

    def chunk(args):
        qc, ic, vc_, c = args
        qpos = c * SEL_Q_CHUNK + jnp.arange(SEL_Q_CHUNK)
        return nsa_selected(qc, kb[bi, hi, ic], vb[bi, hi, ic], ic, vc_, qpos)

    split = lambda a: a.reshape((B, nc, SEL_Q_CHUNK) + a.shape[2:]).swapaxes(0, 1)
    out = lax.map(chunk, (split(qg), split(idx), split(sel_valid), jnp.arange(nc)))
    return out.swapaxes(0, 1).reshape(B, S, K, G, D)


def window_attend(qg, k, v, qpos, kpos):
    s = jnp.einsum('btkgd,blkd->btkgl', qg, k).astype(jnp.float32) * (NSA_HEAD_DIM ** -0.5)
    dist = qpos[:, None] - kpos[None, :]
    mask = ((dist >= 0) & (dist <= WINDOW) & (kpos[None, :] >= 0))[None, :, None, None, :]
    p = masked_softmax(s, mask)
    return jnp.einsum('btkgl,blkd->btkgd', p.astype(v.dtype), v)


def window_prompt(qg, k, v):
    B, S, K, G, D = qg.shape
    pad = ((0, 0), (WINDOW, 0), (0, 0), (0, 0))
    kp, vp = jnp.pad(k, pad), jnp.pad(v, pad)
    span = WINDOW + Q_BLOCK

    def blk(i):
        s0 = i * Q_BLOCK
        qb = lax.dynamic_slice_in_dim(qg, s0, Q_BLOCK, axis=1)
        kb = lax.dynamic_slice_in_dim(kp, s0, span, axis=1)
        vb = lax.dynamic_slice_in_dim(vp, s0, span, axis=1)
        return window_attend(qb, kb, vb, s0 + jnp.arange(Q_BLOCK), s0 - WINDOW + jnp.arange(span))

    out = lax.map(blk, jnp.arange(S // Q_BLOCK))
    return out.swapaxes(0, 1).reshape(B, S, K, G, D)


def nsa_combine(ng, o_cmp, o_slc, o_win):
    o = ng[:, :, 0, ..., None] * o_cmp + ng[:, :, 1, ..., None] * o_slc + ng[:, :, 2, ..., None] * o_win
    return o.reshape(o.shape[:2] + (NSA_WIDTH,))


def pad_to_blocks(rows):
    B, T = rows.shape[:2]
    nbn = -(-T // CMP_BLOCK)
    rows = jnp.pad(rows, ((0, 0), (0, nbn * CMP_BLOCK - T)) + ((0, 0),) * (rows.ndim - 2))
    return rows.reshape((B, nbn, CMP_BLOCK) + rows.shape[2:])


def past_compressed(cache, page_table, layer, w_cmp):
    B, n_pages = page_table.shape
    bpp = PAGE_SIZE // CMP_BLOCK

    def one(p):
        rows = cache[layer, page_table[:, p], :, :2]
        rows = rows.reshape(B, bpp, CMP_BLOCK, 2, NSA_KV_HEADS, NSA_HEAD_DIM)
        return compress(rows[:, :, :, 0], w_cmp[0]), compress(rows[:, :, :, 1], w_cmp[1])

    kc, vc = lax.map(one, jnp.arange(n_pages))
    fix = lambda a: a.swapaxes(0, 1).reshape(B, n_pages * bpp, NSA_KV_HEADS, NSA_HEAD_DIM)
    return fix(kc), fix(vc)


def gather_selected_sample(cache, page_table, layer, slot, new_rows, idx):
    B, T, K, n = idx.shape
    bi = jnp.arange(B)[:, None, None, None, None]
    hi = jnp.arange(K)[None, None, :, None, None]
    kpos = idx[..., None] * SEL_BLOCK + jnp.arange(SEL_BLOCK)
    pp = jnp.clip(kpos, 0, PAST_LEN - 1)
    page = page_table[bi, pp // PAGE_SIZE]
    past = cache[layer, page, pp % PAGE_SIZE, slot, hi]
    nb_rows = pad_to_blocks(new_rows)
    nb_rows = nb_rows.reshape((B, -1) + nb_rows.shape[3:])
    npos = jnp.clip(kpos - PAST_LEN, 0, nb_rows.shape[1] - 1)
    new = nb_rows[bi, npos, hi]
    return jnp.where((kpos >= PAST_LEN)[..., None], new.astype(past.dtype), past)


def merge_branches(diff, nsa, mg, w_bd, w_bn, w_o):
    a = jnp.einsum('btc,cd->btd', diff, w_bd)
    b = jnp.einsum('btc,cd->btd', nsa, w_bn)
    return jnp.einsum('btd,de->bte', mg[:, :, 0] * a + mg[:, :, 1] * b, w_o)


def conv_ffn(h, buf, w_up, conv_w, conv_b, w_down):
    T = h.shape[1]
    u = jnp.einsum('btd,df->btf', h, w_up)
    ext = jnp.concatenate([buf.astype(u.dtype), u], axis=1)
    c = sum(ext[:, j:j + T] * conv_w[j] for j in range(CONV_W)) + conv_b
    a, g = jnp.split(c, 2, axis=-1)
    y = jnp.einsum('btf,fd->btd', jax.nn.gelu(a, approximate=True) * g, w_down)
    return y, ext[:, -(CONV_W - 1):]


def mixers_prompt(h, w_in, lam, lam_init, subln, w_cmp):
    B, S, _ = h.shape
    pos = jnp.arange(S, dtype=jnp.int32)
    dq, dk, dv, nq, nkv, ng, mg = project_inputs(h, w_in, pos)
    diff = diff_head_out(diff_attn_prompt(dq, dk, dv, lam), subln, lam_init)
    qg = nq.reshape(B, S, NSA_KV_HEADS, NSA_GROUP, NSA_HEAD_DIM)
    blocks = nkv[:, :, :2].reshape(B, S // CMP_BLOCK, CMP_BLOCK, 2, NSA_KV_HEADS, NSA_HEAD_DIM)
    kc = compress(blocks[:, :, :, 0], w_cmp[0])
    vc = compress(blocks[:, :, :, 1], w_cmp[1])
    o_cmp, p_slc = nsa_compressed(qg, kc, vc, pos)
    idx, sel_valid = nsa_select(p_slc, pos)
    o_slc = nsa_selected_prompt(qg, nkv[:, :, 2], nkv[:, :, 3], idx, sel_valid)
    o_win = window_prompt(qg, nkv[:, :, 4], nkv[:, :, 5])
    nsa = nsa_combine(ng, o_cmp, o_slc, o_win)
    wb = min(WINDOW, S)
    return diff, nsa, mg, jnp.stack([dk, dv], axis=2), nkv[:, :, :4], nkv[:, S - wb:, 4:]


def mixers_sample(h, cache_diff, cache_nsa, win_buf, page_table, layer, w_in, lam, lam_init, subln, w_cmp):
    B, T, _ = h.shape
    pos = PAST_LEN + jnp.arange(T, dtype=jnp.int32)
    dq, dk, dv, nq, nkv, ng, mg = project_inputs(h, w_in, pos)
    diff = diff_head_out(diff_attn_sample(dq, dk, dv, cache_diff, page_table, layer, lam), subln, lam_init)
    qg = nq.reshape(B, T, NSA_KV_HEADS, NSA_GROUP, NSA_HEAD_DIM)
    kc_p, vc_p = past_compressed(cache_nsa, page_table, layer, w_cmp)
    new_blocks = pad_to_blocks(nkv[:, :, :2])
    kc = jnp.concatenate([kc_p.astype(nkv.dtype), compress(new_blocks[:, :, :, 0], w_cmp[0])], axis=1)
    vc = jnp.concatenate([vc_p.astype(nkv.dtype), compress(new_blocks[:, :, :, 1], w_cmp[1])], axis=1)
    o_cmp, p_slc = nsa_compressed(qg, kc, vc, pos)
    idx, sel_valid = nsa_select(p_slc, pos)
    ks = gather_selected_sample(cache_nsa, page_table, layer, 2, nkv[:, :, 2], idx)
    vs = gather_selected_sample(cache_nsa, page_table, layer, 3, nkv[:, :, 3], idx)
    o_slc = nsa_selected(qg, ks, vs, idx, sel_valid, pos)
    wb = win_buf.shape[1]
    ext = jnp.concatenate([win_buf.astype(nkv.dtype), nkv[:, :, 4:]], axis=1)
    kpos = jnp.concatenate([PAST_LEN - wb + jnp.arange(wb, dtype=jnp.int32), pos])
    o_win = window_attend(qg, ext[:, :, 0], ext[:, :, 1], pos, kpos)
    nsa = nsa_combine(ng, o_cmp, o_slc, o_win)
    return diff, nsa, mg, jnp.stack([dk, dv], axis=2), nkv[:, :, :4], ext[:, -wb:]


def setup_inputs(seed: int = 0) -> dict:
    key = jax.random.key(seed)
    ks = jax.random.split(key, 24)
    f32 = jnp.float32
    n_pages = PAST_LEN // PAGE_SIZE
    n_used = DEC_BATCH * n_pages
    n_pool = n_used + n_used // 4
    wb = min(WINDOW, PAST_LEN)
    nrm = lambda k, shape, scale: jax.random.normal(k, shape, f32) * scale
    gain = lambda k, shape: 1.0 + 0.05 * jax.random.normal(k, shape, f32)
    page_table = jax.random.permutation(ks[6], n_pool)[:n_used].reshape(DEC_BATCH, n_pages).astype(jnp.int32)
    return {
        'x_prompt': nrm(ks[0], (BATCH, SEQ, D_MODEL), 1.0),
        'x_sample': nrm(ks[1], (DEC_BATCH, DEC_SEQ, D_MODEL), 1.0),
        'cache_diff_kv': nrm(ks[2], (DEPTH, n_pool, PAGE_SIZE, 2, DIFF_HEADS, 2 * DIFF_QK_DIM), 1.0),
        'cache_nsa_kv': nrm(ks[3], (DEPTH, n_pool, PAGE_SIZE, 4, NSA_KV_HEADS, NSA_HEAD_DIM), 1.0),
        'state_nsa_win': nrm(ks[4], (DEPTH, DEC_BATCH, wb, 2, NSA_KV_HEADS, NSA_HEAD_DIM), 1.0),
        'state_ffn_conv': nrm(ks[5], (DEPTH, DEC_BATCH, CONV_W - 1, 2 * D_FF), 1.0),
        'page_table': page_table,
        'norm_mix_pre': gain(ks[7], (DEPTH, D_MODEL)),
        'norm_mix_post': gain(ks[8], (DEPTH, D_MODEL)),
        'w_in': nrm(ks[9], (DEPTH, D_MODEL, N_IN), D_MODEL ** -0.5),
        'diff_lambda': nrm(ks[10], (DEPTH, 4, DIFF_QK_DIM), 0.1),
        'diff_subln': gain(ks[11], (DEPTH, DIFF_V_DIM)),
        'nsa_w_cmp': nrm(ks[12], (DEPTH, 2, CMP_BLOCK, NSA_HEAD_DIM, NSA_HEAD_DIM), (CMP_BLOCK * NSA_HEAD_DIM) ** -0.5),
        'w_branch_diff': nrm(ks[13], (DEPTH, DIFF_WIDTH, D_MODEL), DIFF_WIDTH ** -0.5),
        'w_branch_nsa': nrm(ks[14], (DEPTH, NSA_WIDTH, D_MODEL), NSA_WIDTH ** -0.5),
        'w_out': nrm(ks[15], (DEPTH, D_MODEL, D_MODEL), D_MODEL ** -0.5),
        'norm_ffn_pre': gain(ks[16], (DEPTH, D_MODEL)),
        'norm_ffn_post': gain(ks[17], (DEPTH, D_MODEL)),
        'ffn_w_up': nrm(ks[18], (DEPTH, D_MODEL, 2 * D_FF), D_MODEL ** -0.5),
        'ffn_conv_w': nrm(ks[19], (DEPTH, CONV_W, 2 * D_FF), CONV_W ** -0.5),
        'ffn_conv_b': nrm(ks[20], (DEPTH, 2 * D_FF), 0.02),
        'ffn_w_down': nrm(ks[21], (DEPTH, D_FF, D_MODEL), D_FF ** -0.5),
    }


def reference(x_prompt, x_sample, cache_diff_kv, cache_nsa_kv, state_nsa_win, state_ffn_conv, page_table,
              norm_mix_pre, norm_mix_post, w_in, diff_lambda, diff_subln, nsa_w_cmp,
              w_branch_diff, w_branch_nsa, w_out, norm_ffn_pre, norm_ffn_post,
              ffn_w_up, ffn_conv_w, ffn_conv_b, ffn_w_down):
    xp, xs = x_prompt, x_sample
    d_p, d_s, n_p, n_s, w_p, w_s, c_p, c_s = [], [], [], [], [], [], [], []
    for layer in range(DEPTH):
        lam_init = 0.8 - 0.6 * math.exp(-0.3 * layer)
        lam = diff_lambda_value(diff_lambda[layer], lam_init)
        diff, nsa, mg, rd, rn, rw = mixers_prompt(rmsnorm(xp, norm_mix_pre[layer]), w_in[layer], lam, lam_init,
                                                  diff_subln[layer], nsa_w_cmp[layer])
        mix = merge_branches(diff, nsa, mg, w_branch_diff[layer], w_branch_nsa[layer], w_out[layer])
        xp = xp + rmsnorm(mix, norm_mix_post[layer])
        conv0 = jnp.zeros((xp.shape[0], CONV_W - 1, 2 * D_FF), xp.dtype)
        f, rc = conv_ffn(rmsnorm(xp, norm_ffn_pre[layer]), conv0, ffn_w_up[layer], ffn_conv_w[layer],
                         ffn_conv_b[layer], ffn_w_down[layer])
        xp = xp + rmsnorm(f, norm_ffn_post[layer])
        d_p.append(rd); n_p.append(rn); w_p.append(rw); c_p.append(rc)
        diff, nsa, mg, rd, rn, rw = mixers_sample(rmsnorm(xs, norm_mix_pre[layer]), cache_diff_kv, cache_nsa_kv,
                                                  state_nsa_win[layer], page_table, layer, w_in[layer], lam,
                                                  lam_init, diff_subln[layer], nsa_w_cmp[layer])
        mix = merge_branches(diff, nsa, mg, w_branch_diff[layer], w_branch_nsa[layer], w_out[layer])
        xs = xs + rmsnorm(mix, norm_mix_post[layer])
        f, rc = conv_ffn(rmsnorm(xs, norm_ffn_pre[layer]), state_ffn_conv[layer], ffn_w_up[layer],
                         ffn_conv_w[layer], ffn_conv_b[layer], ffn_w_down[layer])
        xs = xs + rmsnorm(f, norm_ffn_post[layer])
        d_s.append(rd); n_s.append(rn); w_s.append(rw); c_s.append(rc)
    y_prompt, y_sample = xp, xs
    new_diff_kv_prompt, new_diff_kv_sample = jnp.stack(d_p), jnp.stack(d_s)
    new_nsa_kv_prompt, new_nsa_kv_sample = jnp.stack(n_p), jnp.stack(n_s)
    new_win_prompt, new_win_sample = jnp.stack(w_p), jnp.stack(w_s)
    new_conv_prompt, new_conv_sample = jnp.stack(c_p), jnp.stack(c_s)
    return (y_prompt, y_sample, new_diff_kv_prompt, new_diff_kv_sample, new_nsa_kv_prompt, new_nsa_kv_sample,
            new_win_prompt, new_win_sample, new_conv_prompt, new_conv_sample)
```

```python
import functools
import math

import jax
import jax.numpy as jnp
from jax import lax
from jax.experimental import pallas as pl
from jax.experimental.pallas import tpu as pltpu

F32 = jnp.float32
BF16 = jnp.bfloat16

DIFF_HEADS = 8
DIFF_QK_DIM = 64
DIFF_V_DIM = 128
NSA_HEADS = 8
NSA_KV_HEADS = 2
NSA_GROUP = 4
NSA_HEAD_DIM = 128
CMP_BLOCK = 64
CMP_SHIFT = 6
N_SEL = 16
WINDOW = 512
PAGE_SIZE = 128
CONV_W = 3
FORCE_BONUS = 1e4
ROPE_THETA = 10000.0
EPS = 1e-6
NEG = -1e30
TINY = 1e-30
LANES = 128
VMEM_LIMIT = 52 * 1024 * 1024

_NT = (((1,), (1,)), ((), ()))


def _nt(a, b):
    return lax.dot_general(a, b, _NT, preferred_element_type=F32)


def _mm(a, b):
    return jnp.dot(a, b, preferred_element_type=F32)


def _cp(sem):
    return pltpu.CompilerParams(dimension_semantics=sem, vmem_limit_bytes=VMEM_LIMIT)


def _pick(n, cands):
    for c in cands:
        if n % c == 0:
            return c
    return n


def _rms(x, g):
    return x * lax.rsqrt(jnp.mean(x * x, axis=-1, keepdims=True) + EPS) * g


def _softmax_update(carry, s, mask, v):
    m, l, acc = carry
    sm = jnp.where(mask, s, NEG)
    m_new = jnp.maximum(m, jnp.max(sm, axis=-1, keepdims=True))
    corr = jnp.exp(m - m_new)
    p = jnp.where(mask, jnp.exp(sm - m_new), 0.0)
    l = l * corr + jnp.sum(p, axis=-1, keepdims=True)
    acc = acc * corr + _mm(p.astype(BF16), v)
    return m_new, l, acc


def _softmax_init(rows, d):
    return (jnp.full((rows, 1), NEG, F32), jnp.zeros((rows, 1), F32), jnp.zeros((rows, d), F32))


def _lam_kernel(l_ref, o_ref, *, lam_init):
    l = l_ref[...]
    a = jnp.sum(l[0:1] * l[1:2], axis=-1, keepdims=True)
    b = jnp.sum(l[2:3] * l[3:4], axis=-1, keepdims=True)
    o_ref[...] = jnp.broadcast_to(jnp.exp(a) - jnp.exp(b) + lam_init, o_ref.shape)


def _lam_call(lam_params, lam_init):
    out = pl.pallas_call(
        functools.partial(_lam_kernel, lam_init=lam_init),
        out_shape=jax.ShapeDtypeStruct((8, LANES), F32),
        name="diff_lambda",
    )(lam_params)
    return out[0, :1]


def _norm_kernel(x_ref, g_ref, o_ref):
    o_ref[...] = _rms(x_ref[...], g_ref[...]).astype(o_ref.dtype)


def _norm_call(x, g):
    m, d = x.shape
    tm = _pick(m, (512, 256, 128, 64, 32, 16))
    return pl.pallas_call(
        _norm_kernel,
        grid=(m // tm,),
        in_specs=[pl.BlockSpec((tm, d), lambda i: (i, 0)), pl.BlockSpec((1, d), lambda i: (0, 0))],
        out_specs=pl.BlockSpec((tm, d), lambda i: (i, 0)),
        out_shape=jax.ShapeDtypeStruct((m, d), BF16),
        compiler_params=_cp(("parallel",)),
        name="rmsnorm",
    )(x, g.reshape(1, d))


MODE_NONE, MODE_ROPE64, MODE_ROPE128, MODE_SIGMOID = 0, 1, 2, 3


def _rope_tables(pos, d):
    half = d // 2
    inv = 1.0 / (ROPE_THETA ** (jnp.arange(0, d, 2, dtype=F32) / d))
    ang = pos.astype(F32)[:, None] * inv[None, :]
    lane = jnp.arange(LANES)
    cos = jnp.cos(ang)[:, lane % half]
    sin = jnp.sin(ang)[:, lane % half]
    first = ((lane % d) < half)[None, :]
    return cos, jnp.where(first, -sin, 0.0), jnp.where(first, 0.0, sin)


def _apply_rope(z, cos, sa, sb, half):
    outs = []
    for c in range(z.shape[1] // LANES):
        blk = z[:, c * LANES:(c + 1) * LANES]
        outs.append(blk * cos + pltpu.roll(blk, LANES - half, 1) * sa + pltpu.roll(blk, half, 1) * sb)
    return outs[0] if len(outs) == 1 else jnp.concatenate(outs, axis=1)


def _proj_kernel(*refs, runs, n_tab):
    h_ref, w_ref = refs[0], refs[1]
    tabs = refs[2:2 + n_tab]
    o_ref = refs[2 + n_tab]
    j = pl.program_id(1)
    acc = _mm(h_ref[...], w_ref[...])

    def emit(mode):
        if mode == MODE_NONE:
            o_ref[...] = acc.astype(o_ref.dtype)
        elif mode == MODE_SIGMOID:
            o_ref[...] = jax.nn.sigmoid(acc).astype(o_ref.dtype)
        else:
            half = 32 if mode == MODE_ROPE64 else 64
            o_ref[...] = _apply_rope(acc, tabs[0][...], tabs[1][...], tabs[2][...], half).astype(o_ref.dtype)

    if len(runs) == 1:
        emit(runs[0][0])
    else:
        for mode, j0, j1 in runs:
            pl.when((j >= j0) & (j < j1))(functools.partial(emit, mode))


def _proj_call(h, w, modes, tn, out_dtype, tables=(), pos_rows=None, name="proj"):
    m, k = h.shape
    n = w.shape[1]
    assert n % tn == 0 and len(modes) == n // tn
    tm = _pick(m, (1024, 512, 256, 128, 64, 48, 32, 16))
    runs = []
    for jj, md in enumerate(modes):
        if runs and runs[-1][0] == md:
            runs[-1][2] = jj + 1
        else:
            runs.append([md, jj, jj + 1])
    runs = tuple(tuple(r) for r in runs)
    in_specs = [pl.BlockSpec((tm, k), lambda i, j: (i, 0)), pl.BlockSpec((k, tn), lambda i, j: (0, j))]
    if tables:
        nblk = pos_rows // tm
        assert nblk * tm == pos_rows
        in_specs += [pl.BlockSpec((tm, LANES), lambda i, j: (i % nblk, 0)) for _ in tables]
    return pl.pallas_call(
        functools.partial(_proj_kernel, runs=runs, n_tab=len(tables)),
        grid=(m // tm, n // tn),
        in_specs=in_specs,
        out_specs=pl.BlockSpec((tm, tn), lambda i, j: (i, j)),
        out_shape=jax.ShapeDtypeStruct((m, n), out_dtype),
        compiler_params=_cp(("parallel", "arbitrary")),
        name=name,
    )(h, w, *tables)


def _diff_prompt_kernel(lam_ref, q_ref, k_ref, v_ref, g_ref, o_ref, kb_ref, vb_ref, *, tq, tk, out_scale):
    qi = pl.program_id(2)

    @pl.when(qi == 0)
    def _():
        kb_ref[...] = k_ref[...].astype(BF16)
        vb_ref[...] = v_ref[...].astype(BF16)

    q = q_ref[...]
    lane = lax.broadcasted_iota(jnp.int32, q.shape, 1)
    scale = DIFF_QK_DIM ** -0.5
    q1 = jnp.where(lane < DIFF_QK_DIM, q, 0) * jnp.asarray(scale, q.dtype)
    q2 = jnp.where(lane >= DIFF_QK_DIM, q, 0) * jnp.asarray(scale, q.dtype)
    qpos = qi * tq + lax.broadcasted_iota(jnp.int32, (tq, 1), 0)

    def body(kb, carry):
        c1, c2 = carry
        start = pl.multiple_of(kb * tk, tk)
        k = kb_ref[pl.ds(start, tk), :]
        v = vb_ref[pl.ds(start, tk), :]
        mask = (start + lax.broadcasted_iota(jnp.int32, (1, tk), 1)) <= qpos
        c1 = _softmax_update(c1, _nt(q1, k), mask, v)
        c2 = _softmax_update(c2, _nt(q2, k), mask, v)
        return c1, c2

    init = _softmax_init(tq, DIFF_V_DIM)
    n_kb = (qi * tq + tq + tk - 1) // tk
    (m1, l1, a1), (m2, l2, a2) = lax.fori_loop(0, n_kb, body, (init, init))
    o = a1 / l1 - lam_ref[0] * (a2 / l2)
    o_ref[...] = (_rms(o, g_ref[...]) * out_scale).astype(o_ref.dtype)


def _diff_prompt_call(lam, q, kv, subln, b, s, lam_init):
    t = b * s
    tq = _pick(s, (256, 128))
    nq = s // tq
    return pl.pallas_call(
        functools.partial(_diff_prompt_kernel, tq=tq, tk=tq, out_scale=1.0 - lam_init),
        grid_spec=pltpu.PrefetchScalarGridSpec(
            num_scalar_prefetch=1,
            grid=(b, DIFF_HEADS, nq),
            in_specs=[
                pl.BlockSpec((tq, LANES), lambda bi, h, qi, lam: (bi * nq + qi, h)),
                pl.BlockSpec((s, LANES), lambda bi, h, qi, lam: (bi, h)),
                pl.BlockSpec((s, LANES), lambda bi, h, qi, lam: (bi, DIFF_HEADS + h)),
                pl.BlockSpec((1, LANES), lambda bi, h, qi, lam: (0, 0)),
            ],
            out_specs=pl.BlockSpec((tq, LANES), lambda bi, h, qi, lam: (bi * nq + qi, h)),
            scratch_shapes=[pltpu.VMEM((s, LANES), BF16), pltpu.VMEM((s, LANES), BF16)],
        ),
        out_shape=jax.ShapeDtypeStruct((t, DIFF_HEADS * DIFF_V_DIM), BF16),
        compiler_params=_cp(("parallel", "parallel", "arbitrary")),
        name="diff_attn_prompt",
    )(lam, q, kv, kv, subln.reshape(1, LANES))


def _compress_kernel(a_ref, w_ref, o_ref):
    @pl.when(pl.program_id(2) == 0)
    def _():
        o_ref[...] = jnp.zeros(o_ref.shape, o_ref.dtype)

    o_ref[0, 0] += _mm(a_ref[0, 0].astype(BF16), w_ref[0].astype(BF16))


def _compress_call(blocks, w_cmp):
    _, kvh, r, kk = blocks.shape
    tk = _pick(kk, (2048, 1024, 512))
    return pl.pallas_call(
        _compress_kernel,
        grid=(2, kvh, kk // tk),
        in_specs=[pl.BlockSpec((1, 1, r, tk), lambda s, h, k: (s, h, 0, k)),
                  pl.BlockSpec((1, tk, NSA_HEAD_DIM), lambda s, h, k: (s, k, 0))],
        out_specs=pl.BlockSpec((1, 1, r, NSA_HEAD_DIM), lambda s, h, k: (s, h, 0, 0)),
        out_shape=jax.ShapeDtypeStruct((2, kvh, r, NSA_HEAD_DIM), F32),
        compiler_params=_cp(("parallel", "parallel", "arbitrary")),
        name="nsa_compress_prompt",
    )(blocks, w_cmp)


def _topk_mask(score, blk, n_sel):
    nb = score.shape[1]
    rank = jnp.zeros(score.shape, jnp.int32)
    for mm in range(nb):
        cm = score[:, mm:mm + 1]
        beats = (cm > score) | ((cm == score) & (blk > mm))
        rank = rank + beats.astype(jnp.int32)
    return rank < n_sel


def _nsa_prompt_kernel(q_ref, kc_ref, vc_ref, ks_ref, vs_ref, kw_ref, vw_ref, ng_ref, o_ref,
                       ksb, vsb, kwb, vwb, selk_ref, *, s_len, tq, tk_slc, tk_win):
    kvh = pl.program_id(1)
    qi = pl.program_id(2)
    nb = s_len // CMP_BLOCK
    g = NSA_GROUP
    rows = g * tq
    scale = NSA_HEAD_DIM ** -0.5

    @pl.when(qi == 0)
    def _():
        ksb[...] = ks_ref[...].astype(BF16)
        vsb[...] = vs_ref[...].astype(BF16)
        kwb[...] = kw_ref[...].astype(BF16)
        vwb[...] = vw_ref[...].astype(BF16)

    q = q_ref[...]
    qr = jnp.concatenate([q[:, i * LANES:(i + 1) * LANES] for i in range(g)], axis=0)
    q0 = qi * tq
    tpos = q0 + lax.broadcasted_iota(jnp.int32, (tq, 1), 0)
    qpos = q0 + (lax.broadcasted_iota(jnp.int32, (rows, 1), 0) & (tq - 1))

    kc = kc_ref[0, 0, 0].astype(BF16)
    vc = vc_ref[0, 0, 0].astype(BF16)
    sc = _nt(qr, kc) * scale
    nblk = lax.broadcasted_iota(jnp.int32, (rows, nb), 1)
    cmask = ((nblk + 1) * CMP_BLOCK - 1) <= qpos
    scm = jnp.where(cmask, sc, NEG)
    pc = jnp.where(cmask, jnp.exp(scm - jnp.max(scm, axis=-1, keepdims=True)), 0.0)
    pc = pc / jnp.maximum(jnp.sum(pc, axis=-1, keepdims=True), TINY)
    o_cmp = _mm(pc.astype(BF16), vc)
    p_slc = pc[0:tq]
    for i in range(1, g):
        p_slc = p_slc + pc[i * tq:(i + 1) * tq]

    blk = lax.broadcasted_iota(jnp.int32, (tq, nb), 1)
    cur = lax.shift_right_logical(tpos, CMP_SHIFT)
    valid = blk * CMP_BLOCK <= tpos
    forced = (blk == 0) | (blk == cur) | (blk == cur - 1)
    score = jnp.where(valid, p_slc + jnp.where(forced, FORCE_BONUS, 0.0), NEG)
    sel = _topk_mask(score, blk, min(N_SEL, nb)) & valid
    expand = (lax.shift_right_logical(lax.broadcasted_iota(jnp.int32, (nb, s_len), 1), CMP_SHIFT)
              == lax.broadcasted_iota(jnp.int32, (nb, s_len), 0))
    selk_ref[...] = _mm(sel.astype(BF16), expand.astype(BF16))

    def slc_body(kb, carry):
        start = pl.multiple_of(kb * tk_slc, tk_slc)
        k = ksb[pl.ds(start, tk_slc), :]
        v = vsb[pl.ds(start, tk_slc), :]
        kpos = start + lax.broadcasted_iota(jnp.int32, (1, tk_slc), 1)
        sel_rows = jnp.concatenate([selk_ref[:, pl.ds(start, tk_slc)]] * g, axis=0)
        mask = (sel_rows > 0.5) & (kpos <= qpos)
        return _softmax_update(carry, _nt(qr, k) * scale, mask, v)

    n_slc = (q0 + tq + tk_slc - 1) // tk_slc
    _, l_s, a_s = lax.fori_loop(0, n_slc, slc_body, _softmax_init(rows, NSA_HEAD_DIM))
    o_slc = a_s / jnp.maximum(l_s, TINY)

    def win_body(kb, carry):
        start = pl.multiple_of(kb * tk_win, tk_win)
        k = kwb[pl.ds(start, tk_win), :]
        v = vwb[pl.ds(start, tk_win), :]
        dist = qpos - (start + lax.broadcasted_iota(jnp.int32, (1, tk_win), 1))
        mask = (dist >= 0) & (dist <= WINDOW)
        return _softmax_update(carry, _nt(qr, k) * scale, mask, v)

    lo = jnp.maximum(q0 - WINDOW, 0) // tk_win
    hi = (q0 + tq + tk_win - 1) // tk_win
    _, l_w, a_w = lax.fori_loop(lo, hi, win_body, _softmax_init(rows, NSA_HEAD_DIM))
    o_win = a_w / jnp.maximum(l_w, TINY)

    ng = ng_ref[...]
    for i in range(g):
        def gate(r):
            c0 = r * NSA_HEADS + i
            c1 = c0 + NSA_GROUP
            return jnp.where(kvh == 0, ng[:, c0:c0 + 1], ng[:, c1:c1 + 1])
        sl = slice(i * tq, (i + 1) * tq)
        o = gate(0) * o_cmp[sl] + gate(1) * o_slc[sl] + gate(2) * o_win[sl]
        o_ref[:, i * LANES:(i + 1) * LANES] = o.astype(o_ref.dtype)


def _nsa_prompt_call(nq, kcvc, nkv_cs, nkv_win, ng, b, s):
    t = b * s
    tq = 128
    nqb = s // tq
    nb = s // CMP_BLOCK
    gw = NSA_GROUP * LANES
    kcvc5 = kcvc.reshape(2, NSA_KV_HEADS, b, nb, NSA_HEAD_DIM)
    kv_spec = lambda col: pl.BlockSpec((s, LANES), lambda bi, h, qi: (bi, col + h))
    return pl.pallas_call(
        functools.partial(_nsa_prompt_kernel, s_len=s, tq=tq, tk_slc=_pick(s, (512, 256, 128)), tk_win=128),
        grid=(b, NSA_KV_HEADS, nqb),
        in_specs=[
            pl.BlockSpec((tq, gw), lambda bi, h, qi: (bi * nqb + qi, h)),
            pl.BlockSpec((1, 1, 1, nb, NSA_HEAD_DIM), lambda bi, h, qi: (0, h, bi, 0, 0)),
            pl.BlockSpec((1, 1, 1, nb, NSA_HEAD_DIM), lambda bi, h, qi: (1, h, bi, 0, 0)),
            kv_spec(4), kv_spec(6),
            kv_spec(0), kv_spec(2),
            pl.BlockSpec((tq, LANES), lambda bi, h, qi: (bi * nqb + qi, 0)),
        ],
        out_specs=pl.BlockSpec((tq, gw), lambda bi, h, qi: (bi * nqb + qi, h)),
        out_shape=jax.ShapeDtypeStruct((t, NSA_HEADS * NSA_HEAD_DIM), BF16),
        scratch_shapes=[pltpu.VMEM((s, LANES), BF16)] * 4 + [pltpu.VMEM((tq, s), F32)],
        compiler_params=_cp(("parallel", "parallel", "arbitrary")),
        name="nsa_attn_prompt",
    )(nq, kcvc5, kcvc5, nkv_cs, nkv_cs, nkv_win, nkv_win, ng)


def _merge_kernel(h_ref, d_ref, n_ref, wg0_ref, wg1_ref, wbd_ref, wbn_ref, o_ref):
    h = h_ref[...]
    a = _mm(d_ref[...], wbd_ref[...])
    bb = _mm(n_ref[...], wbn_ref[...])
    g0 = jax.nn.sigmoid(_mm(h, wg0_ref[...]))
    g1 = jax.nn.sigmoid(_mm(h, wg1_ref[...]))
    o_ref[...] = (g0 * a + g1 * bb).astype(o_ref.dtype)


def _merge_call(h, diff, nsa, w_mg, w_bd, w_bn):
    m, d = h.shape
    tm = _pick(m, (1024, 512, 256, 128, 64, 32))
    tn = _pick(d, (256, 128))
    nj = d // tn
    row = lambda kk: pl.BlockSpec((tm, kk), lambda i, j: (i, 0))
    return pl.pallas_call(
        _merge_kernel,
        grid=(m // tm, nj),
        in_specs=[row(d), row(diff.shape[1]), row(nsa.shape[1]),
                  pl.BlockSpec((d, tn), lambda i, j: (0, j)),
                  pl.BlockSpec((d, tn), lambda i, j: (0, nj + j)),
                  pl.BlockSpec((w_bd.shape[0], tn), lambda i, j: (0, j)),
                  pl.BlockSpec((w_bn.shape[0], tn), lambda i, j: (0, j))],
        out_specs=pl.BlockSpec((tm, tn), lambda i, j: (i, j)),
        out_shape=jax.ShapeDtypeStruct((m, d), BF16),
        compiler_params=_cp(("parallel", "arbitrary")),
        name="branch_merge",
    )(h, diff, nsa, w_mg, w_mg, w_bd, w_bn)


def _out_proj_kernel(m_ref, w_ref, x_ref, g_ref, o_ref):
    y = _mm(m_ref[...], w_ref[...])
    o_ref[...] = x_ref[...] + _rms(y, g_ref[...])


def _out_proj_call(mix, w_o, x, g):
    m, d = x.shape
    tm = _pick(m, (512, 256, 128, 64, 32))
    return pl.pallas_call(
        _out_proj_kernel,
        grid=(m // tm,),
        in_specs=[pl.BlockSpec((tm, d), lambda i: (i, 0)), pl.BlockSpec((d, d), lambda i: (0, 0)),
                  pl.BlockSpec((tm, d), lambda i: (i, 0)), pl.BlockSpec((1, d), lambda i: (0, 0))],
        out_specs=pl.BlockSpec((tm, d), lambda i: (i, 0)),
        out_shape=jax.ShapeDtypeStruct((m, d), F32),
        compiler_params=_cp(("parallel",)),
        name="out_proj_norm_residual",
    )(mix, w_o, x, g.reshape(1, d))


HALO = 16


def _gelu_glu(ca, cg):
    return jax.nn.gelu(ca, approximate=True) * cg


def _ffn_prompt_kernel(x_ref, xh_ref, gpre_ref, gpost_ref, wa_ref, wg_ref, cwa_ref, cwg_ref, ba_ref, bg_ref,
                       wd_ref, o_ref, h_sc, ua_sc, ug_sc, acc_sc, *, tm, blocks_per_seq):
    i = pl.program_id(0)
    j = pl.program_id(1)

    @pl.when(j == 0)
    def _():
        h_sc[0:HALO, :] = _rms(xh_ref[...], gpre_ref[...]).astype(BF16)
        h_sc[HALO:, :] = _rms(x_ref[...], gpre_ref[...]).astype(BF16)
        acc_sc[...] = jnp.zeros(acc_sc.shape, F32)

    first = (i % blocks_per_seq) == 0
    h = h_sc[...]
    keep = jnp.logical_not(first & (lax.broadcasted_iota(jnp.int32, (tm + HALO, 1), 0) < HALO))
    ua_sc[...] = jnp.where(keep, _mm(h, wa_ref[...]), 0.0)
    ug_sc[...] = jnp.where(keep, _mm(h, wg_ref[...]), 0.0)

    def conv(u_sc, cw_ref, b_ref):
        cw = cw_ref[...]
        out = b_ref[...]
        for tap in range(CONV_W):
            out = out + u_sc[pl.ds(HALO - (CONV_W - 1) + tap, tm), :] * cw[tap:tap + 1]
        return out

    act = _gelu_glu(conv(ua_sc, cwa_ref, ba_ref), conv(ug_sc, cwg_ref, bg_ref))
    acc_sc[...] += _mm(act.astype(BF16), wd_ref[...])

    @pl.when(j == pl.num_programs(1) - 1)
    def _():
        o_ref[...] = x_ref[...] + _rms(acc_sc[...], gpost_ref[...])


def _ffn_prompt_call(x, g_pre, g_post, wa, wg, cwa, cwg, ba, bg, wd, s):
    m, d = x.shape
    fp = wa.shape[1]
    tm = _pick(s, (512, 256, 128))
    tf = _pick(fp, (512, 256, 128))
    hb = tm // HALO
    col = lambda r: pl.BlockSpec((r, tf), lambda i, j: (0, j))
    return pl.pallas_call(
        functools.partial(_ffn_prompt_kernel, tm=tm, blocks_per_seq=s // tm),
        grid=(m // tm, fp // tf),
        in_specs=[pl.BlockSpec((tm, d), lambda i, j: (i, 0)),
                  pl.BlockSpec((HALO, d), lambda i, j: (jnp.maximum(i * hb - 1, 0), 0)),
                  pl.BlockSpec((1, d), lambda i, j: (0, 0)), pl.BlockSpec((1, d), lambda i, j: (0, 0)),
                  col(d), col(d), col(CONV_W), col(CONV_W), col(1), col(1),
                  pl.BlockSpec((tf, d), lambda i, j: (j, 0))],
        out_specs=pl.BlockSpec((tm, d), lambda i, j: (i, 0)),
        out_shape=jax.ShapeDtypeStruct((m, d), F32),
        scratch_shapes=[pltpu.VMEM((tm + HALO, d), BF16), pltpu.VMEM((tm + HALO, tf), F32),
                        pltpu.VMEM((tm + HALO, tf), F32), pltpu.VMEM((tm, d), F32)],
        compiler_params=_cp(("parallel", "arbitrary")),
        name="conv_ffn_prompt",
    )(x, x, g_pre.reshape(1, d), g_post.reshape(1, d), wa, wg, cwa, cwg, ba, bg, wd)


def _ffn_sample_kernel(ua_ref, ug_ref, bufa_ref, bufg_ref, cwa_ref, cwg_ref, ba_ref, bg_ref, wd_ref, x_ref,
                       gpost_ref, o_ref, acc_sc):
    j = pl.program_id(0)

    @pl.when(j == 0)
    def _():
        acc_sc[...] = jnp.zeros(acc_sc.shape, F32)

    def conv(u_ref, buf_ref, cw_ref, b_ref):
        cw = cw_ref[...]
        out = b_ref[...] + u_ref[...] * cw[CONV_W - 1:CONV_W]
        for tap in range(CONV_W - 1):
            out = out + buf_ref[tap] * cw[tap:tap + 1]
        return out

    act = _gelu_glu(conv(ua_ref, bufa_ref, cwa_ref, ba_ref), conv(ug_ref, bufg_ref, cwg_ref, bg_ref))
    acc_sc[...] += _mm(act.astype(BF16), wd_ref[...])

    @pl.when(j == pl.num_programs(0) - 1)
    def _():
        o_ref[...] = x_ref[...] + _rms(acc_sc[...], gpost_ref[...])


def _ffn_sample_call(u, bufa, bufg, cwa, cwg, ba, bg, wd, x, g_post):
    bsz, d = x.shape
    fp = wd.shape[0]
    tf = _pick(fp, (512, 256, 128))
    nf = fp // tf
    col = lambda r: pl.BlockSpec((r, tf), lambda j: (0, j))
    return pl.pallas_call(
        _ffn_sample_kernel,
        grid=(nf,),
        in_specs=[pl.BlockSpec((bsz, tf), lambda j: (0, j)), pl.BlockSpec((bsz, tf), lambda j: (0, nf + j)),
                  pl.BlockSpec((CONV_W - 1, bsz, tf), lambda j: (0, 0, j)),
                  pl.BlockSpec((CONV_W - 1, bsz, tf), lambda j: (0, 0, j)),
                  col(CONV_W), col(CONV_W), col(1), col(1),
                  pl.BlockSpec((tf, d), lambda j: (j, 0)),
                  pl.BlockSpec((bsz, d), lambda j: (0, 0)), pl.BlockSpec((1, d), lambda j: (0, 0))],
        out_specs=pl.BlockSpec((bsz, d), lambda j: (0, 0)),
        out_shape=jax.ShapeDtypeStruct((bsz, d), F32),
        scratch_shapes=[pltpu.VMEM((bsz, d), F32)],
        compiler_params=_cp(("arbitrary",)),
        name="conv_ffn_sample",
    )(u, u, bufa, bufg, cwa, cwg, ba, bg, wd, x, g_post.reshape(1, d))


def _diff_decode_kernel(pt_ref, lam_ref, q_ref, kvn_ref, g_ref, *rest, n_pp, out_scale):
    page_refs = rest[:n_pp]
    o_ref = rest[n_pp]
    q_sc, m_sc, l_sc, acc_sc = rest[n_pp + 1:]
    p = pl.program_id(1)
    nh = DIFF_HEADS
    hw = nh * LANES
    row = lax.broadcasted_iota(jnp.int32, (8, LANES), 0)
    lane = lax.broadcasted_iota(jnp.int32, (8, LANES), 1)
    qsel = ((row == 0) & (lane < DIFF_QK_DIM)) | ((row == 1) & (lane >= DIFF_QK_DIM))

    @pl.when(p == 0)
    def _():
        for h in range(nh):
            qh = jnp.broadcast_to(q_ref[0, h:h + 1, :], (8, LANES))
            q_sc[h] = (jnp.where(qsel, qh, 0.0) * (DIFF_QK_DIM ** -0.5)).astype(BF16).astype(F32)
        m_sc[...] = jnp.full(m_sc.shape, NEG, F32)
        l_sc[...] = jnp.zeros(l_sc.shape, F32)
        acc_sc[...] = jnp.zeros(acc_sc.shape, F32)

    def update(h, s, pv):
        m_old = m_sc[h]
        m_new = jnp.maximum(m_old, jnp.max(s, axis=-1, keepdims=True))
        corr = jnp.exp(m_old - m_new)
        pe = jnp.exp(s - m_new[:, 0:1])
        m_sc[h] = m_new
        l_sc[h] = l_sc[h] * corr + jnp.sum(pe, axis=-1, keepdims=True)
        acc_sc[h] = acc_sc[h] * corr + pv(pe)

    for pr in page_refs:
        for h in range(nh):
            kh = pr[0, :, h * LANES:(h + 1) * LANES].astype(BF16)
            vh = pr[0, :, hw + h * LANES:hw + (h + 1) * LANES].astype(BF16)
            update(h, _nt(q_sc[h].astype(BF16), kh), lambda pe, vh=vh: _mm(pe.astype(BF16), vh))

    @pl.when(p == pl.num_programs(1) - 1)
    def _():
        lam = lam_ref[0]
        for h in range(nh):
            kn = kvn_ref[0, h:h + 1, :].astype(BF16).astype(F32)
            vn = kvn_ref[0, nh + h:nh + h + 1, :].astype(BF16).astype(F32)
            s = jnp.sum(q_sc[h] * kn, axis=-1, keepdims=True)
            update(h, s, lambda pe, vn=vn: pe.astype(BF16).astype(F32) * vn)
            o = acc_sc[h] / l_sc[h]
            od = o[0:1] - lam * o[1:2]
            o_ref[0, h:h + 1, :] = _rms(od, g_ref[...]) * out_scale


def _diff_decode_call(page_table, lam, q, kv_new, subln, cache, lam_init, n_pp):
    bsz, n_pages = page_table.shape
    assert n_pages % n_pp == 0
    wide = cache.shape[2]
    page_spec = lambda kk: pl.BlockSpec((1, PAGE_SIZE, wide),
                                        lambda b, p, pt, lam_: (pt[b, p * n_pp + kk], 0, 0))
    st = pltpu.VMEM((DIFF_HEADS, 8, LANES), F32)
    return pl.pallas_call(
        functools.partial(_diff_decode_kernel, n_pp=n_pp, out_scale=1.0 - lam_init),
        grid_spec=pltpu.PrefetchScalarGridSpec(
            num_scalar_prefetch=2,
            grid=(bsz, n_pages // n_pp),
            in_specs=[pl.BlockSpec((1, DIFF_HEADS, LANES), lambda b, p, pt, lam_: (b, 0, 0)),
                      pl.BlockSpec((1, 2 * DIFF_HEADS, LANES), lambda b, p, pt, lam_: (b, 0, 0)),
                      pl.BlockSpec((1, LANES), lambda b, p, pt, lam_: (0, 0))]
                     + [page_spec(kk) for kk in range(n_pp)],
            out_specs=pl.BlockSpec((1, DIFF_HEADS, LANES), lambda b, p, pt, lam_: (b, 0, 0)),
            scratch_shapes=[st, st, st, st],
        ),
        out_shape=jax.ShapeDtypeStruct((bsz, DIFF_HEADS, LANES), F32),
        compiler_params=_cp(("parallel", "arbitrary")),
        name="diff_attn_sample",
    )(page_table, lam, q, kv_new, subln.reshape(1, LANES), *([cache] * n_pp))


def _compress_pages_kernel(pt_ref, w_ref, *rest, n_pp):
    ncol = 2 * NSA_KV_HEADS
    page_refs = rest[:n_pp * ncol]
    o_ref = rest[n_pp * ncol]
    x_sc = rest[n_pp * ncol + 1]
    bpp = PAGE_SIZE // CMP_BLOCK
    d = NSA_HEAD_DIM
    rows = n_pp * bpp

    def body(j, carry):
        for pi in range(n_pp):
            for c in range(ncol):
                x_sc[c, pl.ds(pi * bpp, bpp), :] = page_refs[pi * ncol + c][0, pl.ds(j, bpp, stride=CMP_BLOCK), :]
        out = []
        for s in range(2):
            lhs = x_sc[s * NSA_KV_HEADS:(s + 1) * NSA_KV_HEADS].reshape(NSA_KV_HEADS * rows, d).astype(BF16)
            out.append(carry[s] + _mm(lhs, w_ref[s, j]))
        return tuple(out)

    zero = jnp.zeros((NSA_KV_HEADS * rows, d), F32)
    acc = lax.fori_loop(0, CMP_BLOCK, body, (zero, zero))
    for s in range(2):
        for h in range(NSA_KV_HEADS):
            o_ref[0, s, h] = acc[s][h * rows:(h + 1) * rows]


def _compress_pages_call(page_table, w_cmp_bf, cache_half, n_pp):
    bsz, n_pages = page_table.shape
    assert n_pages % n_pp == 0
    bpp = PAGE_SIZE // CMP_BLOCK
    ncol = 2 * NSA_KV_HEADS
    page_spec = lambda kk, c: pl.BlockSpec((1, PAGE_SIZE, NSA_HEAD_DIM),
                                           lambda b, p, pt: (pt[b, p * n_pp + kk], 0, c))
    return pl.pallas_call(
        functools.partial(_compress_pages_kernel, n_pp=n_pp),
        grid_spec=pltpu.PrefetchScalarGridSpec(
            num_scalar_prefetch=1,
            grid=(bsz, n_pages // n_pp),
            in_specs=[pl.BlockSpec(w_cmp_bf.shape, lambda b, p, pt: (0, 0, 0, 0))]
                     + [page_spec(kk, c) for kk in range(n_pp) for c in range(ncol)],
            out_specs=pl.BlockSpec((1, 2, NSA_KV_HEADS, n_pp * bpp, NSA_HEAD_DIM),
                                   lambda b, p, pt: (b, 0, 0, p, 0)),
            scratch_shapes=[pltpu.VMEM((ncol, n_pp * bpp, NSA_HEAD_DIM), F32)],
        ),
        out_shape=jax.ShapeDtypeStruct((bsz, 2, NSA_KV_HEADS, n_pages * bpp, NSA_HEAD_DIM), F32),
        compiler_params=_cp(("parallel", "arbitrary")),
        name="nsa_compress_pages",
    )(page_table, w_cmp_bf, *([cache_half] * (n_pp * ncol)))


def _nsa_sample_select_kernel(q_ref, kc_ref, vc_ref, new_ref, w0_ref, ocmp_ref, idx_ref, *, past_len):
    kvh = pl.program_id(1)
    nbp = kc_ref.shape[3]
    nbt = nbp + 1
    qpos = past_len
    scale = NSA_HEAD_DIM ** -0.5
    qg = q_ref[0, 0].astype(BF16)
    kc = kc_ref[0, 0, 0].astype(BF16)
    vc = vc_ref[0, 0, 0].astype(BF16)
    new = new_ref[0]
    pick = lambda r: jnp.where(kvh == 0, new[r:r + 1], new[r + 1:r + 2])
    kc_new = _mm(jnp.broadcast_to(pick(0), (8, LANES)).astype(BF16), w0_ref[0].astype(BF16))
    vc_new = _mm(jnp.broadcast_to(pick(2), (8, LANES)).astype(BF16), w0_ref[1].astype(BF16))

    s_past = _nt(qg, kc) * scale
    s_new = _nt(qg, kc_new.astype(BF16))[:, 0:1] * scale
    n_past = lax.broadcasted_iota(jnp.int32, (8, nbp), 1)
    mask_past = ((n_past + 1) * CMP_BLOCK - 1) <= qpos
    mask_new = ((nbt * CMP_BLOCK - 1) <= qpos)
    s_past = jnp.where(mask_past, s_past, NEG)
    s_new = s_new if mask_new else jnp.full_like(s_new, NEG)
    m = jnp.maximum(jnp.max(s_past, axis=-1, keepdims=True), s_new)
    p_past = jnp.where(mask_past, jnp.exp(s_past - m), 0.0)
    p_new = jnp.exp(s_new - m) * (1.0 if mask_new else 0.0)
    den = jnp.maximum(jnp.sum(p_past, axis=-1, keepdims=True) + p_new, TINY)
    p_past = p_past / den
    p_new = p_new / den
    o_cmp = _mm(p_past.astype(BF16), vc) + p_new.astype(BF16).astype(F32) * vc_new.astype(BF16).astype(F32)[0:1]
    ocmp_ref[0, 0] = o_cmp

    grp = lax.broadcasted_iota(jnp.int32, (8, 1), 0) < NSA_GROUP
    ps_past = jnp.sum(jnp.where(grp, p_past, 0.0), axis=0, keepdims=True)
    ps_new = jnp.sum(jnp.where(grp, p_new, 0.0), axis=0, keepdims=True)

    width = ((nbt + LANES - 1) // LANES) * LANES
    nrow = ((nbt + 7) // 8) * 8
    n = lax.broadcasted_iota(jnp.int32, (1, width), 1)
    p_all = jnp.concatenate([ps_past, jnp.broadcast_to(ps_new, (1, width - nbp))], axis=1)
    cur = qpos >> CMP_SHIFT
    valid = (n * CMP_BLOCK <= qpos) & (n < nbt)
    forced = (n == 0) | (n == cur) | (n == cur - 1)
    score = jnp.where(valid, p_all + jnp.where(forced, FORCE_BONUS, 0.0), NEG)
    mrow = lax.broadcasted_iota(jnp.int32, (nrow, width), 0)
    ncol = lax.broadcasted_iota(jnp.int32, (nrow, width), 1)
    score_b = jnp.broadcast_to(score, (nrow, width))
    score_col = jnp.sum(jnp.where(mrow == ncol, score_b, 0.0), axis=1, keepdims=True)
    col_ok = lax.broadcasted_iota(jnp.int32, (nrow, 1), 0) < nbt
    beats = col_ok & ((score_col > score_b) | ((score_col == score_b) & (mrow < ncol)))
    rank = jnp.sum(jnp.where(beats, 1.0, 0.0), axis=0, keepdims=True)
    n_sel = min(N_SEL, nbt)
    r = lax.broadcasted_iota(jnp.int32, (N_SEL, width), 0)
    hit = ((jnp.broadcast_to(rank, (N_SEL, width)) == r.astype(F32)) & jnp.broadcast_to(valid, (N_SEL, width))
           & (r < n_sel))
    nf = jnp.broadcast_to(n, (N_SEL, width)).astype(F32)
    idx = jnp.sum(jnp.where(hit, nf, 0.0), axis=1, keepdims=True)
    any_hit = jnp.sum(jnp.where(hit, 1.0, 0.0), axis=1, keepdims=True)
    idx = jnp.where(any_hit > 0.5, idx, -1.0).astype(jnp.int32)
    idx_ref[0, 0] = jnp.broadcast_to(idx, (N_SEL, LANES))


def _nsa_sample_select_call(q8, kcvc_p, new_rows, w0, past_len):
    bsz = q8.shape[0]
    nbp = kcvc_p.shape[3]
    return pl.pallas_call(
        functools.partial(_nsa_sample_select_kernel, past_len=past_len),
        grid=(bsz, NSA_KV_HEADS),
        in_specs=[pl.BlockSpec((1, 1, 8, LANES), lambda b, h: (b, h, 0, 0)),
                  pl.BlockSpec((1, 1, 1, nbp, LANES), lambda b, h: (b, 0, h, 0, 0)),
                  pl.BlockSpec((1, 1, 1, nbp, LANES), lambda b, h: (b, 1, h, 0, 0)),
                  pl.BlockSpec((1, 8, LANES), lambda b, h: (b, 0, 0)),
                  pl.BlockSpec((2, LANES, LANES), lambda b, h: (0, 0, 0))],
        out_specs=[pl.BlockSpec((1, 1, 8, LANES), lambda b, h: (b, h, 0, 0)),
                   pl.BlockSpec((1, 1, N_SEL, LANES), lambda b, h: (b, h, 0, 0))],
        out_shape=[jax.ShapeDtypeStruct((bsz, NSA_KV_HEADS, 8, LANES), F32),
                   jax.ShapeDtypeStruct((bsz, NSA_KV_HEADS, N_SEL, LANES), jnp.int32)],
        compiler_params=_cp(("parallel", "parallel")),
        name="nsa_select_sample",
    )(q8, kcvc_p, kcvc_p, new_rows, w0)


def _nsa_sample_attend_kernel(phys_ref, flag_ref, q_ref, new_ref, wnew_ref, gate_ref, ocmp_ref, kw_ref, vw_ref,
                              *rest, past_len):
    ks_refs = rest[:N_SEL]
    vs_refs = rest[N_SEL:2 * N_SEL]
    o_ref = rest[2 * N_SEL]
    b = pl.program_id(0)
    kvh = pl.program_id(1)
    base = (b * NSA_KV_HEADS + kvh) * (N_SEL + 1)
    scale = NSA_HEAD_DIM ** -0.5
    qg = q_ref[0, 0].astype(BF16)
    new = new_ref[0]
    wnew = wnew_ref[0]
    pick = lambda arr, r: jnp.where(kvh == 0, arr[r:r + 1], arr[r + 1:r + 2])
    rnd = lambda a: a.astype(BF16).astype(F32)
    qf = qg.astype(F32)

    def attend(k, v, mask, k_new, v_new, new_on):
        s = jnp.where(mask, _nt(qg, k) * scale, NEG)
        s_n = jnp.sum(qf * rnd(k_new), axis=-1, keepdims=True) * scale
        s_n = jnp.where(new_on, s_n, NEG)
        m = jnp.maximum(jnp.max(s, axis=-1, keepdims=True), s_n)
        p = jnp.where(mask, jnp.exp(s - m), 0.0)
        p_n = jnp.where(new_on, jnp.exp(s_n - m), 0.0)
        den = jnp.maximum(jnp.sum(p, axis=-1, keepdims=True) + p_n, TINY)
        return (_mm(p.astype(BF16), v) + rnd(p_n) * rnd(v_new)) / den

    k_sel = jnp.concatenate([r[0].astype(BF16) for r in ks_refs], axis=0)
    v_sel = jnp.concatenate([r[0].astype(BF16) for r in vs_refs], axis=0)
    chunk = lax.shift_right_logical(lax.broadcasted_iota(jnp.int32, (1, N_SEL * CMP_BLOCK), 1), CMP_SHIFT)
    live = jnp.zeros((1, N_SEL * CMP_BLOCK), jnp.int32)
    for r in range(N_SEL):
        live = jnp.where(chunk == r, flag_ref[base + r], live)
    smask = jnp.broadcast_to(live, (8, N_SEL * CMP_BLOCK)) > 0
    o_slc = attend(k_sel, v_sel, smask, pick(new, 4), pick(new, 6), flag_ref[base + N_SEL] > 0)

    wb = kw_ref.shape[1]
    kpos = past_len - wb + lax.broadcasted_iota(jnp.int32, (8, wb), 1)
    dist = past_len - kpos
    wmask = (dist >= 0) & (dist <= WINDOW) & (kpos >= 0)
    o_win = attend(kw_ref[0].astype(BF16), vw_ref[0].astype(BF16), wmask, pick(wnew, 0), pick(wnew, 2), True)

    gate = gate_ref[0, 0]
    o_ref[0, 0] = (gate[:, 0:LANES] * ocmp_ref[0, 0] + gate[:, LANES:2 * LANES] * o_slc
                   + gate[:, 2 * LANES:3 * LANES] * o_win)


def _nsa_sample_attend_call(phys, flags, q8, new_rows, win_new, gates, o_cmp, win_state, cache_blocks, past_len):
    bsz = q8.shape[0]
    wb = win_state.shape[1]
    kvh = NSA_KV_HEADS
    blk_spec = lambda r, col: pl.BlockSpec(
        (1, CMP_BLOCK, LANES), lambda b, h, ph, fl: (ph[(b * kvh + h) * (N_SEL + 1) + r], 0, col + h))
    return pl.pallas_call(
        functools.partial(_nsa_sample_attend_kernel, past_len=past_len),
        grid_spec=pltpu.PrefetchScalarGridSpec(
            num_scalar_prefetch=2,
            grid=(bsz, kvh),
            in_specs=[pl.BlockSpec((1, 1, 8, LANES), lambda b, h, ph, fl: (b, h, 0, 0)),
                      pl.BlockSpec((1, 8, LANES), lambda b, h, ph, fl: (b, 0, 0)),
                      pl.BlockSpec((1, 4, LANES), lambda b, h, ph, fl: (b, 0, 0)),
                      pl.BlockSpec((1, 1, 8, 3 * LANES), lambda b, h, ph, fl: (b, h, 0, 0)),
                      pl.BlockSpec((1, 1, 8, LANES), lambda b, h, ph, fl: (b, h, 0, 0)),
                      pl.BlockSpec((1, wb, LANES), lambda b, h, ph, fl: (b, 0, h)),
                      pl.BlockSpec((1, wb, LANES), lambda b, h, ph, fl: (b, 0, kvh + h))]
                     + [blk_spec(r, 2 * kvh) for r in range(N_SEL)]
                     + [blk_spec(r, 3 * kvh) for r in range(N_SEL)],
            out_specs=pl.BlockSpec((1, 1, 8, LANES), lambda b, h, ph, fl: (b, h, 0, 0)),
        ),
        out_shape=jax.ShapeDtypeStruct((bsz, kvh, 8, LANES), F32),
        compiler_params=_cp(("parallel", "parallel")),
        name="nsa_attend_sample",
    )(phys, flags, q8, new_rows, win_new, gates, o_cmp, win_state, win_state, *([cache_blocks] * (2 * N_SEL)))


def _pad_cols(a, n):
    return jnp.pad(a, [(0, 0)] * (a.ndim - 1) + [(0, n - a.shape[-1])])


def kernel(x_prompt, x_sample, cache_diff_kv, cache_nsa_kv, state_nsa_win, state_ffn_conv, page_table,
           norm_mix_pre, norm_mix_post, w_in, diff_lambda, diff_subln, nsa_w_cmp, w_branch_diff, w_branch_nsa,
           w_out, norm_ffn_pre, norm_ffn_post, ffn_w_up, ffn_conv_w, ffn_conv_b, ffn_w_down):
    b, s, d = x_prompt.shape
    bs, ts, _ = x_sample.shape
    depth = w_in.shape[0]
    assert depth == 1 and ts == 1
    n_pages = page_table.shape[1]
    past_len = n_pages * PAGE_SIZE
    f = ffn_w_down.shape[1]
    t = b * s
    kvh, hd = NSA_KV_HEADS, NSA_HEAD_DIM
    layer = 0
    lam_init = 0.8 - 0.6 * math.exp(-0.3 * layer)

    wq = DIFF_HEADS * 2 * DIFF_QK_DIM
    wv = DIFF_HEADS * DIFF_V_DIM
    wnq = NSA_HEADS * hd
    wkv = 2 * kvh * hd
    o_dq, o_dk, o_dv = 0, wq, 2 * wq
    o_nq = o_dv + wv
    o_cs = o_nq + wnq
    o_win = o_cs + 2 * wkv
    o_ng = o_win + wkv
    o_mg = o_ng + 3 * NSA_HEADS
    w = w_in[layer]
    w_dq = w[:, o_dq:o_dk].astype(BF16)
    w_dkv = w[:, o_dk:o_nq].astype(BF16)
    w_nq = w[:, o_nq:o_cs].astype(BF16)
    w_cs = w[:, o_cs:o_win].astype(BF16)
    w_win = w[:, o_win:o_ng].astype(BF16)
    w_ng = _pad_cols(w[:, o_ng:o_mg], LANES).astype(BF16)
    w_mg = w[:, o_mg:].astype(BF16)
    w_bd = w_branch_diff[layer].astype(BF16)
    w_bn = w_branch_nsa[layer].astype(BF16)
    w_o = w_out[layer].astype(BF16)
    fp = ((f + 511) // 512) * 512
    w_up = ffn_w_up[layer]
    wa = _pad_cols(w_up[:, :f], fp).astype(BF16)
    wg = _pad_cols(w_up[:, f:], fp).astype(BF16)
    cw = ffn_conv_w[layer]
    cwa, cwg = _pad_cols(cw[:, :f], fp), _pad_cols(cw[:, f:], fp)
    cb = ffn_conv_b[layer].reshape(1, 2 * f)
    ba, bg = _pad_cols(cb[:, :f], fp), _pad_cols(cb[:, f:], fp)
    wd = jnp.pad(ffn_w_down[layer], ((0, fp - f), (0, 0))).astype(BF16)
    w_cmp = nsa_w_cmp[layer]

    lam = _lam_call(diff_lambda[layer], lam_init)

    def projections(h, pos, pos_rows, attn_dtype):
        t64 = _rope_tables(pos, DIFF_QK_DIM)
        t128 = _rope_tables(pos, hd)
        r64, r128, nn, sg = MODE_ROPE64, MODE_ROPE128, MODE_NONE, MODE_SIGMOID
        dq = _proj_call(h, w_dq, [r64] * (wq // 512), 512, attn_dtype, t64, pos_rows, "proj_diff_q")
        dkv = _proj_call(h, w_dkv, [r64] * (wq // 512) + [nn] * (wv // 512), 512, F32, t64, pos_rows, "proj_diff_kv")
        nq = _proj_call(h, w_nq, [r128] * (wnq // 512), 512, attn_dtype, t128, pos_rows, "proj_nsa_q")
        cs = _proj_call(h, w_cs, [r128, nn, r128, nn], wkv // 2, F32, t128, pos_rows, "proj_nsa_kv")
        win = _proj_call(h, w_win, [r128, nn], wkv // 2, F32, t128, pos_rows, "proj_nsa_win")
        ng = _proj_call(h, w_ng, [sg], LANES, F32, (), None, "proj_nsa_gate")
        return dq, dkv, nq, cs, win, ng

    xp = x_prompt.reshape(t, d)
    hp = _norm_call(xp, norm_mix_pre[layer])
    pos_p = jnp.arange(s, dtype=jnp.int32)
    dq, dkv, nq, cs, win, ng = projections(hp, pos_p, s, BF16)
    diff = _diff_prompt_call(lam, dq, dkv, diff_subln[layer], b, s, lam_init)
    nb = s // CMP_BLOCK
    blocks = cs.reshape(b, nb, CMP_BLOCK, 4, kvh, hd)[:, :, :, :2]
    blocks = blocks.transpose(3, 4, 0, 1, 2, 5).reshape(2, kvh, b * nb, CMP_BLOCK * hd)
    kcvc = _compress_call(blocks, w_cmp.reshape(2, CMP_BLOCK * hd, hd))
    nsa = _nsa_prompt_call(nq, kcvc, cs, win, ng, b, s)
    mix = _merge_call(hp, diff, nsa, w_mg, w_bd, w_bn)
    xp1 = _out_proj_call(mix, w_o, xp, norm_mix_post[layer])
    xp2 = _ffn_prompt_call(xp1, norm_ffn_pre[layer], norm_ffn_post[layer], wa, wg, cwa, cwg, ba, bg, wd, s)

    wbp = min(WINDOW, s)
    new_diff_kv_prompt = dkv.reshape(1, b, s, 2, DIFF_HEADS, 2 * DIFF_QK_DIM)
    new_nsa_kv_prompt = cs.reshape(1, b, s, 4, kvh, hd)
    new_win_prompt = win.reshape(b, s, 2, kvh, hd)[None, :, s - wbp:]

    xs = x_sample.reshape(bs, d)
    hs = _norm_call(xs, norm_mix_pre[layer])
    pos_s = jnp.full((bs,), past_len, jnp.int32)
    dq_s, dkv_s, nq_s, cs_s, win_s, ng_s = projections(hs, pos_s, bs, F32)

    cache_d = cache_diff_kv[layer].reshape(cache_diff_kv.shape[1], PAGE_SIZE, 2 * DIFF_HEADS * LANES)
    diff_s = _diff_decode_call(page_table, lam, dq_s.reshape(bs, DIFF_HEADS, LANES),
                               dkv_s.reshape(bs, 2 * DIFF_HEADS, LANES), diff_subln[layer], cache_d, lam_init,
                               _pick(n_pages, (4, 2, 1)))
    diff_s = diff_s.reshape(bs, DIFF_HEADS * LANES).astype(BF16)

    n_pool = cache_nsa_kv.shape[1]
    cache_n = cache_nsa_kv[layer].reshape(n_pool, PAGE_SIZE, 4 * kvh * hd)
    kcvc_p = _compress_pages_call(page_table, w_cmp.astype(BF16), cache_n, _pick(n_pages, (16, 8, 4, 2, 1)))
    q8 = jnp.pad(nq_s.reshape(bs, kvh, NSA_GROUP, hd), ((0, 0), (0, 0), (0, 8 - NSA_GROUP), (0, 0)))
    new_rows = cs_s.reshape(bs, 4 * kvh, hd)
    o_cmp_s, idx_s = _nsa_sample_select_call(q8, kcvc_p, new_rows, w_cmp[:, 0], past_len)
    idx = idx_s[..., 0]
    nbp = past_len // CMP_BLOCK
    bpp = PAGE_SIZE // CMP_BLOCK
    is_past = (idx >= 0) & (idx < nbp)
    safe = jnp.clip(idx, 0, nbp - 1)
    page = jnp.take_along_axis(page_table, (safe // bpp).reshape(bs, -1), axis=1).reshape(idx.shape)
    phys = page * bpp + safe % bpp
    new_sel = jnp.any(idx == nbp, axis=-1, keepdims=True)
    phys = jnp.concatenate([phys, jnp.zeros_like(phys[..., :1])], axis=-1).reshape(-1).astype(jnp.int32)
    flags = jnp.concatenate([is_past, new_sel], axis=-1).reshape(-1).astype(jnp.int32)
    gates = ng_s[:, :3 * NSA_HEADS].reshape(bs, 3, kvh, NSA_GROUP).transpose(0, 2, 3, 1)
    gates = jnp.pad(gates, ((0, 0), (0, 0), (0, 8 - NSA_GROUP), (0, 0)))
    gates = jnp.broadcast_to(gates[..., None], (bs, kvh, 8, 3, LANES)).reshape(bs, kvh, 8, 3 * LANES)
    wbs = state_nsa_win.shape[2]
    win_state = state_nsa_win[layer].reshape(bs, wbs, 2 * kvh * hd)
    cache_blocks = cache_nsa_kv[layer].reshape(n_pool * bpp, CMP_BLOCK, 4 * kvh * hd)
    nsa_s = _nsa_sample_attend_call(phys, flags, q8, new_rows, win_s.reshape(bs, 2 * kvh, hd), gates, o_cmp_s,
                                    win_state, cache_blocks, past_len)
    nsa_s = nsa_s[:, :, :NSA_GROUP].reshape(bs, NSA_HEADS * hd).astype(BF16)

    mix_s = _merge_call(hs, diff_s, nsa_s, w_mg, w_bd, w_bn)
    xs1 = _out_proj_call(mix_s, w_o, xs, norm_mix_post[layer])

    tail = xp1.reshape(b, s, d)[:, s - (CONV_W - 1):].reshape(b * (CONV_W - 1), d)
    rows = jnp.concatenate([xs1, tail], axis=0)
    pad_r = (-rows.shape[0]) % 16
    rows = jnp.pad(rows, ((0, pad_r), (0, 0)))
    h_rows = _norm_call(rows, norm_ffn_pre[layer])
    w_up_p = jnp.concatenate([wa, wg], axis=1)
    u_rows = _proj_call(h_rows, w_up_p, [MODE_NONE] * (2 * fp // 512), 512, F32, (), None, "proj_ffn_up_rows")
    u_unpad = jnp.concatenate([u_rows[:, :f], u_rows[:, fp:fp + f]], axis=1)
    buf = state_ffn_conv[layer]
    bufa = _pad_cols(buf[..., :f], fp).transpose(1, 0, 2)
    bufg = _pad_cols(buf[..., f:], fp).transpose(1, 0, 2)
    xs2 = _ffn_sample_call(u_rows, bufa, bufg, cwa, cwg, ba, bg, wd, xs1, norm_ffn_post[layer])

    new_conv_prompt = u_unpad[bs:bs + b * (CONV_W - 1)].reshape(1, b, CONV_W - 1, 2 * f)
    new_conv_sample = jnp.concatenate([buf[:, 1:], u_unpad[:bs, None]], axis=1)[None]
    new_diff_kv_sample = dkv_s.reshape(1, bs, 1, 2, DIFF_HEADS, 2 * DIFF_QK_DIM)
    new_nsa_kv_sample = cs_s.reshape(1, bs, 1, 4, kvh, hd)
    new_win_sample = jnp.concatenate([state_nsa_win[layer][:, 1:], win_s.reshape(bs, 1, 2, kvh, hd)], axis=1)[None]

    return (xp2.reshape(b, s, d), xs2.reshape(bs, 1, d), new_diff_kv_prompt, new_diff_kv_sample,
            new_nsa_kv_prompt, new_nsa_kv_sample, new_win_prompt, new_win_sample, new_conv_prompt, new_conv_sample)
```

```python
import functools
import math

import jax
import jax.numpy as jnp
from jax import lax
from jax.experimental import pallas as pl
from jax.experimental.pallas import tpu as pltpu

F32 = jnp.float32
BF16 = jnp.bfloat16

DIFF_HEADS = 8
DIFF_QK_DIM = 64
DIFF_V_DIM = 128
NSA_HEADS = 8
NSA_KV_HEADS = 2
NSA_GROUP = 4
NSA_HEAD_DIM = 128
CMP_BLOCK = 64
CMP_SHIFT = 6
N_SEL = 16
WINDOW = 512
PAGE_SIZE = 128
CONV_W = 3
FORCE_BONUS = 1e4
ROPE_THETA = 10000.0
EPS = 1e-6
NEG = -1e30
TINY = 1e-30
LANES = 128
VMEM_LIMIT = 52 * 1024 * 1024

_NT = (((1,), (1,)), ((), ()))


def _nt(a, b):
    return lax.dot_general(a, b, _NT, preferred_element_type=F32)


def _mm(a, b):
    return jnp.dot(a, b, preferred_element_type=F32)


def _cp(sem):
    return pltpu.CompilerParams(dimension_semantics=sem, vmem_limit_bytes=VMEM_LIMIT)


def _pick(n, cands):
    for c in cands:
        if n % c == 0:
            return c
    return n


def _rms(x, g):
    return x * lax.rsqrt(jnp.mean(x * x, axis=-1, keepdims=True) + EPS) * g


def _lam_kernel(l_ref, o_ref, *, lam_init):
    l = l_ref[...]
    a = jnp.sum(l[0:1] * l[1:2], axis=-1, keepdims=True)
    b = jnp.sum(l[2:3] * l[3:4], axis=-1, keepdims=True)
    o_ref[...] = jnp.broadcast_to(jnp.exp(a) - jnp.exp(b) + lam_init, o_ref.shape)


def _lam_call(lam_params, lam_init):
    out = pl.pallas_call(
        functools.partial(_lam_kernel, lam_init=lam_init),
        out_shape=jax.ShapeDtypeStruct((8, LANES), F32),
        name="diff_lambda",
    )(lam_params)
    return out[0, :1]


def _norm_kernel(x_ref, g_ref, o_ref):
    o_ref[...] = _rms(x_ref[...], g_ref[...]).astype(o_ref.dtype)


def _norm_call(x, g):
    m, d = x.shape
    tm = _pick(m, (512, 256, 128, 64, 32, 16))
    return pl.pallas_call(
        _norm_kernel,
        grid=(m // tm,),
        in_specs=[pl.BlockSpec((tm, d), lambda i: (i, 0)), pl.BlockSpec((1, d), lambda i: (0, 0))],
        out_specs=pl.BlockSpec((tm, d), lambda i: (i, 0)),
        out_shape=jax.ShapeDtypeStruct((m, d), BF16),
        compiler_params=_cp(("parallel",)),
        name="rmsnorm",
    )(x, g.reshape(1, d))


MODE_NONE, MODE_ROPE64, MODE_ROPE128, MODE_SIGMOID = 0, 1, 2, 3


def _rope_tables(pos, d):
    half = d // 2
    inv = 1.0 / (ROPE_THETA ** (jnp.arange(0, d, 2, dtype=F32) / d))
    ang = pos.astype(F32)[:, None] * inv[None, :]
    lane = jnp.arange(LANES)
    cos = jnp.cos(ang)[:, lane % half]
    sin = jnp.sin(ang)[:, lane % half]
    first = ((lane % d) < half)[None, :]
    return cos, jnp.where(first, -sin, 0.0), jnp.where(first, 0.0, sin)


def _apply_rope(z, cos, sa, sb, half):
    outs = []
    for c in range(z.shape[1] // LANES):
        blk = z[:, c * LANES:(c + 1) * LANES]
        outs.append(blk * cos + pltpu.roll(blk, LANES - half, 1) * sa + pltpu.roll(blk, half, 1) * sb)
    return outs[0] if len(outs) == 1 else jnp.concatenate(outs, axis=1)


def _proj_kernel(*refs, runs, n_tab):
    h_ref, w_ref = refs[0], refs[1]
    tabs = refs[2:2 + n_tab]
    o_ref = refs[2 + n_tab]
    j = pl.program_id(1)
    acc = _mm(h_ref[...], w_ref[...])

    def emit(mode):
        if mode == MODE_NONE:
            o_ref[...] = acc.astype(o_ref.dtype)
        elif mode == MODE_SIGMOID:
            o_ref[...] = jax.nn.sigmoid(acc).astype(o_ref.dtype)
        else:
            half = 32 if mode == MODE_ROPE64 else 64
            o_ref[...] = _apply_rope(acc, tabs[0][...], tabs[1][...], tabs[2][...], half).astype(o_ref.dtype)

    if len(runs) == 1:
        emit(runs[0][0])
    else:
        for mode, j0, j1 in runs:
            pl.when((j >= j0) & (j < j1))(functools.partial(emit, mode))


def _proj_call(h, w, modes, tn, out_dtype, tables=(), pos_rows=None, name="proj"):
    m, k = h.shape
    n = w.shape[1]
    assert n % tn == 0 and len(modes) == n // tn
    tm = _pick(m, (1024, 512, 256, 128, 64, 48, 32, 16))
    runs = []
    for jj, md in enumerate(modes):
        if runs and runs[-1][0] == md:
            runs[-1][2] = jj + 1
        else:
            runs.append([md, jj, jj + 1])
    runs = tuple(tuple(r) for r in runs)
    in_specs = [pl.BlockSpec((tm, k), lambda i, j: (i, 0)), pl.BlockSpec((k, tn), lambda i, j: (0, j))]
    if tables:
        nblk = pos_rows // tm
        assert nblk * tm == pos_rows
        in_specs += [pl.BlockSpec((tm, LANES), lambda i, j: (i % nblk, 0)) for _ in tables]
    return pl.pallas_call(
        functools.partial(_proj_kernel, runs=runs, n_tab=len(tables)),
        grid=(m // tm, n // tn),
        in_specs=in_specs,
        out_specs=pl.BlockSpec((tm, tn), lambda i, j: (i, j)),
        out_shape=jax.ShapeDtypeStruct((m, n), out_dtype),
        compiler_params=_cp(("parallel", "arbitrary")),
        name=name,
    )(h, w, *tables)


def _flash_step(carry, s, v_ext):
    m, acc = carry
    m_new = jnp.maximum(m, jnp.max(s, axis=-1, keepdims=True))
    p = jnp.exp(s - m_new)
    return m_new, acc * jnp.exp(m - m_new) + _mm(p.astype(BF16), v_ext)


def _flash_init(rows, d):
    return jnp.full((rows, 1), NEG, F32), jnp.zeros((rows, 2 * d), F32)


def _flash_out(carry, d):
    acc = carry[1]
    return acc[:, :d] / jnp.maximum(acc[:, d:d + 1], TINY)


def _ext_ones(v):
    return jnp.concatenate([v, jnp.ones(v.shape, v.dtype)], axis=1)


def _diff_prompt_kernel(lam_ref, q_ref, k_ref, v_ref, g_ref, o_ref, kb_ref, vb_ref, *, tq, out_scale):
    qi = pl.program_id(2)
    dv = DIFF_V_DIM

    @pl.when(qi == 0)
    def _():
        kb_ref[...] = k_ref[...].astype(BF16)
        vb_ref[...] = _ext_ones(v_ref[...].astype(BF16))

    q = q_ref[...]
    lane = lax.broadcasted_iota(jnp.int32, q.shape, 1)
    scale = jnp.asarray(DIFF_QK_DIM ** -0.5, q.dtype)
    q1 = jnp.where(lane < DIFF_QK_DIM, q, 0) * scale
    q2 = jnp.where(lane >= DIFF_QK_DIM, q, 0) * scale

    def block(kb, carry, bias):
        start = pl.multiple_of(kb * tq, tq)
        k = kb_ref[pl.ds(start, tq), :]
        v = vb_ref[pl.ds(start, tq), :]
        s1, s2 = _nt(q1, k), _nt(q2, k)
        if bias is not None:
            s1, s2 = s1 + bias, s2 + bias
        return _flash_step(carry[0], s1, v), _flash_step(carry[1], s2, v)

    init = _flash_init(tq, dv)
    carry = lax.fori_loop(0, qi, lambda kb, c: block(kb, c, None), (init, init))
    causal = (lax.broadcasted_iota(jnp.int32, (tq, tq), 1) <= lax.broadcasted_iota(jnp.int32, (tq, tq), 0))
    c1, c2 = block(qi, carry, jnp.where(causal, 0.0, NEG))
    o = _flash_out(c1, dv) - lam_ref[0] * _flash_out(c2, dv)
    o_ref[...] = (_rms(o, g_ref[...]) * out_scale).astype(o_ref.dtype)


def _diff_prompt_call(lam, q, kv, subln, b, s, lam_init):
    t = b * s
    tq = _pick(s, (256, 128))
    nq = s // tq
    return pl.pallas_call(
        functools.partial(_diff_prompt_kernel, tq=tq, out_scale=1.0 - lam_init),
        grid_spec=pltpu.PrefetchScalarGridSpec(
            num_scalar_prefetch=1,
            grid=(b, DIFF_HEADS, nq),
            in_specs=[
                pl.BlockSpec((tq, LANES), lambda bi, h, qi, lam: (bi * nq + qi, h)),
                pl.BlockSpec((s, LANES), lambda bi, h, qi, lam: (bi, h)),
                pl.BlockSpec((s, LANES), lambda bi, h, qi, lam: (bi, DIFF_HEADS + h)),
                pl.BlockSpec((1, LANES), lambda bi, h, qi, lam: (0, 0)),
            ],
            out_specs=pl.BlockSpec((tq, LANES), lambda bi, h, qi, lam: (bi * nq + qi, h)),
            scratch_shapes=[pltpu.VMEM((s, LANES), BF16), pltpu.VMEM((s, 2 * LANES), BF16)],
        ),
        out_shape=jax.ShapeDtypeStruct((t, DIFF_HEADS * DIFF_V_DIM), BF16),
        compiler_params=_cp(("parallel", "parallel", "arbitrary")),
        name="diff_attn_prompt",
    )(lam, q, kv, kv, subln.reshape(1, LANES))


def _compress_kernel(a_ref, w_ref, o_ref):
    @pl.when(pl.program_id(2) == 0)
    def _():
        o_ref[...] = jnp.zeros(o_ref.shape, o_ref.dtype)

    o_ref[0, 0] += _mm(a_ref[0, 0].astype(BF16), w_ref[0].astype(BF16))


def _compress_call(blocks, w_cmp):
    _, kvh, r, kk = blocks.shape
    tk = _pick(kk, (2048, 1024, 512))
    return pl.pallas_call(
        _compress_kernel,
        grid=(2, kvh, kk // tk),
        in_specs=[pl.BlockSpec((1, 1, r, tk), lambda s, h, k: (s, h, 0, k)),
                  pl.BlockSpec((1, tk, NSA_HEAD_DIM), lambda s, h, k: (s, k, 0))],
        out_specs=pl.BlockSpec((1, 1, r, NSA_HEAD_DIM), lambda s, h, k: (s, h, 0, 0)),
        out_shape=jax.ShapeDtypeStruct((2, kvh, r, NSA_HEAD_DIM), F32),
        compiler_params=_cp(("parallel", "parallel", "arbitrary")),
        name="nsa_compress_prompt",
    )(blocks, w_cmp)


def _topk_mask(score, blk, n_sel):
    nb = score.shape[0]
    rank = jnp.zeros(score.shape, jnp.int32)
    for mm in range(nb):
        rm = score[mm:mm + 1, :]
        beats = (rm > score) | ((rm == score) & (blk > mm))
        rank = rank + beats.astype(jnp.int32)
    return rank < n_sel


def _nsa_prompt_kernel(q_ref, kc_ref, vc_ref, ks_ref, vs_ref, kw_ref, vw_ref, ng_ref, o_ref,
                       ksb, vsb, kwb, vwb, bias_ref, *, s_len, tq, tk):
    kvh = pl.program_id(1)
    qi = pl.program_id(2)
    nb = s_len // CMP_BLOCK
    g = NSA_GROUP
    rows = g * tq
    d = NSA_HEAD_DIM
    scale = d ** -0.5

    @pl.when(qi == 0)
    def _():
        ksb[...] = ks_ref[...].astype(BF16)
        vsb[...] = _ext_ones(vs_ref[...].astype(BF16))
        kwb[...] = kw_ref[...].astype(BF16)
        vwb[...] = _ext_ones(vw_ref[...].astype(BF16))

    q = q_ref[...]
    qs = [q[:, i * LANES:(i + 1) * LANES] for i in range(g)]
    qr = jnp.concatenate(qs, axis=0)
    q0 = qi * tq
    tpos = q0 + lax.broadcasted_iota(jnp.int32, (tq, 1), 0)
    qpos = q0 + (lax.broadcasted_iota(jnp.int32, (rows, 1), 0) & (tq - 1))

    kc = kc_ref[0, 0, 0].astype(BF16)
    vc = vc_ref[0, 0, 0].astype(BF16)

    def cmp_probs(s, end_le_qpos, axis):
        sm = jnp.where(end_le_qpos, s, NEG)
        p = jnp.where(end_le_qpos, jnp.exp(sm - jnp.max(sm, axis=axis, keepdims=True)), 0.0)
        return p / jnp.maximum(jnp.sum(p, axis=axis, keepdims=True), TINY)

    nblk = lax.broadcasted_iota(jnp.int32, (rows, nb), 1)
    pc = cmp_probs(_nt(qr, kc) * scale, ((nblk + 1) * CMP_BLOCK - 1) <= qpos, 1)
    o_cmp = _mm(pc.astype(BF16), vc)

    nblk_t = lax.broadcasted_iota(jnp.int32, (nb, rows), 0)
    qpos_t = q0 + (lax.broadcasted_iota(jnp.int32, (nb, rows), 1) & (tq - 1))
    pc_t = cmp_probs(_nt(kc, qr) * scale, ((nblk_t + 1) * CMP_BLOCK - 1) <= qpos_t, 0)
    p_slc = pc_t[:, 0:tq]
    for i in range(1, g):
        p_slc = p_slc + pc_t[:, i * tq:(i + 1) * tq]
    blk = lax.broadcasted_iota(jnp.int32, (nb, tq), 0)
    tpos_t = q0 + lax.broadcasted_iota(jnp.int32, (nb, tq), 1)
    cur = lax.shift_right_logical(tpos_t, CMP_SHIFT)
    valid = blk * CMP_BLOCK <= tpos_t
    forced = (blk == 0) | (blk == cur) | (blk == cur - 1)
    score = jnp.where(valid, p_slc + jnp.where(forced, FORCE_BONUS, 0.0), NEG)
    sel_t = _topk_mask(score, blk, min(N_SEL, nb)) & valid
    eye = (lax.broadcasted_iota(jnp.int32, (tq, tq), 0) == lax.broadcasted_iota(jnp.int32, (tq, tq), 1))
    sel = _nt(eye.astype(BF16), sel_t.astype(BF16))
    expand = (lax.shift_right_logical(lax.broadcasted_iota(jnp.int32, (nb, s_len), 1), CMP_SHIFT)
              == lax.broadcasted_iota(jnp.int32, (nb, s_len), 0))
    selk = _mm(sel.astype(BF16), expand.astype(BF16))
    kpos_all = lax.broadcasted_iota(jnp.int32, (1, s_len), 1)
    bias_ref[...] = jnp.where((selk > 0.5) & (kpos_all <= tpos), 0.0, NEG)

    def slc_body(kb, carry):
        start = pl.multiple_of(kb * tk, tk)
        k = ksb[pl.ds(start, tk), :]
        v = vsb[pl.ds(start, tk), :]
        bias = bias_ref[:, pl.ds(start, tk)]
        return tuple(_flash_step(carry[i], _nt(qs[i], k) * scale + bias, v) for i in range(g))

    init = tuple(_flash_init(tq, d) for _ in range(g))
    c_s = lax.fori_loop(0, (q0 + tq + tk - 1) // tk, slc_body, init)

    def win_body(kb, carry):
        start = pl.multiple_of(kb * tk, tk)
        k = kwb[pl.ds(start, tk), :]
        v = vwb[pl.ds(start, tk), :]
        dist = tpos - (start + lax.broadcasted_iota(jnp.int32, (1, tk), 1))
        bias = jnp.where((dist >= 0) & (dist <= WINDOW), 0.0, NEG)
        return tuple(_flash_step(carry[i], _nt(qs[i], k) * scale + bias, v) for i in range(g))

    c_w = lax.fori_loop(jnp.maximum(q0 - WINDOW, 0) // tk, (q0 + tq + tk - 1) // tk, win_body, init)

    ng = ng_ref[...]
    for i in range(g):
        def gate(r):
            c0 = r * NSA_HEADS + i
            c1 = c0 + NSA_GROUP
            return jnp.where(kvh == 0, ng[:, c0:c0 + 1], ng[:, c1:c1 + 1])
        o = (gate(0) * o_cmp[i * tq:(i + 1) * tq] + gate(1) * _flash_out(c_s[i], d)
             + gate(2) * _flash_out(c_w[i], d))
        o_ref[:, i * LANES:(i + 1) * LANES] = o.astype(o_ref.dtype)


def _nsa_prompt_call(nq, kcvc, nkv_cs, nkv_win, ng, b, s):
    t = b * s
    tq = 128
    nqb = s // tq
    nb = s // CMP_BLOCK
    gw = NSA_GROUP * LANES
    kcvc5 = kcvc.reshape(2, NSA_KV_HEADS, b, nb, NSA_HEAD_DIM)
    kv_spec = lambda col: pl.BlockSpec((s, LANES), lambda bi, h, qi: (bi, col + h))
    return pl.pallas_call(
        functools.partial(_nsa_prompt_kernel, s_len=s, tq=tq, tk=_pick(s, (256, 128))),
        grid=(b, NSA_KV_HEADS, nqb),
        in_specs=[
            pl.BlockSpec((tq, gw), lambda bi, h, qi: (bi * nqb + qi, h)),
            pl.BlockSpec((1, 1, 1, nb, NSA_HEAD_DIM), lambda bi, h, qi: (0, h, bi, 0, 0)),
            pl.BlockSpec((1, 1, 1, nb, NSA_HEAD_DIM), lambda bi, h, qi: (1, h, bi, 0, 0)),
            kv_spec(4), kv_spec(6),
            kv_spec(0), kv_spec(2),
            pl.BlockSpec((tq, LANES), lambda bi, h, qi: (bi * nqb + qi, 0)),
        ],
        out_specs=pl.BlockSpec((tq, gw), lambda bi, h, qi: (bi * nqb + qi, h)),
        out_shape=jax.ShapeDtypeStruct((t, NSA_HEADS * NSA_HEAD_DIM), BF16),
        scratch_shapes=[pltpu.VMEM((s, LANES), BF16), pltpu.VMEM((s, 2 * LANES), BF16),
                        pltpu.VMEM((s, LANES), BF16), pltpu.VMEM((s, 2 * LANES), BF16),
                        pltpu.VMEM((tq, s), F32)],
        compiler_params=_cp(("parallel", "parallel", "arbitrary")),
        name="nsa_attn_prompt",
    )(nq, kcvc5, kcvc5, nkv_cs, nkv_cs, nkv_win, nkv_win, ng)


def _merge_kernel(h_ref, d_ref, n_ref, wg0_ref, wg1_ref, wbd_ref, wbn_ref, o_ref):
    h = h_ref[...]
    a = _mm(d_ref[...], wbd_ref[...])
    bb = _mm(n_ref[...], wbn_ref[...])
    g0 = jax.nn.sigmoid(_mm(h, wg0_ref[...]))
    g1 = jax.nn.sigmoid(_mm(h, wg1_ref[...]))
    o_ref[...] = (g0 * a + g1 * bb).astype(o_ref.dtype)


def _merge_call(h, diff, nsa, w_mg, w_bd, w_bn):
    m, d = h.shape
    tm = _pick(m, (1024, 512, 256, 128, 64, 32))
    tn = _pick(d, (256, 128))
    nj = d // tn
    row = lambda kk: pl.BlockSpec((tm, kk), lambda i, j: (i, 0))
    return pl.pallas_call(
        _merge_kernel,
        grid=(m // tm, nj),
        in_specs=[row(d), row(diff.shape[1]), row(nsa.shape[1]),
                  pl.BlockSpec((d, tn), lambda i, j: (0, j)),
                  pl.BlockSpec((d, tn), lambda i, j: (0, nj + j)),
                  pl.BlockSpec((w_bd.shape[0], tn), lambda i, j: (0, j)),
                  pl.BlockSpec((w_bn.shape[0], tn), lambda i, j: (0, j))],
        out_specs=pl.BlockSpec((tm, tn), lambda i, j: (i, j)),
        out_shape=jax.ShapeDtypeStruct((m, d), BF16),
        compiler_params=_cp(("parallel", "arbitrary")),
        name="branch_merge",
    )(h, diff, nsa, w_mg, w_mg, w_bd, w_bn)


def _out_proj_kernel(m_ref, w_ref, x_ref, g_ref, o_ref):
    y = _mm(m_ref[...], w_ref[...])
    o_ref[...] = x_ref[...] + _rms(y, g_ref[...])


def _out_proj_call(mix, w_o, x, g):
    m, d = x.shape
    tm = _pick(m, (512, 256, 128, 64, 32))
    return pl.pallas_call(
        _out_proj_kernel,
        grid=(m // tm,),
        in_specs=[pl.BlockSpec((tm, d), lambda i: (i, 0)), pl.BlockSpec((d, d), lambda i: (0, 0)),
                  pl.BlockSpec((tm, d), lambda i: (i, 0)), pl.BlockSpec((1, d), lambda i: (0, 0))],
        out_specs=pl.BlockSpec((tm, d), lambda i: (i, 0)),
        out_shape=jax.ShapeDtypeStruct((m, d), F32),
        compiler_params=_cp(("parallel",)),
        name="out_proj_norm_residual",
    )(mix, w_o, x, g.reshape(1, d))


HALO = 16


def _gelu_glu(ca, cg):
    return jax.nn.gelu(ca, approximate=True) * cg


def _ffn_prompt_kernel(x_ref, xh_ref, gpre_ref, gpost_ref, wa_ref, wg_ref, cwa_ref, cwg_ref, ba_ref, bg_ref,
                       wd_ref, o_ref, h_sc, ua_sc, ug_sc, acc_sc, *, tm, blocks_per_seq):
    i = pl.program_id(0)
    j = pl.program_id(1)

    @pl.when(j == 0)
    def _():
        h_sc[0:HALO, :] = _rms(xh_ref[...], gpre_ref[...]).astype(BF16)
        h_sc[HALO:, :] = _rms(x_ref[...], gpre_ref[...]).astype(BF16)
        acc_sc[...] = jnp.zeros(acc_sc.shape, F32)

    first = (i % blocks_per_seq) == 0
    h = h_sc[...]
    keep = jnp.logical_not(first & (lax.broadcasted_iota(jnp.int32, (tm + HALO, 1), 0) < HALO))
    ua_sc[...] = jnp.where(keep, _mm(h, wa_ref[...]), 0.0)
    ug_sc[...] = jnp.where(keep, _mm(h, wg_ref[...]), 0.0)

    def conv(u_sc, cw_ref, b_ref):
        cw = cw_ref[...]
        out = b_ref[...]
        for tap in range(CONV_W):
            out = out + u_sc[pl.ds(HALO - (CONV_W - 1) + tap, tm), :] * cw[tap:tap + 1]
        return out

    act = _gelu_glu(conv(ua_sc, cwa_ref, ba_ref), conv(ug_sc, cwg_ref, bg_ref))
    acc_sc[...] += _mm(act.astype(BF16), wd_ref[...])

    @pl.when(j == pl.num_programs(1) - 1)
    def _():
        o_ref[...] = x_ref[...] + _rms(acc_sc[...], gpost_ref[...])


def _ffn_prompt_call(x, g_pre, g_post, wa, wg, cwa, cwg, ba, bg, wd, s):
    m, d = x.shape
    fp = wa.shape[1]
    tm = _pick(s, (512, 256, 128))
    tf = _pick(fp, (512, 256, 128))
    hb = tm // HALO
    col = lambda r: pl.BlockSpec((r, tf), lambda i, j: (0, j))
    return pl.pallas_call(
        functools.partial(_ffn_prompt_kernel, tm=tm, blocks_per_seq=s // tm),
        grid=(m // tm, fp // tf),
        in_specs=[pl.BlockSpec((tm, d), lambda i, j: (i, 0)),
                  pl.BlockSpec((HALO, d), lambda i, j: (jnp.maximum(i * hb - 1, 0), 0)),
                  pl.BlockSpec((1, d), lambda i, j: (0, 0)), pl.BlockSpec((1, d), lambda i, j: (0, 0)),
                  col(d), col(d), col(CONV_W), col(CONV_W), col(1), col(1),
                  pl.BlockSpec((tf, d), lambda i, j: (j, 0))],
        out_specs=pl.BlockSpec((tm, d), lambda i, j: (i, 0)),
        out_shape=jax.ShapeDtypeStruct((m, d), F32),
        scratch_shapes=[pltpu.VMEM((tm + HALO, d), BF16), pltpu.VMEM((tm + HALO, tf), F32),
                        pltpu.VMEM((tm + HALO, tf), F32), pltpu.VMEM((tm, d), F32)],
        compiler_params=_cp(("parallel", "arbitrary")),
        name="conv_ffn_prompt",
    )(x, x, g_pre.reshape(1, d), g_post.reshape(1, d), wa, wg, cwa, cwg, ba, bg, wd)


def _ffn_sample_kernel(ua_ref, ug_ref, bufa_ref, bufg_ref, cwa_ref, cwg_ref, ba_ref, bg_ref, wd_ref, x_ref,
                       gpost_ref, o_ref, acc_sc):
    j = pl.program_id(0)

    @pl.when(j == 0)
    def _():
        acc_sc[...] = jnp.zeros(acc_sc.shape, F32)

    def conv(u_ref, buf_ref, cw_ref, b_ref):
        cw = cw_ref[...]
        out = b_ref[...] + u_ref[...] * cw[CONV_W - 1:CONV_W]
        for tap in range(CONV_W - 1):
            out = out + buf_ref[tap] * cw[tap:tap + 1]
        return out

    act = _gelu_glu(conv(ua_ref, bufa_ref, cwa_ref, ba_ref), conv(ug_ref, bufg_ref, cwg_ref, bg_ref))
    acc_sc[...] += _mm(act.astype(BF16), wd_ref[...])

    @pl.when(j == pl.num_programs(0) - 1)
    def _():
        o_ref[...] = x_ref[...] + _rms(acc_sc[...], gpost_ref[...])


def _ffn_sample_call(u, bufa, bufg, cwa, cwg, ba, bg, wd, x, g_post):
    bsz, d = x.shape
    fp = wd.shape[0]
    tf = _pick(fp, (512, 256, 128))
    nf = fp // tf
    col = lambda r: pl.BlockSpec((r, tf), lambda j: (0, j))
    return pl.pallas_call(
        _ffn_sample_kernel,
        grid=(nf,),
        in_specs=[pl.BlockSpec((bsz, tf), lambda j: (0, j)), pl.BlockSpec((bsz, tf), lambda j: (0, nf + j)),
                  pl.BlockSpec((CONV_W - 1, bsz, tf), lambda j: (0, 0, j)),
                  pl.BlockSpec((CONV_W - 1, bsz, tf), lambda j: (0, 0, j)),
                  col(CONV_W), col(CONV_W), col(1), col(1),
                  pl.BlockSpec((tf, d), lambda j: (j, 0)),
                  pl.BlockSpec((bsz, d), lambda j: (0, 0)), pl.BlockSpec((1, d), lambda j: (0, 0))],
        out_specs=pl.BlockSpec((bsz, d), lambda j: (0, 0)),
        out_shape=jax.ShapeDtypeStruct((bsz, d), F32),
        scratch_shapes=[pltpu.VMEM((bsz, d), F32)],
        compiler_params=_cp(("arbitrary",)),
        name="conv_ffn_sample",
    )(u, u, bufa, bufg, cwa, cwg, ba, bg, wd, x, g_post.reshape(1, d))


def _diff_decode_kernel(pt_ref, lam_ref, q_ref, kvn_ref, g_ref, *rest, n_pp, out_scale):
    k_refs = rest[:n_pp]
    v_refs = rest[n_pp:2 * n_pp]
    o_ref = rest[2 * n_pp]
    q_sc, m_sc, l_sc, acc_sc = rest[2 * n_pp + 1:]
    p = pl.program_id(1)
    nh = DIFF_HEADS
    nr = 2 * nh
    prow = PAGE_SIZE * nh

    @pl.when(p == 0)
    def _():
        q8 = q_ref[0]
        lane = lax.broadcasted_iota(jnp.int32, (nh, LANES), 1)
        scale = DIFF_QK_DIM ** -0.5
        q_sc[0:nh] = (jnp.where(lane < DIFF_QK_DIM, q8, 0.0) * scale).astype(BF16).astype(F32)
        q_sc[nh:nr] = (jnp.where(lane >= DIFF_QK_DIM, q8, 0.0) * scale).astype(BF16).astype(F32)
        m_sc[...] = jnp.full(m_sc.shape, NEG, F32)
        l_sc[...] = jnp.zeros(l_sc.shape, F32)
        acc_sc[...] = jnp.zeros(acc_sc.shape, F32)

    def own_head(width):
        lane = lax.broadcasted_iota(jnp.int32, (nr, width), 1)
        row = lax.broadcasted_iota(jnp.int32, (nr, width), 0)
        return (lane & (nh - 1)) == (row & (nh - 1))

    def update(s, keep, pv):
        sm = jnp.where(keep, s, NEG)
        m_old = m_sc[...]
        m_new = jnp.maximum(m_old, jnp.max(sm, axis=-1, keepdims=True))
        corr = jnp.exp(m_old - m_new)
        pe = jnp.where(keep, jnp.exp(sm - m_new[:, 0:1]), 0.0)
        m_sc[...] = m_new
        l_sc[...] = l_sc[...] * corr + jnp.sum(pe, axis=-1, keepdims=True)
        acc_sc[...] = acc_sc[...] * corr + pv(pe.astype(BF16))

    qb = q_sc[...].astype(BF16)
    s = jnp.concatenate([_nt(qb, kr[...].reshape(prow, LANES).astype(BF16)) for kr in k_refs], axis=1)

    def pv_pages(pb):
        out = None
        for i, vr in enumerate(v_refs):
            t = _mm(pb[:, i * prow:(i + 1) * prow], vr[...].reshape(prow, LANES).astype(BF16))
            out = t if out is None else out + t
        return out

    update(s, own_head(n_pp * prow), pv_pages)

    @pl.when(p == pl.num_programs(1) - 1)
    def _():
        kvn = kvn_ref[0]
        vk = jnp.concatenate([kvn[nh:nr], kvn[0:nh]], axis=0).astype(BF16)
        is_key = lax.broadcasted_iota(jnp.int32, (nr, nr), 1) < nh
        update(_nt(qb, kvn.astype(BF16)), own_head(nr) & is_key, lambda pb: _mm(pb, vk))
        o = acc_sc[...] / l_sc[...]
        od = o[0:nh] - lam_ref[0] * o[nh:nr]
        o_ref[0] = _rms(od, g_ref[...]) * out_scale


def _diff_decode_call(page_table, lam, q, kv_new, subln, cache, lam_init, n_pp):
    bsz, n_pages = page_table.shape
    assert n_pages % n_pp == 0
    page_spec = lambda kk, half: pl.BlockSpec((None, PAGE_SIZE, DIFF_HEADS, LANES),
                                              lambda b, p, pt, lam_: (pt[b, p * n_pp + kk], 0, half, 0))
    st = pltpu.VMEM((2 * DIFF_HEADS, LANES), F32)
    return pl.pallas_call(
        functools.partial(_diff_decode_kernel, n_pp=n_pp, out_scale=1.0 - lam_init),
        grid_spec=pltpu.PrefetchScalarGridSpec(
            num_scalar_prefetch=2,
            grid=(bsz, n_pages // n_pp),
            in_specs=[pl.BlockSpec((1, DIFF_HEADS, LANES), lambda b, p, pt, lam_: (b, 0, 0)),
                      pl.BlockSpec((1, 2 * DIFF_HEADS, LANES), lambda b, p, pt, lam_: (b, 0, 0)),
                      pl.BlockSpec((1, LANES), lambda b, p, pt, lam_: (0, 0))]
                     + [page_spec(kk, 0) for kk in range(n_pp)] + [page_spec(kk, 1) for kk in range(n_pp)],
            out_specs=pl.BlockSpec((1, DIFF_HEADS, LANES), lambda b, p, pt, lam_: (b, 0, 0)),
            scratch_shapes=[st, st, st, st],
        ),
        out_shape=jax.ShapeDtypeStruct((bsz, DIFF_HEADS, LANES), F32),
        compiler_params=_cp(("parallel", "arbitrary")),
        name="diff_attn_sample",
    )(page_table, lam, q, kv_new, subln.reshape(1, LANES), *([cache] * (2 * n_pp)))


def _compress_pages_kernel(pt_ref, w_ref, *rest, n_pp):
    page_refs = rest[:n_pp]
    o_ref = rest[n_pp]
    acc_sc = rest[n_pp + 1]
    bpp = PAGE_SIZE // CMP_BLOCK
    d = NSA_HEAD_DIM
    nslot = 4 * NSA_KV_HEADS
    nblk = n_pp * bpp

    jc = 8
    acc = None
    for j0 in range(0, CMP_BLOCK, jc):
        cols = []
        for j in range(j0, j0 + jc):
            tiles = [pr[blk * CMP_BLOCK + j] for pr in page_refs for blk in range(bpp)]
            cols.append(jnp.concatenate(tiles, axis=0).astype(BF16))
        part = _mm(jnp.concatenate(cols, axis=1), w_ref[j0:j0 + jc].reshape(jc * d, 2 * d))
        acc = part if acc is None else acc + part
    acc_sc[0] = acc[:, :d]
    acc_sc[1] = acc[:, d:]
    for s in range(2):
        for h in range(NSA_KV_HEADS):
            o_ref[0, s, h] = acc_sc[s, pl.ds(s * NSA_KV_HEADS + h, nblk, stride=nslot), :]


def _compress_pages_call(page_table, w_cat, cache_rows, n_pp):
    bsz, n_pages = page_table.shape
    assert n_pages % n_pp == 0
    bpp = PAGE_SIZE // CMP_BLOCK
    nslot = 4 * NSA_KV_HEADS
    page_spec = lambda kk: pl.BlockSpec((None, PAGE_SIZE, nslot, NSA_HEAD_DIM),
                                        lambda b, p, pt: (pt[b, p * n_pp + kk], 0, 0, 0))
    return pl.pallas_call(
        functools.partial(_compress_pages_kernel, n_pp=n_pp),
        grid_spec=pltpu.PrefetchScalarGridSpec(
            num_scalar_prefetch=1,
            grid=(bsz, n_pages // n_pp),
            in_specs=[pl.BlockSpec(w_cat.shape, lambda b, p, pt: (0, 0, 0))]
                     + [page_spec(kk) for kk in range(n_pp)],
            out_specs=pl.BlockSpec((1, 2, NSA_KV_HEADS, n_pp * bpp, NSA_HEAD_DIM),
                                   lambda b, p, pt: (b, 0, 0, p, 0)),
            scratch_shapes=[pltpu.VMEM((2, n_pp * bpp * nslot, NSA_HEAD_DIM), F32)],
        ),
        out_shape=jax.ShapeDtypeStruct((bsz, 2, NSA_KV_HEADS, n_pages * bpp, NSA_HEAD_DIM), F32),
        compiler_params=_cp(("parallel", "arbitrary")),
        name="nsa_compress_pages",
    )(page_table, w_cat, *([cache_rows] * n_pp))


def _nsa_sample_select_kernel(q_ref, kc_ref, vc_ref, new_ref, w0_ref, ocmp_ref, idx_ref, *, past_len):
    kvh = pl.program_id(1)
    nbp = kc_ref.shape[3]
    nbt = nbp + 1
    qpos = past_len
    scale = NSA_HEAD_DIM ** -0.5
    qg = q_ref[0, 0].astype(BF16)
    kc = kc_ref[0, 0, 0].astype(BF16)
    vc = vc_ref[0, 0, 0].astype(BF16)
    new = new_ref[0]
    pick = lambda r: jnp.where(kvh == 0, new[r:r + 1], new[r + 1:r + 2])
    kc_new = _mm(jnp.broadcast_to(pick(0), (8, LANES)).astype(BF16), w0_ref[0].astype(BF16))
    vc_new = _mm(jnp.broadcast_to(pick(2), (8, LANES)).astype(BF16), w0_ref[1].astype(BF16))

    s_past = _nt(qg, kc) * scale
    s_new = _nt(qg, kc_new.astype(BF16))[:, 0:1] * scale
    n_past = lax.broadcasted_iota(jnp.int32, (8, nbp), 1)
    mask_past = ((n_past + 1) * CMP_BLOCK - 1) <= qpos
    mask_new = ((nbt * CMP_BLOCK - 1) <= qpos)
    s_past = jnp.where(mask_past, s_past, NEG)
    s_new = s_new if mask_new else jnp.full_like(s_new, NEG)
    m = jnp.maximum(jnp.max(s_past, axis=-1, keepdims=True), s_new)
    p_past = jnp.where(mask_past, jnp.exp(s_past - m), 0.0)
    p_new = jnp.exp(s_new - m) * (1.0 if mask_new else 0.0)
    den = jnp.maximum(jnp.sum(p_past, axis=-1, keepdims=True) + p_new, TINY)
    p_past = p_past / den
    p_new = p_new / den
    o_cmp = _mm(p_past.astype(BF16), vc) + p_new.astype(BF16).astype(F32) * vc_new.astype(BF16).astype(F32)[0:1]
    ocmp_ref[0, 0] = o_cmp

    grp = lax.broadcasted_iota(jnp.int32, (8, 1), 0) < NSA_GROUP
    ps_past = jnp.sum(jnp.where(grp, p_past, 0.0), axis=0, keepdims=True)
    ps_new = jnp.sum(jnp.where(grp, p_new, 0.0), axis=0, keepdims=True)

    width = ((nbt + LANES - 1) // LANES) * LANES
    nrow = ((nbt + 7) // 8) * 8
    n = lax.broadcasted_iota(jnp.int32, (1, width), 1)
    p_all = jnp.concatenate([ps_past, jnp.broadcast_to(ps_new, (1, width - nbp))], axis=1)
    cur = qpos >> CMP_SHIFT
    valid = (n * CMP_BLOCK <= qpos) & (n < nbt)
    forced = (n == 0) | (n == cur) | (n == cur - 1)
    score = jnp.where(valid, p_all + jnp.where(forced, FORCE_BONUS, 0.0), NEG)
    mrow = lax.broadcasted_iota(jnp.int32, (nrow, width), 0)
    ncol = lax.broadcasted_iota(jnp.int32, (nrow, width), 1)
    score_b = jnp.broadcast_to(score, (nrow, width))
    score_col = jnp.sum(jnp.where(mrow == ncol, score_b, 0.0), axis=1, keepdims=True)
    col_ok = lax.broadcasted_iota(jnp.int32, (nrow, 1), 0) < nbt
    beats = col_ok & ((score_col > score_b) | ((score_col == score_b) & (mrow < ncol)))
    rank = jnp.sum(jnp.where(beats, 1.0, 0.0), axis=0, keepdims=True)
    n_sel = min(N_SEL, nbt)
    r = lax.broadcasted_iota(jnp.int32, (N_SEL, width), 0)
    hit = ((jnp.broadcast_to(rank, (N_SEL, width)) == r.astype(F32)) & jnp.broadcast_to(valid, (N_SEL, width))
           & (r < n_sel))
    nf = jnp.broadcast_to(n, (N_SEL, width)).astype(F32)
    idx = jnp.sum(jnp.where(hit, nf, 0.0), axis=1, keepdims=True)
    any_hit = jnp.sum(jnp.where(hit, 1.0, 0.0), axis=1, keepdims=True)
    idx = jnp.where(any_hit > 0.5, idx, -1.0).astype(jnp.int32)
    idx_ref[0, 0] = jnp.broadcast_to(idx, (N_SEL, LANES))


def _nsa_sample_select_call(q8, kcvc_p, new_rows, w0, past_len):
    bsz = q8.shape[0]
    nbp = kcvc_p.shape[3]
    return pl.pallas_call(
        functools.partial(_nsa_sample_select_kernel, past_len=past_len),
        grid=(bsz, NSA_KV_HEADS),
        in_specs=[pl.BlockSpec((1, 1, 8, LANES), lambda b, h: (b, h, 0, 0)),
                  pl.BlockSpec((1, 1, 1, nbp, LANES), lambda b, h: (b, 0, h, 0, 0)),
                  pl.BlockSpec((1, 1, 1, nbp, LANES), lambda b, h: (b, 1, h, 0, 0)),
                  pl.BlockSpec((1, 8, LANES), lambda b, h: (b, 0, 0)),
                  pl.BlockSpec((2, LANES, LANES), lambda b, h: (0, 0, 0))],
        out_specs=[pl.BlockSpec((1, 1, 8, LANES), lambda b, h: (b, h, 0, 0)),
                   pl.BlockSpec((1, 1, N_SEL, LANES), lambda b, h: (b, h, 0, 0))],
        out_shape=[jax.ShapeDtypeStruct((bsz, NSA_KV_HEADS, 8, LANES), F32),
                   jax.ShapeDtypeStruct((bsz, NSA_KV_HEADS, N_SEL, LANES), jnp.int32)],
        compiler_params=_cp(("parallel", "parallel")),
        name="nsa_select_sample",
    )(q8, kcvc_p, kcvc_p, new_rows, w0)


def _nsa_sample_attend_kernel(page_ref, half_ref, flag_ref, q_ref, new_ref, wnew_ref, gate_ref, ocmp_ref, win_ref,
                              *rest, past_len):
    blk_refs = rest[:N_SEL]
    o_ref = rest[N_SEL]
    b = pl.program_id(0)
    kvh = pl.program_id(1)
    nkv = NSA_KV_HEADS
    base = (b * nkv + kvh) * (N_SEL + 1)
    scale = NSA_HEAD_DIM ** -0.5
    qg = q_ref[0, 0].astype(BF16)
    new = new_ref[0]
    wnew = wnew_ref[0]
    pick = lambda arr, r: jnp.where(kvh == 0, arr[r:r + 1], arr[r + 1:r + 2])
    rnd = lambda a: a.astype(BF16).astype(F32)
    qf = qg.astype(F32)

    def attend(x, keep, k_new, v_new, new_on):
        s = jnp.where(keep, _nt(qg, x) * scale, NEG)
        s_n = jnp.sum(qf * rnd(k_new), axis=-1, keepdims=True) * scale
        s_n = jnp.where(new_on, s_n, NEG)
        m = jnp.maximum(jnp.max(s, axis=-1, keepdims=True), s_n)
        p = jnp.where(keep, jnp.exp(s - m), 0.0)
        p_n = jnp.where(new_on, jnp.exp(s_n - m), 0.0)
        den = jnp.maximum(jnp.sum(p, axis=-1, keepdims=True) + p_n, TINY)
        p_v = pltpu.roll(p, nkv, 1)
        return (_mm(p_v.astype(BF16), x) + rnd(p_n) * rnd(v_new)) / den

    nslot = 4 * nkv
    brow = CMP_BLOCK * nslot
    x_sel = jnp.concatenate([r[...].reshape(brow, LANES).astype(BF16) for r in blk_refs], axis=0)
    lane = lax.broadcasted_iota(jnp.int32, (1, N_SEL * brow), 1)
    chunk = lax.shift_right_logical(lane, brow.bit_length() - 1)
    live = jnp.zeros((1, N_SEL * brow), jnp.int32)
    for r in range(N_SEL):
        live = jnp.where(chunk == r, flag_ref[base + r], live)
    keep = (live > 0) & ((lane & (nslot - 1)) == 2 * nkv + kvh)
    o_slc = attend(x_sel, jnp.broadcast_to(keep, (8, N_SEL * brow)), pick(new, 4), pick(new, 6),
                   flag_ref[base + N_SEL] > 0)

    wrows = win_ref.shape[1]
    wb = wrows // (2 * nkv)
    wl = lax.broadcasted_iota(jnp.int32, (8, wrows), 1)
    kpos = past_len - wb + lax.shift_right_logical(wl, (2 * nkv).bit_length() - 1)
    dist = past_len - kpos
    wkeep = (dist >= 0) & (dist <= WINDOW) & (kpos >= 0) & ((wl & (2 * nkv - 1)) == kvh)
    o_win = attend(win_ref[0].astype(BF16), wkeep, pick(wnew, 0), pick(wnew, 2), True)

    gate = gate_ref[0, 0]
    o_ref[0, 0] = (gate[:, 0:LANES] * ocmp_ref[0, 0] + gate[:, LANES:2 * LANES] * o_slc
                   + gate[:, 2 * LANES:3 * LANES] * o_win)


def _nsa_sample_attend_call(pages, halves, flags, q8, new_rows, win_new, gates, o_cmp, win_rows, cache_rows,
                            past_len):
    bsz = q8.shape[0]
    kvh = NSA_KV_HEADS
    nslot = 4 * kvh
    sel = lambda ref, b, h, r: ref[(b * kvh + h) * (N_SEL + 1) + r]
    blk_spec = lambda r: pl.BlockSpec((None, CMP_BLOCK, nslot, LANES),
                                      lambda b, h, pg, hf, fl: (sel(pg, b, h, r), sel(hf, b, h, r), 0, 0))
    return pl.pallas_call(
        functools.partial(_nsa_sample_attend_kernel, past_len=past_len),
        grid_spec=pltpu.PrefetchScalarGridSpec(
            num_scalar_prefetch=3,
            grid=(bsz, kvh),
            in_specs=[pl.BlockSpec((1, 1, 8, LANES), lambda b, h, pg, hf, fl: (b, h, 0, 0)),
                      pl.BlockSpec((1, 8, LANES), lambda b, h, pg, hf, fl: (b, 0, 0)),
                      pl.BlockSpec((1, 4, LANES), lambda b, h, pg, hf, fl: (b, 0, 0)),
                      pl.BlockSpec((1, 1, 8, 3 * LANES), lambda b, h, pg, hf, fl: (b, h, 0, 0)),
                      pl.BlockSpec((1, 1, 8, LANES), lambda b, h, pg, hf, fl: (b, h, 0, 0)),
                      pl.BlockSpec((1, win_rows.shape[1], LANES), lambda b, h, pg, hf, fl: (b, 0, 0))]
                     + [blk_spec(r) for r in range(N_SEL)],
            out_specs=pl.BlockSpec((1, 1, 8, LANES), lambda b, h, pg, hf, fl: (b, h, 0, 0)),
        ),
        out_shape=jax.ShapeDtypeStruct((bsz, kvh, 8, LANES), F32),
        compiler_params=_cp(("parallel", "parallel")),
        name="nsa_attend_sample",
    )(pages, halves, flags, q8, new_rows, win_new, gates, o_cmp, win_rows, *([cache_rows] * N_SEL))


def _pad_cols(a, n):
    return jnp.pad(a, [(0, 0)] * (a.ndim - 1) + [(0, n - a.shape[-1])])


def kernel(x_prompt, x_sample, cache_diff_kv, cache_nsa_kv, state_nsa_win, state_ffn_conv, page_table,
           norm_mix_pre, norm_mix_post, w_in, diff_lambda, diff_subln, nsa_w_cmp, w_branch_diff, w_branch_nsa,
           w_out, norm_ffn_pre, norm_ffn_post, ffn_w_up, ffn_conv_w, ffn_conv_b, ffn_w_down):
    b, s, d = x_prompt.shape
    bs, ts, _ = x_sample.shape
    depth = w_in.shape[0]
    assert depth == 1 and ts == 1
    n_pages = page_table.shape[1]
    past_len = n_pages * PAGE_SIZE
    f = ffn_w_down.shape[1]
    t = b * s
    kvh, hd = NSA_KV_HEADS, NSA_HEAD_DIM
    layer = 0
    lam_init = 0.8 - 0.6 * math.exp(-0.3 * layer)

    wq = DIFF_HEADS * 2 * DIFF_QK_DIM
    wv = DIFF_HEADS * DIFF_V_DIM
    wnq = NSA_HEADS * hd
    wkv = 2 * kvh * hd
    o_dq, o_dk, o_dv = 0, wq, 2 * wq
    o_nq = o_dv + wv
    o_cs = o_nq + wnq
    o_win = o_cs + 2 * wkv
    o_ng = o_win + wkv
    o_mg = o_ng + 3 * NSA_HEADS
    w = w_in[layer]
    w_dq = w[:, o_dq:o_dk].astype(BF16)
    w_dkv = w[:, o_dk:o_nq].astype(BF16)
    w_nq = w[:, o_nq:o_cs].astype(BF16)
    w_cs = w[:, o_cs:o_win].astype(BF16)
    w_win = w[:, o_win:o_ng].astype(BF16)
    w_ng = _pad_cols(w[:, o_ng:o_mg], LANES).astype(BF16)
    w_mg = w[:, o_mg:].astype(BF16)
    w_bd = w_branch_diff[layer].astype(BF16)
    w_bn = w_branch_nsa[layer].astype(BF16)
    w_o = w_out[layer].astype(BF16)
    fp = ((f + 511) // 512) * 512
    w_up = ffn_w_up[layer]
    wa = _pad_cols(w_up[:, :f], fp).astype(BF16)
    wg = _pad_cols(w_up[:, f:], fp).astype(BF16)
    cw = ffn_conv_w[layer]
    cwa, cwg = _pad_cols(cw[:, :f], fp), _pad_cols(cw[:, f:], fp)
    cb = ffn_conv_b[layer].reshape(1, 2 * f)
    ba, bg = _pad_cols(cb[:, :f], fp), _pad_cols(cb[:, f:], fp)
    wd = jnp.pad(ffn_w_down[layer], ((0, fp - f), (0, 0))).astype(BF16)
    w_cmp = nsa_w_cmp[layer]

    lam = _lam_call(diff_lambda[layer], lam_init)

    def projections(h, pos, pos_rows, attn_dtype):
        t64 = _rope_tables(pos, DIFF_QK_DIM)
        t128 = _rope_tables(pos, hd)
        r64, r128, nn, sg = MODE_ROPE64, MODE_ROPE128, MODE_NONE, MODE_SIGMOID
        dq = _proj_call(h, w_dq, [r64] * (wq // 512), 512, attn_dtype, t64, pos_rows, "proj_diff_q")
        dkv = _proj_call(h, w_dkv, [r64] * (wq // 512) + [nn] * (wv // 512), 512, F32, t64, pos_rows, "proj_diff_kv")
        nq = _proj_call(h, w_nq, [r128] * (wnq // 512), 512, attn_dtype, t128, pos_rows, "proj_nsa_q")
        cs = _proj_call(h, w_cs, [r128, nn, r128, nn], wkv // 2, F32, t128, pos_rows, "proj_nsa_kv")
        win = _proj_call(h, w_win, [r128, nn], wkv // 2, F32, t128, pos_rows, "proj_nsa_win")
        ng = _proj_call(h, w_ng, [sg], LANES, F32, (), None, "proj_nsa_gate")
        return dq, dkv, nq, cs, win, ng

    xp = x_prompt.reshape(t, d)
    hp = _norm_call(xp, norm_mix_pre[layer])
    pos_p = jnp.arange(s, dtype=jnp.int32)
    dq, dkv, nq, cs, win, ng = projections(hp, pos_p, s, BF16)
    diff = _diff_prompt_call(lam, dq, dkv, diff_subln[layer], b, s, lam_init)
    nb = s // CMP_BLOCK
    blocks = cs.reshape(b, nb, CMP_BLOCK, 4, kvh, hd)[:, :, :, :2]
    blocks = blocks.transpose(3, 4, 0, 1, 2, 5).reshape(2, kvh, b * nb, CMP_BLOCK * hd)
    kcvc = _compress_call(blocks, w_cmp.reshape(2, CMP_BLOCK * hd, hd))
    nsa = _nsa_prompt_call(nq, kcvc, cs, win, ng, b, s)
    mix = _merge_call(hp, diff, nsa, w_mg, w_bd, w_bn)
    xp1 = _out_proj_call(mix, w_o, xp, norm_mix_post[layer])
    xp2 = _ffn_prompt_call(xp1, norm_ffn_pre[layer], norm_ffn_post[layer], wa, wg, cwa, cwg, ba, bg, wd, s)

    wbp = min(WINDOW, s)
    new_diff_kv_prompt = dkv.reshape(1, b, s, 2, DIFF_HEADS, 2 * DIFF_QK_DIM)
    new_nsa_kv_prompt = cs.reshape(1, b, s, 4, kvh, hd)
    new_win_prompt = win.reshape(b, s, 2, kvh, hd)[None, :, s - wbp:]

    xs = x_sample.reshape(bs, d)
    hs = _norm_call(xs, norm_mix_pre[layer])
    pos_s = jnp.full((bs,), past_len, jnp.int32)
    dq_s, dkv_s, nq_s, cs_s, win_s, ng_s = projections(hs, pos_s, bs, F32)

    cache_d = cache_diff_kv.reshape(cache_diff_kv.shape[1], PAGE_SIZE, 2 * DIFF_HEADS, LANES)
    diff_s = _diff_decode_call(page_table, lam, dq_s.reshape(bs, DIFF_HEADS, LANES),
                               dkv_s.reshape(bs, 2 * DIFF_HEADS, LANES), diff_subln[layer], cache_d, lam_init,
                               _pick(n_pages, (8, 4, 2, 1)))
    diff_s = diff_s.reshape(bs, DIFF_HEADS * LANES).astype(BF16)

    n_pool = cache_nsa_kv.shape[1]
    cache_n = cache_nsa_kv.reshape(n_pool, PAGE_SIZE, 4 * kvh, hd)
    w_cat = jnp.concatenate([w_cmp[0], w_cmp[1]], axis=-1).astype(BF16)
    kcvc_p = _compress_pages_call(page_table, w_cat, cache_n, _pick(n_pages, (8, 4, 2, 1)))
    q8 = jnp.pad(nq_s.reshape(bs, kvh, NSA_GROUP, hd), ((0, 0), (0, 0), (0, 8 - NSA_GROUP), (0, 0)))
    new_rows = cs_s.reshape(bs, 4 * kvh, hd)
    o_cmp_s, idx_s = _nsa_sample_select_call(q8, kcvc_p, new_rows, w_cmp[:, 0], past_len)
    idx = idx_s[..., 0]
    nbp = past_len // CMP_BLOCK
    bpp = PAGE_SIZE // CMP_BLOCK
    is_past = (idx >= 0) & (idx < nbp)
    safe = jnp.clip(idx, 0, nbp - 1)
    page = jnp.take_along_axis(page_table, (safe // bpp).reshape(bs, -1), axis=1).reshape(idx.shape)
    new_sel = jnp.any(idx == nbp, axis=-1, keepdims=True)
    pad1 = lambda a: jnp.concatenate([a, jnp.zeros_like(a[..., :1])], axis=-1).reshape(-1).astype(jnp.int32)
    pages_sel, halves_sel = pad1(page), pad1(safe % bpp)
    flags = jnp.concatenate([is_past, new_sel], axis=-1).reshape(-1).astype(jnp.int32)
    gates = ng_s[:, :3 * NSA_HEADS].reshape(bs, 3, kvh, NSA_GROUP).transpose(0, 2, 3, 1)
    gates = jnp.pad(gates, ((0, 0), (0, 0), (0, 8 - NSA_GROUP), (0, 0)))
    gates = jnp.broadcast_to(gates[..., None], (bs, kvh, 8, 3, LANES)).reshape(bs, kvh, 8, 3 * LANES)
    wbs = state_nsa_win.shape[2]
    win_rows = state_nsa_win.reshape(bs, wbs * 2 * kvh, hd)
    nsa_s = _nsa_sample_attend_call(pages_sel, halves_sel, flags, q8, new_rows, win_s.reshape(bs, 2 * kvh, hd),
                                    gates, o_cmp_s, win_rows, cache_n, past_len)
    nsa_s = nsa_s[:, :, :NSA_GROUP].reshape(bs, NSA_HEADS * hd).astype(BF16)

    mix_s = _merge_call(hs, diff_s, nsa_s, w_mg, w_bd, w_bn)
    xs1 = _out_proj_call(mix_s, w_o, xs, norm_mix_post[layer])

    tail = xp1.reshape(b, s, d)[:, s - (CONV_W - 1):].reshape(b * (CONV_W - 1), d)
    rows = jnp.concatenate([xs1, tail], axis=0)
    pad_r = (-rows.shape[0]) % 16
    rows = jnp.pad(rows, ((0, pad_r), (0, 0)))
    h_rows = _norm_call(rows, norm_ffn_pre[layer])
    w_up_p = jnp.concatenate([wa, wg], axis=1)
    u_rows = _proj_call(h_rows, w_up_p, [MODE_NONE] * (2 * fp // 512), 512, F32, (), None, "proj_ffn_up_rows")
    u_unpad = jnp.concatenate([u_rows[:, :f], u_rows[:, fp:fp + f]], axis=1)
    buf = state_ffn_conv[layer]
    bufa = _pad_cols(buf[..., :f], fp).transpose(1, 0, 2)
    bufg = _pad_cols(buf[..., f:], fp).transpose(1, 0, 2)
    xs2 = _ffn_sample_call(u_rows, bufa, bufg, cwa, cwg, ba, bg, wd, xs1, norm_ffn_post[layer])

    new_conv_prompt = u_unpad[bs:bs + b * (CONV_W - 1)].reshape(1, b, CONV_W - 1, 2 * f)
    new_conv_sample = jnp.concatenate([buf[:, 1:], u_unpad[:bs, None]], axis=1)[None]
    new_diff_kv_sample = dkv_s.reshape(1, bs, 1, 2, DIFF_HEADS, 2 * DIFF_QK_DIM)
    new_nsa_kv_sample = cs_s.reshape(1, bs, 1, 4, kvh, hd)
    new_win_sample = jnp.concatenate([state_nsa_win[layer][:, 1:], win_s.reshape(bs, 1, 2, kvh, hd)], axis=1)[None]

    return (xp2.reshape(b, s, d), xs2.reshape(bs, 1, d), new_diff_kv_prompt, new_diff_kv_sample,
            new_nsa_kv_prompt, new_nsa_kv_sample, new_win_prompt, new_win_sample, new_conv_prompt, new_conv_sample)
```

```python
import functools
import math

import jax
import jax.numpy as jnp
from jax import lax
from jax.experimental import pallas as pl
from jax.experimental.pallas import tpu as pltpu

F32 = jnp.float32
BF16 = jnp.bfloat16

DIFF_HEADS = 8
DIFF_QK_DIM = 64
DIFF_V_DIM = 128
NSA_HEADS = 8
NSA_KV_HEADS = 2
NSA_GROUP = 4
NSA_HEAD_DIM = 128
CMP_BLOCK = 64
CMP_SHIFT = 6
N_SEL = 16
WINDOW = 512
PAGE_SIZE = 128
CONV_W = 3
FORCE_BONUS = 1e4
ROPE_THETA = 10000.0
EPS = 1e-6
NEG = -1e30
TINY = 1e-30
LANES = 128
VMEM_LIMIT = 52 * 1024 * 1024

_NT = (((1,), (1,)), ((), ()))


def _nt(a, b):
    return lax.dot_general(a, b, _NT, preferred_element_type=F32)


def _mm(a, b):
    return jnp.dot(a, b, preferred_element_type=F32)


def _cp(sem):
    return pltpu.CompilerParams(dimension_semantics=sem, vmem_limit_bytes=VMEM_LIMIT)


def _pick(n, cands):
    for c in cands:
        if n % c == 0:
            return c
    return n


def _rms(x, g):
    return x * lax.rsqrt(jnp.mean(x * x, axis=-1, keepdims=True) + EPS) * g


def _lam_kernel(l_ref, o_ref, *, lam_init):
    l = l_ref[...]
    a = jnp.sum(l[0:1] * l[1:2], axis=-1, keepdims=True)
    b = jnp.sum(l[2:3] * l[3:4], axis=-1, keepdims=True)
    o_ref[...] = jnp.broadcast_to(jnp.exp(a) - jnp.exp(b) + lam_init, o_ref.shape)


def _lam_call(lam_params, lam_init):
    out = pl.pallas_call(
        functools.partial(_lam_kernel, lam_init=lam_init),
        out_shape=jax.ShapeDtypeStruct((8, LANES), F32),
        name="diff_lambda",
    )(lam_params)
    return out[0, :1]


def _norm_kernel(x_ref, g_ref, o_ref):
    o_ref[...] = _rms(x_ref[...], g_ref[...]).astype(o_ref.dtype)


def _norm_call(x, g):
    m, d = x.shape
    tm = _pick(m, (512, 256, 128, 64, 32, 16))
    return pl.pallas_call(
        _norm_kernel,
        grid=(m // tm,),
        in_specs=[pl.BlockSpec((tm, d), lambda i: (i, 0)), pl.BlockSpec((1, d), lambda i: (0, 0))],
        out_specs=pl.BlockSpec((tm, d), lambda i: (i, 0)),
        out_shape=jax.ShapeDtypeStruct((m, d), BF16),
        compiler_params=_cp(("parallel",)),
        name="rmsnorm",
    )(x, g.reshape(1, d))


MODE_NONE, MODE_ROPE64, MODE_ROPE128, MODE_SIGMOID = 0, 1, 2, 3


def _rope_tables(pos, d):
    half = d // 2
    inv = 1.0 / (ROPE_THETA ** (jnp.arange(0, d, 2, dtype=F32) / d))
    ang = pos.astype(F32)[:, None] * inv[None, :]
    lane = jnp.arange(LANES)
    cos = jnp.cos(ang)[:, lane % half]
    sin = jnp.sin(ang)[:, lane % half]
    first = ((lane % d) < half)[None, :]
    return cos, jnp.where(first, -sin, 0.0), jnp.where(first, 0.0, sin)


def _apply_rope(z, cos, sa, sb, half):
    outs = []
    for c in range(z.shape[1] // LANES):
        blk = z[:, c * LANES:(c + 1) * LANES]
        outs.append(blk * cos + pltpu.roll(blk, LANES - half, 1) * sa + pltpu.roll(blk, half, 1) * sb)
    return outs[0] if len(outs) == 1 else jnp.concatenate(outs, axis=1)


def _proj_kernel(*refs, runs, n_tab):
    h_ref, w_ref = refs[0], refs[1]
    tabs = refs[2:2 + n_tab]
    o_ref = refs[2 + n_tab]
    j = pl.program_id(1)
    acc = _mm(h_ref[...], w_ref[...])

    def emit(mode):
        if mode == MODE_NONE:
            o_ref[...] = acc.astype(o_ref.dtype)
        elif mode == MODE_SIGMOID:
            o_ref[...] = jax.nn.sigmoid(acc).astype(o_ref.dtype)
        else:
            half = 32 if mode == MODE_ROPE64 else 64
            o_ref[...] = _apply_rope(acc, tabs[0][...], tabs[1][...], tabs[2][...], half).astype(o_ref.dtype)

    if len(runs) == 1:
        emit(runs[0][0])
    else:
        for mode, j0, j1 in runs:
            pl.when((j >= j0) & (j < j1))(functools.partial(emit, mode))


def _proj_call(h, w, modes, tn, out_dtype, tables=(), pos_rows=None, name="proj"):
    m, k = h.shape
    n = w.shape[1]
    assert n % tn == 0 and len(modes) == n // tn
    tm = _pick(m, (1024, 512, 256, 128, 64, 48, 32, 16))
    runs = []
    for jj, md in enumerate(modes):
        if runs and runs[-1][0] == md:
            runs[-1][2] = jj + 1
        else:
            runs.append([md, jj, jj + 1])
    runs = tuple(tuple(r) for r in runs)
    in_specs = [pl.BlockSpec((tm, k), lambda i, j: (i, 0)), pl.BlockSpec((k, tn), lambda i, j: (0, j))]
    if tables:
        nblk = pos_rows // tm
        assert nblk * tm == pos_rows
        in_specs += [pl.BlockSpec((tm, LANES), lambda i, j: (i % nblk, 0)) for _ in tables]
    return pl.pallas_call(
        functools.partial(_proj_kernel, runs=runs, n_tab=len(tables)),
        grid=(m // tm, n // tn),
        in_specs=in_specs,
        out_specs=pl.BlockSpec((tm, tn), lambda i, j: (i, j)),
        out_shape=jax.ShapeDtypeStruct((m, n), out_dtype),
        compiler_params=_cp(("parallel", "arbitrary")),
        name=name,
    )(h, w, *tables)


def _flash_step(carry, s, v_ext):
    m, acc = carry
    m_new = jnp.maximum(m, jnp.max(s, axis=-1, keepdims=True))
    p = jnp.exp(s - m_new)
    return m_new, acc * jnp.exp(m - m_new) + _mm(p.astype(BF16), v_ext)


def _flash_init(rows, d):
    return jnp.full((rows, 1), NEG, F32), jnp.zeros((rows, 2 * d), F32)


def _flash_out(carry, d):
    acc = carry[1]
    return acc[:, :d] / jnp.maximum(acc[:, d:d + 1], TINY)


def _ext_ones(v):
    return jnp.concatenate([v, jnp.ones(v.shape, v.dtype)], axis=1)


def _step_id():
    t = pl.program_id(0)
    for ax in range(1, 3):
        t = t * pl.num_programs(ax) + pl.program_id(ax)
    return t


def _pages_per_step(n_req, n_pages, n_steps):
    for n_pp in range(1, n_pages + 1):
        if n_pages % n_pp == 0 and n_req * (n_pages // n_pp) <= n_steps:
            return n_pp
    raise ValueError("prompt grid too small to carry the sample group's page stream")


def _diff_decode_step(pg, n_pg, lam_ref, q_ref, kvn_ref, g_ref, k_refs, v_refs, o_ref, q_sc, m_sc, l_sc, acc_sc,
                      out_scale):
    nh = DIFF_HEADS
    nr = 2 * nh
    prow = PAGE_SIZE * nh

    @pl.when(pg == 0)
    def _():
        q8 = q_ref[0]
        lane = lax.broadcasted_iota(jnp.int32, (nh, LANES), 1)
        scale = DIFF_QK_DIM ** -0.5
        q_sc[0:nh] = (jnp.where(lane < DIFF_QK_DIM, q8, 0.0) * scale).astype(BF16).astype(F32)
        q_sc[nh:nr] = (jnp.where(lane >= DIFF_QK_DIM, q8, 0.0) * scale).astype(BF16).astype(F32)
        m_sc[...] = jnp.full(m_sc.shape, NEG, F32)
        l_sc[...] = jnp.zeros(l_sc.shape, F32)
        acc_sc[...] = jnp.zeros(acc_sc.shape, F32)

    def own_head(width):
        lane = lax.broadcasted_iota(jnp.int32, (nr, width), 1)
        row = lax.broadcasted_iota(jnp.int32, (nr, width), 0)
        return (lane & (nh - 1)) == (row & (nh - 1))

    def update(s, keep, pv):
        sm = jnp.where(keep, s, NEG)
        m_old = m_sc[...]
        m_new = jnp.maximum(m_old, jnp.max(sm, axis=-1, keepdims=True))
        corr = jnp.exp(m_old - m_new)
        pe = jnp.where(keep, jnp.exp(sm - m_new[:, 0:1]), 0.0)
        m_sc[...] = m_new
        l_sc[...] = l_sc[...] * corr + jnp.sum(pe, axis=-1, keepdims=True)
        acc_sc[...] = acc_sc[...] * corr + pv(pe.astype(BF16))

    qb = q_sc[...].astype(BF16)
    s = jnp.concatenate([_nt(qb, kr[...].reshape(prow, LANES).astype(BF16)) for kr in k_refs], axis=1)

    def pv_pages(pb):
        out = None
        for i, vr in enumerate(v_refs):
            part = _mm(pb[:, i * prow:(i + 1) * prow], vr[...].reshape(prow, LANES).astype(BF16))
            out = part if out is None else out + part
        return out

    update(s, own_head(len(k_refs) * prow), pv_pages)

    @pl.when(pg == n_pg - 1)
    def _():
        kvn = kvn_ref[0]
        vk = jnp.concatenate([kvn[nh:nr], kvn[0:nh]], axis=0).astype(BF16)
        is_key = lax.broadcasted_iota(jnp.int32, (nr, nr), 1) < nh
        update(_nt(qb, kvn.astype(BF16)), own_head(nr) & is_key, lambda pb: _mm(pb, vk))
        o = acc_sc[...] / l_sc[...]
        od = o[0:nh] - lam_ref[0] * o[nh:nr]
        o_ref[0] = _rms(od, g_ref[...]) * out_scale


def _diff_attn_kernel(lam_ref, pt_ref, q_ref, k_ref, v_ref, g_ref, qd_ref, kvn_ref, *rest, tq, out_scale, n_pp,
                      n_pg, dec_steps, n_steps):
    k_pages = rest[:n_pp]
    v_pages = rest[n_pp:2 * n_pp]
    o_ref, od_ref = rest[2 * n_pp:2 * n_pp + 2]
    kb_ref, vb_ref, q_sc, m_sc, l_sc, acc_sc = rest[2 * n_pp + 2:]
    qi = pl.program_id(2)
    dv = DIFF_V_DIM

    @pl.when(qi == 0)
    def _():
        kb_ref[...] = k_ref[...].astype(BF16)
        vb_ref[...] = _ext_ones(v_ref[...].astype(BF16))

    q = q_ref[...]
    lane = lax.broadcasted_iota(jnp.int32, q.shape, 1)
    scale = jnp.asarray(DIFF_QK_DIM ** -0.5, q.dtype)
    q1 = jnp.where(lane < DIFF_QK_DIM, q, 0) * scale
    q2 = jnp.where(lane >= DIFF_QK_DIM, q, 0) * scale

    def block(kb, carry, bias):
        start = pl.multiple_of(kb * tq, tq)
        k = kb_ref[pl.ds(start, tq), :]
        v = vb_ref[pl.ds(start, tq), :]
        s1, s2 = _nt(q1, k), _nt(q2, k)
        if bias is not None:
            s1, s2 = s1 + bias, s2 + bias
        return _flash_step(carry[0], s1, v), _flash_step(carry[1], s2, v)

    init = _flash_init(tq, dv)
    carry = lax.fori_loop(0, qi, lambda kb, c: block(kb, c, None), (init, init))
    causal = (lax.broadcasted_iota(jnp.int32, (tq, tq), 1) <= lax.broadcasted_iota(jnp.int32, (tq, tq), 0))
    c1, c2 = block(qi, carry, jnp.where(causal, 0.0, NEG))
    o = _flash_out(c1, dv) - lam_ref[0] * _flash_out(c2, dv)
    o_ref[...] = (_rms(o, g_ref[...]) * out_scale).astype(o_ref.dtype)

    t = _step_id()
    decode = functools.partial(_diff_decode_step, lax.rem(t, n_pg), n_pg, lam_ref, qd_ref, kvn_ref, g_ref, k_pages,
                               v_pages, od_ref, q_sc, m_sc, l_sc, acc_sc, out_scale)
    if dec_steps == n_steps:
        decode()
    else:
        pl.when(t < dec_steps)(decode)


def _diff_attn_call(lam, page_table, q, kv, subln, q_dec, kv_new, cache, b, s, lam_init):
    t = b * s
    tq = _pick(s, (256, 128))
    nq = s // tq
    nh = DIFF_HEADS
    bs, n_pages = page_table.shape
    n_steps = b * nh * nq
    n_pp = _pages_per_step(bs, n_pages, n_steps)
    n_pg = n_pages // n_pp
    dec_steps = bs * n_pg

    def dec(bi, h, qi):
        td = jnp.minimum((bi * nh + h) * nq + qi, dec_steps - 1)
        return td // n_pg, td % n_pg

    def page_spec(kk, half):
        def imap(bi, h, qi, lam_, pt):
            r, g = dec(bi, h, qi)
            return pt[r, g * n_pp + kk], 0, half, 0
        return pl.BlockSpec((None, PAGE_SIZE, nh, LANES), imap)

    req_spec = lambda rows: pl.BlockSpec((1, rows, LANES), lambda bi, h, qi, lam_, pt: (dec(bi, h, qi)[0], 0, 0))
    st = pltpu.VMEM((2 * nh, LANES), F32)
    return pl.pallas_call(
        functools.partial(_diff_attn_kernel, tq=tq, out_scale=1.0 - lam_init, n_pp=n_pp, n_pg=n_pg,
                          dec_steps=dec_steps, n_steps=n_steps),
        grid_spec=pltpu.PrefetchScalarGridSpec(
            num_scalar_prefetch=2,
            grid=(b, nh, nq),
            in_specs=[
                pl.BlockSpec((tq, LANES), lambda bi, h, qi, lam_, pt: (bi * nq + qi, h)),
                pl.BlockSpec((s, LANES), lambda bi, h, qi, lam_, pt: (bi, h)),
                pl.BlockSpec((s, LANES), lambda bi, h, qi, lam_, pt: (bi, nh + h)),
                pl.BlockSpec((1, LANES), lambda bi, h, qi, lam_, pt: (0, 0)),
                req_spec(nh), req_spec(2 * nh),
            ] + [page_spec(kk, 0) for kk in range(n_pp)] + [page_spec(kk, 1) for kk in range(n_pp)],
            out_specs=[pl.BlockSpec((tq, LANES), lambda bi, h, qi, lam_, pt: (bi * nq + qi, h)), req_spec(nh)],
            scratch_shapes=[pltpu.VMEM((s, LANES), BF16), pltpu.VMEM((s, 2 * LANES), BF16), st, st, st, st],
        ),
        out_shape=[jax.ShapeDtypeStruct((t, nh * DIFF_V_DIM), BF16), jax.ShapeDtypeStruct((bs, nh, LANES), F32)],
        compiler_params=_cp(("arbitrary", "arbitrary", "arbitrary")),
        name="diff_attn",
    )(lam, page_table, q, kv, kv, subln.reshape(1, LANES), q_dec, kv_new, *([cache] * (2 * n_pp)))


def _compress_kernel(a_ref, w_ref, o_ref):
    @pl.when(pl.program_id(2) == 0)
    def _():
        o_ref[...] = jnp.zeros(o_ref.shape, o_ref.dtype)

    o_ref[0, 0] += _mm(a_ref[0, 0].astype(BF16), w_ref[0].astype(BF16))


def _compress_call(blocks, w_cmp):
    _, kvh, r, kk = blocks.shape
    tk = _pick(kk, (2048, 1024, 512))
    return pl.pallas_call(
        _compress_kernel,
        grid=(2, kvh, kk // tk),
        in_specs=[pl.BlockSpec((1, 1, r, tk), lambda s, h, k: (s, h, 0, k)),
                  pl.BlockSpec((1, tk, NSA_HEAD_DIM), lambda s, h, k: (s, k, 0))],
        out_specs=pl.BlockSpec((1, 1, r, NSA_HEAD_DIM), lambda s, h, k: (s, h, 0, 0)),
        out_shape=jax.ShapeDtypeStruct((2, kvh, r, NSA_HEAD_DIM), F32),
        compiler_params=_cp(("parallel", "parallel", "arbitrary")),
        name="nsa_compress_prompt",
    )(blocks, w_cmp)


def _topk_mask(score, blk, n_sel):
    nb = score.shape[0]
    rank = jnp.zeros(score.shape, jnp.int32)
    for mm in range(nb):
        rm = score[mm:mm + 1, :]
        beats = (rm > score) | ((rm == score) & (blk > mm))
        rank = rank + beats.astype(jnp.int32)
    return rank < n_sel


def _compress_pages_step(w_ref, page_refs, o_ref, acc_sc):
    bpp = PAGE_SIZE // CMP_BLOCK
    d = NSA_HEAD_DIM
    nslot = 4 * NSA_KV_HEADS
    nblk = len(page_refs) * bpp
    jc = 8
    acc = None
    for j0 in range(0, CMP_BLOCK, jc):
        cols = []
        for j in range(j0, j0 + jc):
            tiles = [pr[blk * CMP_BLOCK + j] for pr in page_refs for blk in range(bpp)]
            cols.append(jnp.concatenate(tiles, axis=0).astype(BF16))
        part = _mm(jnp.concatenate(cols, axis=1), w_ref[j0:j0 + jc].reshape(jc * d, 2 * d))
        acc = part if acc is None else acc + part
    acc_sc[0] = acc[:, :d]
    acc_sc[1] = acc[:, d:]
    for s in range(2):
        for h in range(NSA_KV_HEADS):
            o_ref[0, s, h] = acc_sc[s, pl.ds(s * NSA_KV_HEADS + h, nblk, stride=nslot), :]


def _nsa_prompt_kernel(pt_ref, q_ref, kc_ref, vc_ref, ks_ref, vs_ref, kw_ref, vw_ref, ng_ref, wcat_ref, *rest,
                       s_len, tq, tk, n_pp, dec_steps, n_steps):
    page_refs = rest[:n_pp]
    o_ref, oc_ref = rest[n_pp:n_pp + 2]
    ksb, vsb, kwb, vwb, bias_ref, cacc_sc = rest[n_pp + 2:]
    kvh = pl.program_id(1)
    qi = pl.program_id(2)
    nb = s_len // CMP_BLOCK
    g = NSA_GROUP
    rows = g * tq
    d = NSA_HEAD_DIM
    scale = d ** -0.5

    @pl.when(qi == 0)
    def _():
        ksb[...] = ks_ref[...].astype(BF16)
        vsb[...] = _ext_ones(vs_ref[...].astype(BF16))
        kwb[...] = kw_ref[...].astype(BF16)
        vwb[...] = _ext_ones(vw_ref[...].astype(BF16))

    q = q_ref[...]
    qs = [q[:, i * LANES:(i + 1) * LANES] for i in range(g)]
    qr = jnp.concatenate(qs, axis=0)
    q0 = qi * tq
    tpos = q0 + lax.broadcasted_iota(jnp.int32, (tq, 1), 0)
    qpos = q0 + (lax.broadcasted_iota(jnp.int32, (rows, 1), 0) & (tq - 1))

    kc = kc_ref[0, 0, 0].astype(BF16)
    vc = vc_ref[0, 0, 0].astype(BF16)

    def cmp_probs(s, end_le_qpos, axis):
        sm = jnp.where(end_le_qpos, s, NEG)
        p = jnp.where(end_le_qpos, jnp.exp(sm - jnp.max(sm, axis=axis, keepdims=True)), 0.0)
        return p / jnp.maximum(jnp.sum(p, axis=axis, keepdims=True), TINY)

    nblk = lax.broadcasted_iota(jnp.int32, (rows, nb), 1)
    pc = cmp_probs(_nt(qr, kc) * scale, ((nblk + 1) * CMP_BLOCK - 1) <= qpos, 1)
    o_cmp = _mm(pc.astype(BF16), vc)

    nblk_t = lax.broadcasted_iota(jnp.int32, (nb, rows), 0)
    qpos_t = q0 + (lax.broadcasted_iota(jnp.int32, (nb, rows), 1) & (tq - 1))
    pc_t = cmp_probs(_nt(kc, qr) * scale, ((nblk_t + 1) * CMP_BLOCK - 1) <= qpos_t, 0)
    p_slc = pc_t[:, 0:tq]
    for i in range(1, g):
        p_slc = p_slc + pc_t[:, i * tq:(i + 1) * tq]
    blk = lax.broadcasted_iota(jnp.int32, (nb, tq), 0)
    tpos_t = q0 + lax.broadcasted_iota(jnp.int32, (nb, tq), 1)
    cur = lax.shift_right_logical(tpos_t, CMP_SHIFT)
    valid = blk * CMP_BLOCK <= tpos_t
    forced = (blk == 0) | (blk == cur) | (blk == cur - 1)
    score = jnp.where(valid, p_slc + jnp.where(forced, FORCE_BONUS, 0.0), NEG)
    sel_t = _topk_mask(score, blk, min(N_SEL, nb)) & valid
    eye = (lax.broadcasted_iota(jnp.int32, (tq, tq), 0) == lax.broadcasted_iota(jnp.int32, (tq, tq), 1))
    sel = _nt(eye.astype(BF16), sel_t.astype(BF16))
    expand = (lax.shift_right_logical(lax.broadcasted_iota(jnp.int32, (nb, s_len), 1), CMP_SHIFT)
              == lax.broadcasted_iota(jnp.int32, (nb, s_len), 0))
    selk = _mm(sel.astype(BF16), expand.astype(BF16))
    kpos_all = lax.broadcasted_iota(jnp.int32, (1, s_len), 1)
    bias_ref[...] = jnp.where((selk > 0.5) & (kpos_all <= tpos), 0.0, NEG)

    def slc_body(kb, carry):
        start = pl.multiple_of(kb * tk, tk)
        k = ksb[pl.ds(start, tk), :]
        v = vsb[pl.ds(start, tk), :]
        bias = bias_ref[:, pl.ds(start, tk)]
        return tuple(_flash_step(carry[i], _nt(qs[i], k) * scale + bias, v) for i in range(g))

    init = tuple(_flash_init(tq, d) for _ in range(g))
    c_s = lax.fori_loop(0, (q0 + tq + tk - 1) // tk, slc_body, init)

    def win_body(kb, carry):
        start = pl.multiple_of(kb * tk, tk)
        k = kwb[pl.ds(start, tk), :]
        v = vwb[pl.ds(start, tk), :]
        dist = tpos - (start + lax.broadcasted_iota(jnp.int32, (1, tk), 1))
        bias = jnp.where((dist >= 0) & (dist <= WINDOW), 0.0, NEG)
        return tuple(_flash_step(carry[i], _nt(qs[i], k) * scale + bias, v) for i in range(g))

    c_w = lax.fori_loop(jnp.maximum(q0 - WINDOW, 0) // tk, (q0 + tq + tk - 1) // tk, win_body, init)

    ng = ng_ref[...]
    for i in range(g):
        def gate(r):
            c0 = r * NSA_HEADS + i
            c1 = c0 + NSA_GROUP
            return jnp.where(kvh == 0, ng[:, c0:c0 + 1], ng[:, c1:c1 + 1])
        o = (gate(0) * o_cmp[i * tq:(i + 1) * tq] + gate(1) * _flash_out(c_s[i], d)
             + gate(2) * _flash_out(c_w[i], d))
        o_ref[:, i * LANES:(i + 1) * LANES] = o.astype(o_ref.dtype)

    compress = functools.partial(_compress_pages_step, wcat_ref, page_refs, oc_ref, cacc_sc)
    if dec_steps == n_steps:
        compress()
    else:
        pl.when(_step_id() < dec_steps)(compress)


def _nsa_attn_call(page_table, nq, kcvc, nkv_cs, nkv_win, ng, w_cat, cache_rows, b, s):
    t = b * s
    tq = 128
    nqb = s // tq
    nb = s // CMP_BLOCK
    gw = NSA_GROUP * LANES
    kvh = NSA_KV_HEADS
    d = NSA_HEAD_DIM
    nslot = 4 * kvh
    bpp = PAGE_SIZE // CMP_BLOCK
    bs, n_pages = page_table.shape
    n_steps = b * kvh * nqb
    n_pp = _pages_per_step(bs, n_pages, n_steps)
    n_pg = n_pages // n_pp
    dec_steps = bs * n_pg
    kcvc5 = kcvc.reshape(2, kvh, b, nb, d)

    def dec(bi, h, qi):
        td = jnp.minimum((bi * kvh + h) * nqb + qi, dec_steps - 1)
        return td // n_pg, td % n_pg

    def page_spec(kk):
        def imap(bi, h, qi, pt):
            r, g = dec(bi, h, qi)
            return pt[r, g * n_pp + kk], 0, 0, 0
        return pl.BlockSpec((None, PAGE_SIZE, nslot, d), imap)

    def summary_map(bi, h, qi, pt):
        r, g = dec(bi, h, qi)
        return r, 0, 0, g, 0

    kv_spec = lambda col: pl.BlockSpec((s, LANES), lambda bi, h, qi, pt: (bi, col + h))
    return pl.pallas_call(
        functools.partial(_nsa_prompt_kernel, s_len=s, tq=tq, tk=_pick(s, (256, 128)), n_pp=n_pp,
                          dec_steps=dec_steps, n_steps=n_steps),
        grid_spec=pltpu.PrefetchScalarGridSpec(
            num_scalar_prefetch=1,
            grid=(b, kvh, nqb),
            in_specs=[
                pl.BlockSpec((tq, gw), lambda bi, h, qi, pt: (bi * nqb + qi, h)),
                pl.BlockSpec((1, 1, 1, nb, d), lambda bi, h, qi, pt: (0, h, bi, 0, 0)),
                pl.BlockSpec((1, 1, 1, nb, d), lambda bi, h, qi, pt: (1, h, bi, 0, 0)),
                kv_spec(4), kv_spec(6),
                kv_spec(0), kv_spec(2),
                pl.BlockSpec((tq, LANES), lambda bi, h, qi, pt: (bi * nqb + qi, 0)),
                pl.BlockSpec(w_cat.shape, lambda bi, h, qi, pt: (0, 0, 0)),
            ] + [page_spec(kk) for kk in range(n_pp)],
            out_specs=[pl.BlockSpec((tq, gw), lambda bi, h, qi, pt: (bi * nqb + qi, h)),
                       pl.BlockSpec((1, 2, kvh, n_pp * bpp, d), summary_map)],
            scratch_shapes=[pltpu.VMEM((s, LANES), BF16), pltpu.VMEM((s, 2 * LANES), BF16),
                            pltpu.VMEM((s, LANES), BF16), pltpu.VMEM((s, 2 * LANES), BF16),
                            pltpu.VMEM((tq, s), F32), pltpu.VMEM((2, n_pp * bpp * nslot, d), F32)],
        ),
        out_shape=[jax.ShapeDtypeStruct((t, NSA_HEADS * d), BF16),
                   jax.ShapeDtypeStruct((bs, 2, kvh, n_pages * bpp, d), F32)],
        compiler_params=_cp(("arbitrary", "arbitrary", "arbitrary")),
        name="nsa_attn",
    )(page_table, nq, kcvc5, kcvc5, nkv_cs, nkv_cs, nkv_win, nkv_win, ng, w_cat, *([cache_rows] * n_pp))


def _merge_kernel(h_ref, d_ref, n_ref, wg0_ref, wg1_ref, wbd_ref, wbn_ref, o_ref):
    h = h_ref[...]
    a = _mm(d_ref[...], wbd_ref[...])
    bb = _mm(n_ref[...], wbn_ref[...])
    g0 = jax.nn.sigmoid(_mm(h, wg0_ref[...]))
    g1 = jax.nn.sigmoid(_mm(h, wg1_ref[...]))
    o_ref[...] = (g0 * a + g1 * bb).astype(o_ref.dtype)


def _merge_call(h, diff, nsa, w_mg, w_bd, w_bn):
    m, d = h.shape
    tm = _pick(m, (1024, 512, 256, 128, 64, 32))
    tn = _pick(d, (256, 128))
    nj = d // tn
    row = lambda kk: pl.BlockSpec((tm, kk), lambda i, j: (i, 0))
    return pl.pallas_call(
        _merge_kernel,
        grid=(m // tm, nj),
        in_specs=[row(d), row(diff.shape[1]), row(nsa.shape[1]),
                  pl.BlockSpec((d, tn), lambda i, j: (0, j)),
                  pl.BlockSpec((d, tn), lambda i, j: (0, nj + j)),
                  pl.BlockSpec((w_bd.shape[0], tn), lambda i, j: (0, j)),
                  pl.BlockSpec((w_bn.shape[0], tn), lambda i, j: (0, j))],
        out_specs=pl.BlockSpec((tm, tn), lambda i, j: (i, j)),
        out_shape=jax.ShapeDtypeStruct((m, d), BF16),
        compiler_params=_cp(("parallel", "arbitrary")),
        name="branch_merge",
    )(h, diff, nsa, w_mg, w_mg, w_bd, w_bn)


def _out_proj_kernel(m_ref, w_ref, x_ref, g_ref, o_ref):
    y = _mm(m_ref[...], w_ref[...])
    o_ref[...] = x_ref[...] + _rms(y, g_ref[...])


def _out_proj_call(mix, w_o, x, g):
    m, d = x.shape
    tm = _pick(m, (512, 256, 128, 64, 32))
    return pl.pallas_call(
        _out_proj_kernel,
        grid=(m // tm,),
        in_specs=[pl.BlockSpec((tm, d), lambda i: (i, 0)), pl.BlockSpec((d, d), lambda i: (0, 0)),
                  pl.BlockSpec((tm, d), lambda i: (i, 0)), pl.BlockSpec((1, d), lambda i: (0, 0))],
        out_specs=pl.BlockSpec((tm, d), lambda i: (i, 0)),
        out_shape=jax.ShapeDtypeStruct((m, d), F32),
        compiler_params=_cp(("parallel",)),
        name="out_proj_norm_residual",
    )(mix, w_o, x, g.reshape(1, d))


HALO = 16


def _gelu_glu(ca, cg):
    return jax.nn.gelu(ca, approximate=True) * cg


def _ffn_prompt_kernel(x_ref, xh_ref, gpre_ref, gpost_ref, wa_ref, wg_ref, cwa_ref, cwg_ref, ba_ref, bg_ref,
                       wd_ref, o_ref, h_sc, ua_sc, ug_sc, acc_sc, *, tm, blocks_per_seq):
    i = pl.program_id(0)
    j = pl.program_id(1)

    @pl.when(j == 0)
    def _():
        h_sc[0:HALO, :] = _rms(xh_ref[...], gpre_ref[...]).astype(BF16)
        h_sc[HALO:, :] = _rms(x_ref[...], gpre_ref[...]).astype(BF16)
        acc_sc[...] = jnp.zeros(acc_sc.shape, F32)

    first = (i % blocks_per_seq) == 0
    h = h_sc[...]
    keep = jnp.logical_not(first & (lax.broadcasted_iota(jnp.int32, (tm + HALO, 1), 0) < HALO))
    ua_sc[...] = jnp.where(keep, _mm(h, wa_ref[...]), 0.0)
    ug_sc[...] = jnp.where(keep, _mm(h, wg_ref[...]), 0.0)

    def conv(u_sc, cw_ref, b_ref):
        cw = cw_ref[...]
        out = b_ref[...]
        for tap in range(CONV_W):
            out = out + u_sc[pl.ds(HALO - (CONV_W - 1) + tap, tm), :] * cw[tap:tap + 1]
        return out

    act = _gelu_glu(conv(ua_sc, cwa_ref, ba_ref), conv(ug_sc, cwg_ref, bg_ref))
    acc_sc[...] += _mm(act.astype(BF16), wd_ref[...])

    @pl.when(j == pl.num_programs(1) - 1)
    def _():
        o_ref[...] = x_ref[...] + _rms(acc_sc[...], gpost_ref[...])


def _ffn_prompt_call(x, g_pre, g_post, wa, wg, cwa, cwg, ba, bg, wd, s):
    m, d = x.shape
    fp = wa.shape[1]
    tm = _pick(s, (512, 256, 128))
    tf = _pick(fp, (512, 256, 128))
    hb = tm // HALO
    col = lambda r: pl.BlockSpec((r, tf), lambda i, j: (0, j))
    return pl.pallas_call(
        functools.partial(_ffn_prompt_kernel, tm=tm, blocks_per_seq=s // tm),
        grid=(m // tm, fp // tf),
        in_specs=[pl.BlockSpec((tm, d), lambda i, j: (i, 0)),
                  pl.BlockSpec((HALO, d), lambda i, j: (jnp.maximum(i * hb - 1, 0), 0)),
                  pl.BlockSpec((1, d), lambda i, j: (0, 0)), pl.BlockSpec((1, d), lambda i, j: (0, 0)),
                  col(d), col(d), col(CONV_W), col(CONV_W), col(1), col(1),
                  pl.BlockSpec((tf, d), lambda i, j: (j, 0))],
        out_specs=pl.BlockSpec((tm, d), lambda i, j: (i, 0)),
        out_shape=jax.ShapeDtypeStruct((m, d), F32),
        scratch_shapes=[pltpu.VMEM((tm + HALO, d), BF16), pltpu.VMEM((tm + HALO, tf), F32),
                        pltpu.VMEM((tm + HALO, tf), F32), pltpu.VMEM((tm, d), F32)],
        compiler_params=_cp(("parallel", "arbitrary")),
        name="conv_ffn_prompt",
    )(x, x, g_pre.reshape(1, d), g_post.reshape(1, d), wa, wg, cwa, cwg, ba, bg, wd)


def _ffn_sample_kernel(ua_ref, ug_ref, bufa_ref, bufg_ref, cwa_ref, cwg_ref, ba_ref, bg_ref, wd_ref, x_ref,
                       gpost_ref, o_ref, acc_sc):
    j = pl.program_id(0)

    @pl.when(j == 0)
    def _():
        acc_sc[...] = jnp.zeros(acc_sc.shape, F32)

    def conv(u_ref, buf_ref, cw_ref, b_ref):
        cw = cw_ref[...]
        out = b_ref[...] + u_ref[...] * cw[CONV_W - 1:CONV_W]
        for tap in range(CONV_W - 1):
            out = out + buf_ref[tap] * cw[tap:tap + 1]
        return out

    act = _gelu_glu(conv(ua_ref, bufa_ref, cwa_ref, ba_ref), conv(ug_ref, bufg_ref, cwg_ref, bg_ref))
    acc_sc[...] += _mm(act.astype(BF16), wd_ref[...])

    @pl.when(j == pl.num_programs(0) - 1)
    def _():
        o_ref[...] = x_ref[...] + _rms(acc_sc[...], gpost_ref[...])


def _ffn_sample_call(u, bufa, bufg, cwa, cwg, ba, bg, wd, x, g_post):
    bsz, d = x.shape
    fp = wd.shape[0]
    tf = _pick(fp, (512, 256, 128))
    nf = fp // tf
    col = lambda r: pl.BlockSpec((r, tf), lambda j: (0, j))
    return pl.pallas_call(
        _ffn_sample_kernel,
        grid=(nf,),
        in_specs=[pl.BlockSpec((bsz, tf), lambda j: (0, j)), pl.BlockSpec((bsz, tf), lambda j: (0, nf + j)),
                  pl.BlockSpec((CONV_W - 1, bsz, tf), lambda j: (0, 0, j)),
                  pl.BlockSpec((CONV_W - 1, bsz, tf), lambda j: (0, 0, j)),
                  col(CONV_W), col(CONV_W), col(1), col(1),
                  pl.BlockSpec((tf, d), lambda j: (j, 0)),
                  pl.BlockSpec((bsz, d), lambda j: (0, 0)), pl.BlockSpec((1, d), lambda j: (0, 0))],
        out_specs=pl.BlockSpec((bsz, d), lambda j: (0, 0)),
        out_shape=jax.ShapeDtypeStruct((bsz, d), F32),
        scratch_shapes=[pltpu.VMEM((bsz, d), F32)],
        compiler_params=_cp(("arbitrary",)),
        name="conv_ffn_sample",
    )(u, u, bufa, bufg, cwa, cwg, ba, bg, wd, x, g_post.reshape(1, d))


def _nsa_sample_select_kernel(q_ref, kc_ref, vc_ref, new_ref, w0_ref, ocmp_ref, idx_ref, *, past_len):
    kvh = pl.program_id(1)
    nbp = kc_ref.shape[3]
    nbt = nbp + 1
    qpos = past_len
    scale = NSA_HEAD_DIM ** -0.5
    qg = q_ref[0, 0].astype(BF16)
    kc = kc_ref[0, 0, 0].astype(BF16)
    vc = vc_ref[0, 0, 0].astype(BF16)
    new = new_ref[0]
    pick = lambda r: jnp.where(kvh == 0, new[r:r + 1], new[r + 1:r + 2])
    kc_new = _mm(jnp.broadcast_to(pick(0), (8, LANES)).astype(BF16), w0_ref[0].astype(BF16))
    vc_new = _mm(jnp.broadcast_to(pick(2), (8, LANES)).astype(BF16), w0_ref[1].astype(BF16))

    s_past = _nt(qg, kc) * scale
    s_new = _nt(qg, kc_new.astype(BF16))[:, 0:1] * scale
    n_past = lax.broadcasted_iota(jnp.int32, (8, nbp), 1)
    mask_past = ((n_past + 1) * CMP_BLOCK - 1) <= qpos
    mask_new = ((nbt * CMP_BLOCK - 1) <= qpos)
    s_past = jnp.where(mask_past, s_past, NEG)
    s_new = s_new if mask_new else jnp.full_like(s_new, NEG)
    m = jnp.maximum(jnp.max(s_past, axis=-1, keepdims=True), s_new)
    p_past = jnp.where(mask_past, jnp.exp(s_past - m), 0.0)
    p_new = jnp.exp(s_new - m) * (1.0 if mask_new else 0.0)
    den = jnp.maximum(jnp.sum(p_past, axis=-1, keepdims=True) + p_new, TINY)
    p_past = p_past / den
    p_new = p_new / den
    o_cmp = _mm(p_past.astype(BF16), vc) + p_new.astype(BF16).astype(F32) * vc_new.astype(BF16).astype(F32)[0:1]
    ocmp_ref[0, 0] = o_cmp

    grp = lax.broadcasted_iota(jnp.int32, (8, 1), 0) < NSA_GROUP
    ps_past = jnp.sum(jnp.where(grp, p_past, 0.0), axis=0, keepdims=True)
    ps_new = jnp.sum(jnp.where(grp, p_new, 0.0), axis=0, keepdims=True)

    width = ((nbt + LANES - 1) // LANES) * LANES
    nrow = ((nbt + 7) // 8) * 8
    n = lax.broadcasted_iota(jnp.int32, (1, width), 1)
    p_all = jnp.concatenate([ps_past, jnp.broadcast_to(ps_new, (1, width - nbp))], axis=1)
    cur = qpos >> CMP_SHIFT
    valid = (n * CMP_BLOCK <= qpos) & (n < nbt)
    forced = (n == 0) | (n == cur) | (n == cur - 1)
    score = jnp.where(valid, p_all + jnp.where(forced, FORCE_BONUS, 0.0), NEG)
    mrow = lax.broadcasted_iota(jnp.int32, (nrow, width), 0)
    ncol = lax.broadcasted_iota(jnp.int32, (nrow, width), 1)
    score_b = jnp.broadcast_to(score, (nrow, width))
    score_col = jnp.sum(jnp.where(mrow == ncol, score_b, 0.0), axis=1, keepdims=True)
    col_ok = lax.broadcasted_iota(jnp.int32, (nrow, 1), 0) < nbt
    beats = col_ok & ((score_col > score_b) | ((score_col == score_b) & (mrow < ncol)))
    rank = jnp.sum(jnp.where(beats, 1.0, 0.0), axis=0, keepdims=True)
    n_sel = min(N_SEL, nbt)
    r = lax.broadcasted_iota(jnp.int32, (N_SEL, width), 0)
    hit = ((jnp.broadcast_to(rank, (N_SEL, width)) == r.astype(F32)) & jnp.broadcast_to(valid, (N_SEL, width))
           & (r < n_sel))
    nf = jnp.broadcast_to(n, (N_SEL, width)).astype(F32)
    idx = jnp.sum(jnp.where(hit, nf, 0.0), axis=1, keepdims=True)
    any_hit = jnp.sum(jnp.where(hit, 1.0, 0.0), axis=1, keepdims=True)
    idx = jnp.where(any_hit > 0.5, idx, -1.0).astype(jnp.int32)
    idx_ref[0, 0] = jnp.broadcast_to(idx, (N_SEL, LANES))


def _nsa_sample_select_call(q8, kcvc_p, new_rows, w0, past_len):
    bsz = q8.shape[0]
    nbp = kcvc_p.shape[3]
    return pl.pallas_call(
        functools.partial(_nsa_sample_select_kernel, past_len=past_len),
        grid=(bsz, NSA_KV_HEADS),
        in_specs=[pl.BlockSpec((1, 1, 8, LANES), lambda b, h: (b, h, 0, 0)),
                  pl.BlockSpec((1, 1, 1, nbp, LANES), lambda b, h: (b, 0, h, 0, 0)),
                  pl.BlockSpec((1, 1, 1, nbp, LANES), lambda b, h: (b, 1, h, 0, 0)),
                  pl.BlockSpec((1, 8, LANES), lambda b, h: (b, 0, 0)),
                  pl.BlockSpec((2, LANES, LANES), lambda b, h: (0, 0, 0))],
        out_specs=[pl.BlockSpec((1, 1, 8, LANES), lambda b, h: (b, h, 0, 0)),
                   pl.BlockSpec((1, 1, N_SEL, LANES), lambda b, h: (b, h, 0, 0))],
        out_shape=[jax.ShapeDtypeStruct((bsz, NSA_KV_HEADS, 8, LANES), F32),
                   jax.ShapeDtypeStruct((bsz, NSA_KV_HEADS, N_SEL, LANES), jnp.int32)],
        compiler_params=_cp(("parallel", "parallel")),
        name="nsa_select_sample",
    )(q8, kcvc_p, kcvc_p, new_rows, w0)


def _nsa_sample_attend_kernel(page_ref, half_ref, flag_ref, q_ref, new_ref, wnew_ref, gate_ref, ocmp_ref, win_ref,
                              *rest, past_len):
    blk_refs = rest[:N_SEL]
    o_ref = rest[N_SEL]
    b = pl.program_id(0)
    kvh = pl.program_id(1)
    nkv = NSA_KV_HEADS
    base = (b * nkv + kvh) * (N_SEL + 1)
    scale = NSA_HEAD_DIM ** -0.5
    qg = q_ref[0, 0].astype(BF16)
    new = new_ref[0]
    wnew = wnew_ref[0]
    pick = lambda arr, r: jnp.where(kvh == 0, arr[r:r + 1], arr[r + 1:r + 2])
    rnd = lambda a: a.astype(BF16).astype(F32)
    qf = qg.astype(F32)

    def attend(x, keep, k_new, v_new, new_on):
        s = jnp.where(keep, _nt(qg, x) * scale, NEG)
        s_n = jnp.sum(qf * rnd(k_new), axis=-1, keepdims=True) * scale
        s_n = jnp.where(new_on, s_n, NEG)
        m = jnp.maximum(jnp.max(s, axis=-1, keepdims=True), s_n)
        p = jnp.where(keep, jnp.exp(s - m), 0.0)
        p_n = jnp.where(new_on, jnp.exp(s_n - m), 0.0)
        den = jnp.maximum(jnp.sum(p, axis=-1, keepdims=True) + p_n, TINY)
        p_v = pltpu.roll(p, nkv, 1)
        return (_mm(p_v.astype(BF16), x) + rnd(p_n) * rnd(v_new)) / den

    nslot = 4 * nkv
    brow = CMP_BLOCK * nslot
    x_sel = jnp.concatenate([r[...].reshape(brow, LANES).astype(BF16) for r in blk_refs], axis=0)
    lane = lax.broadcasted_iota(jnp.int32, (1, N_SEL * brow), 1)
    chunk = lax.shift_right_logical(lane, brow.bit_length() - 1)
    live = jnp.zeros((1, N_SEL * brow), jnp.int32)
    for r in range(N_SEL):
        live = jnp.where(chunk == r, flag_ref[base + r], live)
    keep = (live > 0) & ((lane & (nslot - 1)) == 2 * nkv + kvh)
    o_slc = attend(x_sel, jnp.broadcast_to(keep, (8, N_SEL * brow)), pick(new, 4), pick(new, 6),
                   flag_ref[base + N_SEL] > 0)

    wrows = win_ref.shape[1]
    wb = wrows // (2 * nkv)
    wl = lax.broadcasted_iota(jnp.int32, (8, wrows), 1)
    kpos = past_len - wb + lax.shift_right_logical(wl, (2 * nkv).bit_length() - 1)
    dist = past_len - kpos
    wkeep = (dist >= 0) & (dist <= WINDOW) & (kpos >= 0) & ((wl & (2 * nkv - 1)) == kvh)
    o_win = attend(win_ref[0].astype(BF16), wkeep, pick(wnew, 0), pick(wnew, 2), True)

    gate = gate_ref[0, 0]
    o_ref[0, 0] = (gate[:, 0:LANES] * ocmp_ref[0, 0] + gate[:, LANES:2 * LANES] * o_slc
                   + gate[:, 2 * LANES:3 * LANES] * o_win)


def _nsa_sample_attend_call(pages, halves, flags, q8, new_rows, win_new, gates, o_cmp, win_rows, cache_rows,
                            past_len):
    bsz = q8.shape[0]
    kvh = NSA_KV_HEADS
    nslot = 4 * kvh
    sel = lambda ref, b, h, r: ref[(b * kvh + h) * (N_SEL + 1) + r]
    blk_spec = lambda r: pl.BlockSpec((None, CMP_BLOCK, nslot, LANES),
                                      lambda b, h, pg, hf, fl: (sel(pg, b, h, r), sel(hf, b, h, r), 0, 0))
    return pl.pallas_call(
        functools.partial(_nsa_sample_attend_kernel, past_len=past_len),
        grid_spec=pltpu.PrefetchScalarGridSpec(
            num_scalar_prefetch=3,
            grid=(bsz, kvh),
            in_specs=[pl.BlockSpec((1, 1, 8, LANES), lambda b, h, pg, hf, fl: (b, h, 0, 0)),
                      pl.BlockSpec((1, 8, LANES), lambda b, h, pg, hf, fl: (b, 0, 0)),
                      pl.BlockSpec((1, 4, LANES), lambda b, h, pg, hf, fl: (b, 0, 0)),
                      pl.BlockSpec((1, 1, 8, 3 * LANES), lambda b, h, pg, hf, fl: (b, h, 0, 0)),
                      pl.BlockSpec((1, 1, 8, LANES), lambda b, h, pg, hf, fl: (b, h, 0, 0)),
                      pl.BlockSpec((1, win_rows.shape[1], LANES), lambda b, h, pg, hf, fl: (b, 0, 0))]
                     + [blk_spec(r) for r in range(N_SEL)],
            out_specs=pl.BlockSpec((1, 1, 8, LANES), lambda b, h, pg, hf, fl: (b, h, 0, 0)),
        ),
        out_shape=jax.ShapeDtypeStruct((bsz, kvh, 8, LANES), F32),
        compiler_params=_cp(("parallel", "parallel")),
        name="nsa_attend_sample",
    )(pages, halves, flags, q8, new_rows, win_new, gates, o_cmp, win_rows, *([cache_rows] * N_SEL))


def _pad_cols(a, n):
    return jnp.pad(a, [(0, 0)] * (a.ndim - 1) + [(0, n - a.shape[-1])])


def kernel(x_prompt, x_sample, cache_diff_kv, cache_nsa_kv, state_nsa_win, state_ffn_conv, page_table,
           norm_mix_pre, norm_mix_post, w_in, diff_lambda, diff_subln, nsa_w_cmp, w_branch_diff, w_branch_nsa,
           w_out, norm_ffn_pre, norm_ffn_post, ffn_w_up, ffn_conv_w, ffn_conv_b, ffn_w_down):
    b, s, d = x_prompt.shape
    bs, ts, _ = x_sample.shape
    depth = w_in.shape[0]
    assert depth == 1 and ts == 1
    n_pages = page_table.shape[1]
    past_len = n_pages * PAGE_SIZE
    f = ffn_w_down.shape[1]
    t = b * s
    kvh, hd = NSA_KV_HEADS, NSA_HEAD_DIM
    layer = 0
    lam_init = 0.8 - 0.6 * math.exp(-0.3 * layer)

    wq = DIFF_HEADS * 2 * DIFF_QK_DIM
    wv = DIFF_HEADS * DIFF_V_DIM
    wnq = NSA_HEADS * hd
    wkv = 2 * kvh * hd
    o_dq, o_dk, o_dv = 0, wq, 2 * wq
    o_nq = o_dv + wv
    o_cs = o_nq + wnq
    o_win = o_cs + 2 * wkv
    o_ng = o_win + wkv
    o_mg = o_ng + 3 * NSA_HEADS
    w = w_in[layer]
    w_dq = w[:, o_dq:o_dk].astype(BF16)
    w_dkv = w[:, o_dk:o_nq].astype(BF16)
    w_nq = w[:, o_nq:o_cs].astype(BF16)
    w_cs = w[:, o_cs:o_win].astype(BF16)
    w_win = w[:, o_win:o_ng].astype(BF16)
    w_ng = _pad_cols(w[:, o_ng:o_mg], LANES).astype(BF16)
    w_mg = w[:, o_mg:].astype(BF16)
    w_bd = w_branch_diff[layer].astype(BF16)
    w_bn = w_branch_nsa[layer].astype(BF16)
    w_o = w_out[layer].astype(BF16)
    fp = ((f + 511) // 512) * 512
    w_up = ffn_w_up[layer]
    wa = _pad_cols(w_up[:, :f], fp).astype(BF16)
    wg = _pad_cols(w_up[:, f:], fp).astype(BF16)
    cw = ffn_conv_w[layer]
    cwa, cwg = _pad_cols(cw[:, :f], fp), _pad_cols(cw[:, f:], fp)
    cb = ffn_conv_b[layer].reshape(1, 2 * f)
    ba, bg = _pad_cols(cb[:, :f], fp), _pad_cols(cb[:, f:], fp)
    wd = jnp.pad(ffn_w_down[layer], ((0, fp - f), (0, 0))).astype(BF16)
    w_cmp = nsa_w_cmp[layer]

    lam = _lam_call(diff_lambda[layer], lam_init)

    def projections(h, pos, pos_rows, attn_dtype):
        t64 = _rope_tables(pos, DIFF_QK_DIM)
        t128 = _rope_tables(pos, hd)
        r64, r128, nn, sg = MODE_ROPE64, MODE_ROPE128, MODE_NONE, MODE_SIGMOID
        dq = _proj_call(h, w_dq, [r64] * (wq // 512), 512, attn_dtype, t64, pos_rows, "proj_diff_q")
        dkv = _proj_call(h, w_dkv, [r64] * (wq // 512) + [nn] * (wv // 512), 512, F32, t64, pos_rows, "proj_diff_kv")
        nq = _proj_call(h, w_nq, [r128] * (wnq // 512), 512, attn_dtype, t128, pos_rows, "proj_nsa_q")
        cs = _proj_call(h, w_cs, [r128, nn, r128, nn], wkv // 2, F32, t128, pos_rows, "proj_nsa_kv")
        win = _proj_call(h, w_win, [r128, nn], wkv // 2, F32, t128, pos_rows, "proj_nsa_win")
        ng = _proj_call(h, w_ng, [sg], LANES, F32, (), None, "proj_nsa_gate")
        return dq, dkv, nq, cs, win, ng

    xp = x_prompt.reshape(t, d)
    hp = _norm_call(xp, norm_mix_pre[layer])
    pos_p = jnp.arange(s, dtype=jnp.int32)
    dq, dkv, nq, cs, win, ng = projections(hp, pos_p, s, BF16)

    xs = x_sample.reshape(bs, d)
    hs = _norm_call(xs, norm_mix_pre[layer])
    pos_s = jnp.full((bs,), past_len, jnp.int32)
    dq_s, dkv_s, nq_s, cs_s, win_s, ng_s = projections(hs, pos_s, bs, F32)

    n_pool = cache_nsa_kv.shape[1]
    cache_d = cache_diff_kv.reshape(cache_diff_kv.shape[1], PAGE_SIZE, 2 * DIFF_HEADS, LANES)
    cache_n = cache_nsa_kv.reshape(n_pool, PAGE_SIZE, 4 * kvh, hd)
    w_cat = jnp.concatenate([w_cmp[0], w_cmp[1]], axis=-1).astype(BF16)

    diff, diff_s = _diff_attn_call(lam, page_table, dq, dkv, diff_subln[layer], dq_s.reshape(bs, DIFF_HEADS, LANES),
                                   dkv_s.reshape(bs, 2 * DIFF_HEADS, LANES), cache_d, b, s, lam_init)
    nb = s // CMP_BLOCK
    blocks = cs.reshape(b, nb, CMP_BLOCK, 4, kvh, hd)[:, :, :, :2]
    blocks = blocks.transpose(3, 4, 0, 1, 2, 5).reshape(2, kvh, b * nb, CMP_BLOCK * hd)
    kcvc = _compress_call(blocks, w_cmp.reshape(2, CMP_BLOCK * hd, hd))
    nsa, kcvc_p = _nsa_attn_call(page_table, nq, kcvc, cs, win, ng, w_cat, cache_n, b, s)
    mix = _merge_call(hp, diff, nsa, w_mg, w_bd, w_bn)
    xp1 = _out_proj_call(mix, w_o, xp, norm_mix_post[layer])
    xp2 = _ffn_prompt_call(xp1, norm_ffn_pre[layer], norm_ffn_post[layer], wa, wg, cwa, cwg, ba, bg, wd, s)

    wbp = min(WINDOW, s)
    new_diff_kv_prompt = dkv.reshape(1, b, s, 2, DIFF_HEADS, 2 * DIFF_QK_DIM)
    new_nsa_kv_prompt = cs.reshape(1, b, s, 4, kvh, hd)
    new_win_prompt = win.reshape(b, s, 2, kvh, hd)[None, :, s - wbp:]

    diff_s = diff_s.reshape(bs, DIFF_HEADS * LANES).astype(BF16)
    q8 = jnp.pad(nq_s.reshape(bs, kvh, NSA_GROUP, hd), ((0, 0), (0, 0), (0, 8 - NSA_GROUP), (0, 0)))
    new_rows = cs_s.reshape(bs, 4 * kvh, hd)
    o_cmp_s, idx_s = _nsa_sample_select_call(q8, kcvc_p, new_rows, w_cmp[:, 0], past_len)
    idx = idx_s[..., 0]
    nbp = past_len // CMP_BLOCK
    bpp = PAGE_SIZE // CMP_BLOCK
    is_past = (idx >= 0) & (idx < nbp)
    safe = jnp.clip(idx, 0, nbp - 1)
    page = jnp.take_along_axis(page_table, (safe // bpp).reshape(bs, -1), axis=1).reshape(idx.shape)
    new_sel = jnp.any(idx == nbp, axis=-1, keepdims=True)
    pad1 = lambda a: jnp.concatenate([a, jnp.zeros_like(a[..., :1])], axis=-1).reshape(-1).astype(jnp.int32)
    pages_sel, halves_sel = pad1(page), pad1(safe % bpp)
    flags = jnp.concatenate([is_past, new_sel], axis=-1).reshape(-1).astype(jnp.int32)
    gates = ng_s[:, :3 * NSA_HEADS].reshape(bs, 3, kvh, NSA_GROUP).transpose(0, 2, 3, 1)
    gates = jnp.pad(gates, ((0, 0), (0, 0), (0, 8 - NSA_GROUP), (0, 0)))
    gates = jnp.broadcast_to(gates[..., None], (bs, kvh, 8, 3, LANES)).reshape(bs, kvh, 8, 3 * LANES)
    wbs = state_nsa_win.shape[2]
    win_rows = state_nsa_win.reshape(bs, wbs * 2 * kvh, hd)
    nsa_s = _nsa_sample_attend_call(pages_sel, halves_sel, flags, q8, new_rows, win_s.reshape(bs, 2 * kvh, hd),
                                    gates, o_cmp_s, win_rows, cache_n, past_len)
    nsa_s = nsa_s[:, :, :NSA_GROUP].reshape(bs, NSA_HEADS * hd).astype(BF16)

    mix_s = _merge_call(hs, diff_s, nsa_s, w_mg, w_bd, w_bn)
    xs1 = _out_proj_call(mix_s, w_o, xs, norm_mix_post[layer])

    tail = xp1.reshape(b, s, d)[:, s - (CONV_W - 1):].reshape(b * (CONV_W - 1), d)
    rows = jnp.concatenate([xs1, tail], axis=0)
    pad_r = (-rows.shape[0]) % 16
    rows = jnp.pad(rows, ((0, pad_r), (0, 0)))
    h_rows = _norm_call(rows, norm_ffn_pre[layer])
    w_up_p = jnp.concatenate([wa, wg], axis=1)
    u_rows = _proj_call(h_rows, w_up_p, [MODE_NONE] * (2 * fp // 512), 512, F32, (), None, "proj_ffn_up_rows")
    u_unpad = jnp.concatenate([u_rows[:, :f], u_rows[:, fp:fp + f]], axis=1)
    buf = state_ffn_conv[layer]
    bufa = _pad_cols(buf[..., :f], fp).transpose(1, 0, 2)
    bufg = _pad_cols(buf[..., f:], fp).transpose(1, 0, 2)
    xs2 = _ffn_sample_call(u_rows, bufa, bufg, cwa, cwg, ba, bg, wd, xs1, norm_ffn_post[layer])

    new_conv_prompt = u_unpad[bs:bs + b * (CONV_W - 1)].reshape(1, b, CONV_W - 1, 2 * f)
    new_conv_sample = jnp.concatenate([buf[:, 1:], u_unpad[:bs, None]], axis=1)[None]
    new_diff_kv_sample = dkv_s.reshape(1, bs, 1, 2, DIFF_HEADS, 2 * DIFF_QK_DIM)
    new_nsa_kv_sample = cs_s.reshape(1, bs, 1, 4, kvh, hd)
    new_win_sample = jnp.concatenate([state_nsa_win[layer][:, 1:], win_s.reshape(bs, 1, 2, kvh, hd)], axis=1)[None]

    return (xp2.reshape(b, s, d), xs2.reshape(bs, 1, d), new_diff_kv_prompt, new_diff_kv_sample,
            new_nsa_kv_prompt, new_nsa_kv_sample, new_win_prompt, new_win_sample, new_conv_prompt, new_conv_sample)
```

```python
import functools
import math

import jax
import jax.numpy as jnp
from jax import lax
from jax.experimental import pallas as pl
from jax.experimental.pallas import tpu as pltpu

F32 = jnp.float32
BF16 = jnp.bfloat16

DIFF_HEADS = 8
DIFF_QK_DIM = 64
DIFF_V_DIM = 128
NSA_HEADS = 8
NSA_KV_HEADS = 2
NSA_GROUP = 4
NSA_HEAD_DIM = 128
CMP_BLOCK = 64
CMP_SHIFT = 6
N_SEL = 16
WINDOW = 512
PAGE_SIZE = 128
CONV_W = 3
FORCE_BONUS = 1e4
ROPE_THETA = 10000.0
EPS = 1e-6
NEG = -1e30
TINY = 1e-30
LANES = 128
VMEM_LIMIT = 52 * 1024 * 1024

_NT = (((1,), (1,)), ((), ()))


def _nt(a, b):
    return lax.dot_general(a, b, _NT, preferred_element_type=F32)


def _mm(a, b):
    return jnp.dot(a, b, preferred_element_type=F32)


def _cp(sem):
    return pltpu.CompilerParams(dimension_semantics=sem, vmem_limit_bytes=VMEM_LIMIT)


def _pick(n, cands):
    for c in cands:
        if n % c == 0:
            return c
    return n


def _rms(x, g):
    return x * lax.rsqrt(jnp.mean(x * x, axis=-1, keepdims=True) + EPS) * g


def _lam_kernel(l_ref, o_ref, *, lam_init):
    l = l_ref[...]
    a = jnp.sum(l[0:1] * l[1:2], axis=-1, keepdims=True)
    b = jnp.sum(l[2:3] * l[3:4], axis=-1, keepdims=True)
    o_ref[...] = jnp.broadcast_to(jnp.exp(a) - jnp.exp(b) + lam_init, o_ref.shape)


def _lam_call(lam_params, lam_init):
    out = pl.pallas_call(
        functools.partial(_lam_kernel, lam_init=lam_init),
        out_shape=jax.ShapeDtypeStruct((8, LANES), F32),
        name="diff_lambda",
    )(lam_params)
    return out[0, :1]


def _norm_kernel(x_ref, g_ref, o_ref):
    o_ref[...] = _rms(x_ref[...], g_ref[...]).astype(o_ref.dtype)


def _norm_call(x, g):
    m, d = x.shape
    tm = _pick(m, (512, 256, 128, 64, 32, 16))
    return pl.pallas_call(
        _norm_kernel,
        grid=(m // tm,),
        in_specs=[pl.BlockSpec((tm, d), lambda i: (i, 0)), pl.BlockSpec((1, d), lambda i: (0, 0))],
        out_specs=pl.BlockSpec((tm, d), lambda i: (i, 0)),
        out_shape=jax.ShapeDtypeStruct((m, d), BF16),
        compiler_params=_cp(("parallel",)),
        name="rmsnorm",
    )(x, g.reshape(1, d))


MODE_NONE, MODE_ROPE64, MODE_ROPE128, MODE_SIGMOID = 0, 1, 2, 3


def _rope_tables(pos, d):
    half = d // 2
    inv = 1.0 / (ROPE_THETA ** (jnp.arange(0, d, 2, dtype=F32) / d))
    ang = pos.astype(F32)[:, None] * inv[None, :]
    lane = jnp.arange(LANES)
    cos = jnp.cos(ang)[:, lane % half]
    sin = jnp.sin(ang)[:, lane % half]
    first = ((lane % d) < half)[None, :]
    return cos, jnp.where(first, -sin, 0.0), jnp.where(first, 0.0, sin)


def _apply_rope(z, cos, sa, sb, half):
    outs = []
    for c in range(z.shape[1] // LANES):
        blk = z[:, c * LANES:(c + 1) * LANES]
        outs.append(blk * cos + pltpu.roll(blk, LANES - half, 1) * sa + pltpu.roll(blk, half, 1) * sb)
    return outs[0] if len(outs) == 1 else jnp.concatenate(outs, axis=1)


def _proj_kernel(*refs, runs, n_tab):
    h_ref, w_ref = refs[0], refs[1]
    tabs = refs[2:2 + n_tab]
    o_ref = refs[2 + n_tab]
    j = pl.program_id(1)
    acc = _mm(h_ref[...], w_ref[...])

    def emit(mode):
        if mode == MODE_NONE:
            o_ref[...] = acc.astype(o_ref.dtype)
        elif mode == MODE_SIGMOID:
            o_ref[...] = jax.nn.sigmoid(acc).astype(o_ref.dtype)
        else:
            half = 32 if mode == MODE_ROPE64 else 64
            o_ref[...] = _apply_rope(acc, tabs[0][...], tabs[1][...], tabs[2][...], half).astype(o_ref.dtype)

    if len(runs) == 1:
        emit(runs[0][0])
    else:
        for mode, j0, j1 in runs:
            pl.when((j >= j0) & (j < j1))(functools.partial(emit, mode))


def _proj_call(h, w, modes, tn, out_dtype, tables=(), pos_rows=None, name="proj"):
    m, k = h.shape
    n = w.shape[1]
    assert n % tn == 0 and len(modes) == n // tn
    tm = _pick(m, (1024, 512, 256, 128, 64, 48, 32, 16))
    runs = []
    for jj, md in enumerate(modes):
        if runs and runs[-1][0] == md:
            runs[-1][2] = jj + 1
        else:
            runs.append([md, jj, jj + 1])
    runs = tuple(tuple(r) for r in runs)
    in_specs = [pl.BlockSpec((tm, k), lambda i, j: (i, 0)), pl.BlockSpec((k, tn), lambda i, j: (0, j))]
    if tables:
        nblk = pos_rows // tm
        assert nblk * tm == pos_rows
        in_specs += [pl.BlockSpec((tm, LANES), lambda i, j: (i % nblk, 0)) for _ in tables]
    return pl.pallas_call(
        functools.partial(_proj_kernel, runs=runs, n_tab=len(tables)),
        grid=(m // tm, n // tn),
        in_specs=in_specs,
        out_specs=pl.BlockSpec((tm, tn), lambda i, j: (i, j)),
        out_shape=jax.ShapeDtypeStruct((m, n), out_dtype),
        compiler_params=_cp(("parallel", "arbitrary")),
        name=name,
    )(h, w, *tables)


def _flash_step(carry, s, v_ext):
    m, acc = carry
    m_new = jnp.maximum(m, jnp.max(s, axis=-1, keepdims=True))
    p = jnp.exp(s - m_new)
    return m_new, acc * jnp.exp(m - m_new) + _mm(p.astype(BF16), v_ext)


def _flash_init(rows, d):
    return jnp.full((rows, 1), NEG, F32), jnp.zeros((rows, 2 * d), F32)


def _flash_out(carry, d):
    acc = carry[1]
    return acc[:, :d] / jnp.maximum(acc[:, d:d + 1], TINY)


def _ext_ones(v):
    return jnp.concatenate([v, jnp.ones(v.shape, v.dtype)], axis=1)


def _step_id():
    t = pl.program_id(0)
    for ax in range(1, 3):
        t = t * pl.num_programs(ax) + pl.program_id(ax)
    return t


def _pages_per_step(n_req, n_pages, n_steps):
    for n_pp in range(1, n_pages + 1):
        if n_pages % n_pp == 0 and n_req * (n_pages // n_pp) <= n_steps:
            return n_pp
    raise ValueError("prompt grid too small to carry the sample group's page stream")


def _diff_decode_step(pg, n_pg, lam_ref, q_ref, kvn_ref, g_ref, k_refs, v_refs, o_ref, q_sc, m_sc, l_sc, acc_sc,
                      out_scale):
    nh = DIFF_HEADS
    nr = 2 * nh
    prow = PAGE_SIZE * nh

    @pl.when(pg == 0)
    def _():
        q8 = q_ref[0]
        lane = lax.broadcasted_iota(jnp.int32, (nh, LANES), 1)
        scale = DIFF_QK_DIM ** -0.5
        q_sc[0:nh] = (jnp.where(lane < DIFF_QK_DIM, q8, 0.0) * scale).astype(BF16).astype(F32)
        q_sc[nh:nr] = (jnp.where(lane >= DIFF_QK_DIM, q8, 0.0) * scale).astype(BF16).astype(F32)
        m_sc[...] = jnp.full(m_sc.shape, NEG, F32)
        l_sc[...] = jnp.zeros(l_sc.shape, F32)
        acc_sc[...] = jnp.zeros(acc_sc.shape, F32)

    def own_head(width):
        lane = lax.broadcasted_iota(jnp.int32, (nr, width), 1)
        row = lax.broadcasted_iota(jnp.int32, (nr, width), 0)
        return (lane & (nh - 1)) == (row & (nh - 1))

    def update(s, keep, pv):
        sm = jnp.where(keep, s, NEG)
        m_old = m_sc[...]
        m_new = jnp.maximum(m_old, jnp.max(sm, axis=-1, keepdims=True))
        corr = jnp.exp(m_old - m_new)
        pe = jnp.where(keep, jnp.exp(sm - m_new[:, 0:1]), 0.0)
        m_sc[...] = m_new
        l_sc[...] = l_sc[...] * corr + jnp.sum(pe, axis=-1, keepdims=True)
        acc_sc[...] = acc_sc[...] * corr + pv(pe.astype(BF16))

    qb = q_sc[...].astype(BF16)
    s = jnp.concatenate([_nt(qb, kr[...].reshape(prow, LANES).astype(BF16)) for kr in k_refs], axis=1)

    def pv_pages(pb):
        out = None
        for i, vr in enumerate(v_refs):
            part = _mm(pb[:, i * prow:(i + 1) * prow], vr[...].reshape(prow, LANES).astype(BF16))
            out = part if out is None else out + part
        return out

    update(s, own_head(len(k_refs) * prow), pv_pages)

    @pl.when(pg == n_pg - 1)
    def _():
        kvn = kvn_ref[0]
        vk = jnp.concatenate([kvn[nh:nr], kvn[0:nh]], axis=0).astype(BF16)
        is_key = lax.broadcasted_iota(jnp.int32, (nr, nr), 1) < nh
        update(_nt(qb, kvn.astype(BF16)), own_head(nr) & is_key, lambda pb: _mm(pb, vk))
        o = acc_sc[...] / l_sc[...]
        od = o[0:nh] - lam_ref[0] * o[nh:nr]
        o_ref[0] = _rms(od, g_ref[...]) * out_scale


def _diff_attn_kernel(lam_ref, pt_ref, q_ref, k_ref, v_ref, g_ref, qd_ref, kvn_ref, *rest, tq, tk, out_scale, n_pp,
                      n_pg, dec_steps, n_steps):
    k_pages = rest[:n_pp]
    v_pages = rest[n_pp:2 * n_pp]
    o_ref, od_ref = rest[2 * n_pp:2 * n_pp + 2]
    kb_ref, vb_ref, q_sc, m_sc, l_sc, acc_sc = rest[2 * n_pp + 2:]
    qi = pl.program_id(2)
    dv = DIFF_V_DIM

    @pl.when(qi == 0)
    def _():
        kb_ref[...] = k_ref[...].astype(BF16)
        vb_ref[...] = _ext_ones(v_ref[...].astype(BF16))

    q = q_ref[...]
    lane = lax.broadcasted_iota(jnp.int32, q.shape, 1)
    scale = jnp.asarray(DIFF_QK_DIM ** -0.5, q.dtype)
    q1 = jnp.where(lane < DIFF_QK_DIM, q, 0) * scale
    q2 = jnp.where(lane >= DIFF_QK_DIM, q, 0) * scale

    def block(kb, carry, bias):
        start = pl.multiple_of(kb * tk, tk)
        k = kb_ref[pl.ds(start, tk), :]
        v = vb_ref[pl.ds(start, tk), :]
        s1, s2 = _nt(q1, k), _nt(q2, k)
        if bias is not None:
            s1, s2 = s1 + bias, s2 + bias
        return _flash_step(carry[0], s1, v), _flash_step(carry[1], s2, v)

    init = _flash_init(tq, dv)
    nd = tq // tk
    carry = lax.fori_loop(0, qi * nd, lambda kb, c: block(kb, c, None), (init, init))
    for j in range(nd):
        causal = ((j * tk + lax.broadcasted_iota(jnp.int32, (tq, tk), 1))
                  <= lax.broadcasted_iota(jnp.int32, (tq, tk), 0))
        carry = block(qi * nd + j, carry, jnp.where(causal, 0.0, NEG))
    c1, c2 = carry
    o = _flash_out(c1, dv) - lam_ref[0] * _flash_out(c2, dv)
    o_ref[...] = (_rms(o, g_ref[...]) * out_scale).astype(o_ref.dtype)

    t = _step_id()
    decode = functools.partial(_diff_decode_step, lax.rem(t, n_pg), n_pg, lam_ref, qd_ref, kvn_ref, g_ref, k_pages,
                               v_pages, od_ref, q_sc, m_sc, l_sc, acc_sc, out_scale)
    if dec_steps == n_steps:
        decode()
    else:
        pl.when(t < dec_steps)(decode)


def _diff_attn_call(lam, page_table, q, kv, subln, q_dec, kv_new, cache, b, s, lam_init):
    t = b * s
    tq = _pick(s, (512, 256, 128))
    tk = min(tq, 256)
    nq = s // tq
    nh = DIFF_HEADS
    bs, n_pages = page_table.shape
    n_steps = b * nh * nq
    n_pp = _pages_per_step(bs, n_pages, n_steps)
    n_pg = n_pages // n_pp
    dec_steps = bs * n_pg

    def dec(bi, h, qi):
        td = jnp.minimum((bi * nh + h) * nq + qi, dec_steps - 1)
        return td // n_pg, td % n_pg

    def page_spec(kk, half):
        def imap(bi, h, qi, lam_, pt):
            r, g = dec(bi, h, qi)
            return pt[r, g * n_pp + kk], 0, half, 0
        return pl.BlockSpec((None, PAGE_SIZE, nh, LANES), imap)

    req_spec = lambda rows: pl.BlockSpec((1, rows, LANES), lambda bi, h, qi, lam_, pt: (dec(bi, h, qi)[0], 0, 0))
    st = pltpu.VMEM((2 * nh, LANES), F32)
    return pl.pallas_call(
        functools.partial(_diff_attn_kernel, tq=tq, tk=tk, out_scale=1.0 - lam_init, n_pp=n_pp, n_pg=n_pg,
                          dec_steps=dec_steps, n_steps=n_steps),
        grid_spec=pltpu.PrefetchScalarGridSpec(
            num_scalar_prefetch=2,
            grid=(b, nh, nq),
            in_specs=[
                pl.BlockSpec((tq, LANES), lambda bi, h, qi, lam_, pt: (bi * nq + qi, h)),
                pl.BlockSpec((s, LANES), lambda bi, h, qi, lam_, pt: (bi, h)),
                pl.BlockSpec((s, LANES), lambda bi, h, qi, lam_, pt: (bi, nh + h)),
                pl.BlockSpec((1, LANES), lambda bi, h, qi, lam_, pt: (0, 0)),
                req_spec(nh), req_spec(2 * nh),
            ] + [page_spec(kk, 0) for kk in range(n_pp)] + [page_spec(kk, 1) for kk in range(n_pp)],
            out_specs=[pl.BlockSpec((tq, LANES), lambda bi, h, qi, lam_, pt: (bi * nq + qi, h)), req_spec(nh)],
            scratch_shapes=[pltpu.VMEM((s, LANES), BF16), pltpu.VMEM((s, 2 * LANES), BF16), st, st, st, st],
        ),
        out_shape=[jax.ShapeDtypeStruct((t, nh * DIFF_V_DIM), BF16), jax.ShapeDtypeStruct((bs, nh, LANES), F32)],
        compiler_params=_cp(("arbitrary", "arbitrary", "arbitrary")),
        name="diff_attn",
    )(lam, page_table, q, kv, kv, subln.reshape(1, LANES), q_dec, kv_new, *([cache] * (2 * n_pp)))


def _compress_kernel(x_ref, w_ref, o_ref):
    d = NSA_HEAD_DIM
    nblk = x_ref.shape[0] // CMP_BLOCK
    jc = 8
    acc = None
    for j0 in range(0, CMP_BLOCK, jc):
        cols = [x_ref[pl.ds(j, nblk, stride=CMP_BLOCK), :].astype(BF16) for j in range(j0, j0 + jc)]
        part = _mm(jnp.concatenate(cols, axis=1), w_ref[0, j0:j0 + jc].reshape(jc * d, d).astype(BF16))
        acc = part if acc is None else acc + part
    o_ref[0, 0] = acc


def _compress_call(cs, w_cmp):
    t = cs.shape[0]
    kvh, d = NSA_KV_HEADS, NSA_HEAD_DIM
    return pl.pallas_call(
        _compress_kernel,
        grid=(2, kvh),
        in_specs=[pl.BlockSpec((t, d), lambda s, h: (0, s * kvh + h)),
                  pl.BlockSpec((1, CMP_BLOCK, d, d), lambda s, h: (s, 0, 0, 0))],
        out_specs=pl.BlockSpec((1, 1, t // CMP_BLOCK, d), lambda s, h: (s, h, 0, 0)),
        out_shape=jax.ShapeDtypeStruct((2, kvh, t // CMP_BLOCK, d), F32),
        compiler_params=_cp(("parallel", "parallel")),
        name="nsa_compress_prompt",
    )(cs, w_cmp)


def _topk_mask(score, blk, n_sel):
    nb = score.shape[0]
    rank = jnp.zeros(score.shape, jnp.int32)
    for mm in range(nb):
        rm = score[mm:mm + 1, :]
        beats = (rm > score) | ((rm == score) & (blk > mm))
        rank = rank + beats.astype(jnp.int32)
    return rank < n_sel


def _compress_pages_step(w_ref, page_refs, o_ref, acc_sc):
    bpp = PAGE_SIZE // CMP_BLOCK
    d = NSA_HEAD_DIM
    nslot = 4 * NSA_KV_HEADS
    nblk = len(page_refs) * bpp
    jc = 8
    acc = None
    for j0 in range(0, CMP_BLOCK, jc):
        cols = []
        for j in range(j0, j0 + jc):
            tiles = [pr[blk * CMP_BLOCK + j] for pr in page_refs for blk in range(bpp)]
            cols.append(jnp.concatenate(tiles, axis=0).astype(BF16))
        part = _mm(jnp.concatenate(cols, axis=1), w_ref[j0:j0 + jc].reshape(jc * d, 2 * d))
        acc = part if acc is None else acc + part
    acc_sc[0] = acc[:, :d]
    acc_sc[1] = acc[:, d:]
    for s in range(2):
        for h in range(NSA_KV_HEADS):
            o_ref[0, s, h] = acc_sc[s, pl.ds(s * NSA_KV_HEADS + h, nblk, stride=nslot), :]


def _nsa_prompt_kernel(pt_ref, q_ref, kc_ref, vc_ref, ks_ref, vs_ref, kw_ref, vw_ref, ng_ref, wcat_ref, *rest,
                       s_len, tq, tk, n_pp, dec_steps, n_steps):
    page_refs = rest[:n_pp]
    o_ref, oc_ref = rest[n_pp:n_pp + 2]
    ksb, vsb, kwb, vwb, bias_ref, cacc_sc = rest[n_pp + 2:]
    kvh = pl.program_id(1)
    qi = pl.program_id(2)
    nb = s_len // CMP_BLOCK
    g = NSA_GROUP
    rows = g * tq
    d = NSA_HEAD_DIM
    scale = d ** -0.5

    @pl.when(qi == 0)
    def _():
        ksb[...] = ks_ref[...].astype(BF16)
        vsb[...] = _ext_ones(vs_ref[...].astype(BF16))
        kwb[...] = kw_ref[...].astype(BF16)
        vwb[...] = _ext_ones(vw_ref[...].astype(BF16))

    q = q_ref[...]
    qs = [q[:, i * LANES:(i + 1) * LANES] for i in range(g)]
    qr = jnp.concatenate(qs, axis=0)
    q0 = qi * tq
    tpos = q0 + lax.broadcasted_iota(jnp.int32, (tq, 1), 0)
    qpos = q0 + (lax.broadcasted_iota(jnp.int32, (rows, 1), 0) & (tq - 1))

    kc = kc_ref[0, 0, 0].astype(BF16)
    vc = vc_ref[0, 0, 0].astype(BF16)

    def cmp_probs(s, end_le_qpos, axis):
        sm = jnp.where(end_le_qpos, s, NEG)
        p = jnp.where(end_le_qpos, jnp.exp(sm - jnp.max(sm, axis=axis, keepdims=True)), 0.0)
        return p / jnp.maximum(jnp.sum(p, axis=axis, keepdims=True), TINY)

    nblk = lax.broadcasted_iota(jnp.int32, (rows, nb), 1)
    pc = cmp_probs(_nt(qr, kc) * scale, ((nblk + 1) * CMP_BLOCK - 1) <= qpos, 1)
    o_cmp = _mm(pc.astype(BF16), vc)

    nblk_t = lax.broadcasted_iota(jnp.int32, (nb, rows), 0)
    qpos_t = q0 + (lax.broadcasted_iota(jnp.int32, (nb, rows), 1) & (tq - 1))
    pc_t = cmp_probs(_nt(kc, qr) * scale, ((nblk_t + 1) * CMP_BLOCK - 1) <= qpos_t, 0)
    p_slc = pc_t[:, 0:tq]
    for i in range(1, g):
        p_slc = p_slc + pc_t[:, i * tq:(i + 1) * tq]
    blk = lax.broadcasted_iota(jnp.int32, (nb, tq), 0)
    tpos_t = q0 + lax.broadcasted_iota(jnp.int32, (nb, tq), 1)
    cur = lax.shift_right_logical(tpos_t, CMP_SHIFT)
    valid = blk * CMP_BLOCK <= tpos_t
    forced = (blk == 0) | (blk == cur) | (blk == cur - 1)
    score = jnp.where(valid, p_slc + jnp.where(forced, FORCE_BONUS, 0.0), NEG)
    sel_t = _topk_mask(score, blk, min(N_SEL, nb)) & valid
    eye = (lax.broadcasted_iota(jnp.int32, (tq, tq), 0) == lax.broadcasted_iota(jnp.int32, (tq, tq), 1))
    sel = _nt(eye.astype(BF16), sel_t.astype(BF16))
    expand = (lax.shift_right_logical(lax.broadcasted_iota(jnp.int32, (nb, s_len), 1), CMP_SHIFT)
              == lax.broadcasted_iota(jnp.int32, (nb, s_len), 0))
    selk = _mm(sel.astype(BF16), expand.astype(BF16))
    kpos_all = lax.broadcasted_iota(jnp.int32, (1, s_len), 1)
    bias_ref[...] = jnp.where((selk > 0.5) & (kpos_all <= tpos), 0.0, NEG)

    def slc_step(kb, carry):
        start = pl.multiple_of(kb * tk, tk)
        k = ksb[pl.ds(start, tk), :]
        v = vsb[pl.ds(start, tk), :]
        bias = bias_ref[:, pl.ds(start, tk)]
        return tuple(_flash_step(carry[i], _nt(qs[i], k) * scale + bias, v) for i in range(g))

    def win_step(kb, carry):
        start = pl.multiple_of(kb * tk, tk)
        k = kwb[pl.ds(start, tk), :]
        v = vwb[pl.ds(start, tk), :]
        dist = tpos - (start + lax.broadcasted_iota(jnp.int32, (1, tk), 1))
        bias = jnp.where((dist >= 0) & (dist <= WINDOW), 0.0, NEG)
        return tuple(_flash_step(carry[i], _nt(qs[i], k) * scale + bias, v) for i in range(g))

    init = tuple(_flash_init(tq, d) for _ in range(g))
    lo = jnp.maximum(q0 - WINDOW, 0) // tk
    hi = (q0 + tq + tk - 1) // tk
    c_s = lax.fori_loop(0, lo, slc_step, init)
    c_s, c_w = lax.fori_loop(lo, hi, lambda kb, c: (slc_step(kb, c[0]), win_step(kb, c[1])), (c_s, init))

    ng = ng_ref[...]
    for i in range(g):
        def gate(r):
            c0 = r * NSA_HEADS + i
            c1 = c0 + NSA_GROUP
            return jnp.where(kvh == 0, ng[:, c0:c0 + 1], ng[:, c1:c1 + 1])
        o = (gate(0) * o_cmp[i * tq:(i + 1) * tq] + gate(1) * _flash_out(c_s[i], d)
             + gate(2) * _flash_out(c_w[i], d))
        o_ref[:, i * LANES:(i + 1) * LANES] = o.astype(o_ref.dtype)

    compress = functools.partial(_compress_pages_step, wcat_ref, page_refs, oc_ref, cacc_sc)
    if dec_steps == n_steps:
        compress()
    else:
        pl.when(_step_id() < dec_steps)(compress)


def _nsa_attn_call(page_table, nq, kcvc, nkv_cs, nkv_win, ng, w_cat, cache_rows, b, s):
    t = b * s
    tq = 128
    nqb = s // tq
    nb = s // CMP_BLOCK
    gw = NSA_GROUP * LANES
    kvh = NSA_KV_HEADS
    d = NSA_HEAD_DIM
    nslot = 4 * kvh
    bpp = PAGE_SIZE // CMP_BLOCK
    bs, n_pages = page_table.shape
    n_steps = b * kvh * nqb
    n_pp = _pages_per_step(bs, n_pages, n_steps)
    n_pg = n_pages // n_pp
    dec_steps = bs * n_pg
    kcvc5 = kcvc.reshape(2, kvh, b, nb, d)

    def dec(bi, h, qi):
        td = jnp.minimum((bi * kvh + h) * nqb + qi, dec_steps - 1)
        return td // n_pg, td % n_pg

    def page_spec(kk):
        def imap(bi, h, qi, pt):
            r, g = dec(bi, h, qi)
            return pt[r, g * n_pp + kk], 0, 0, 0
        return pl.BlockSpec((None, PAGE_SIZE, nslot, d), imap)

    def summary_map(bi, h, qi, pt):
        r, g = dec(bi, h, qi)
        return r, 0, 0, g, 0

    kv_spec = lambda col: pl.BlockSpec((s, LANES), lambda bi, h, qi, pt: (bi, col + h))
    return pl.pallas_call(
        functools.partial(_nsa_prompt_kernel, s_len=s, tq=tq, tk=_pick(s, (256, 128)), n_pp=n_pp,
                          dec_steps=dec_steps, n_steps=n_steps),
        grid_spec=pltpu.PrefetchScalarGridSpec(
            num_scalar_prefetch=1,
            grid=(b, kvh, nqb),
            in_specs=[
                pl.BlockSpec((tq, gw), lambda bi, h, qi, pt: (bi * nqb + qi, h)),
                pl.BlockSpec((1, 1, 1, nb, d), lambda bi, h, qi, pt: (0, h, bi, 0, 0)),
                pl.BlockSpec((1, 1, 1, nb, d), lambda bi, h, qi, pt: (1, h, bi, 0, 0)),
                kv_spec(4), kv_spec(6),
                kv_spec(0), kv_spec(2),
                pl.BlockSpec((tq, LANES), lambda bi, h, qi, pt: (bi * nqb + qi, 0)),
                pl.BlockSpec(w_cat.shape, lambda bi, h, qi, pt: (0, 0, 0)),
            ] + [page_spec(kk) for kk in range(n_pp)],
            out_specs=[pl.BlockSpec((tq, gw), lambda bi, h, qi, pt: (bi * nqb + qi, h)),
                       pl.BlockSpec((1, 2, kvh, n_pp * bpp, d), summary_map)],
            scratch_shapes=[pltpu.VMEM((s, LANES), BF16), pltpu.VMEM((s, 2 * LANES), BF16),
                            pltpu.VMEM((s, LANES), BF16), pltpu.VMEM((s, 2 * LANES), BF16),
                            pltpu.VMEM((tq, s), F32), pltpu.VMEM((2, n_pp * bpp * nslot, d), F32)],
        ),
        out_shape=[jax.ShapeDtypeStruct((t, NSA_HEADS * d), BF16),
                   jax.ShapeDtypeStruct((bs, 2, kvh, n_pages * bpp, d), F32)],
        compiler_params=_cp(("arbitrary", "arbitrary", "arbitrary")),
        name="nsa_attn",
    )(page_table, nq, kcvc5, kcvc5, nkv_cs, nkv_cs, nkv_win, nkv_win, ng, w_cat, *([cache_rows] * n_pp))


def _merge_kernel(h_ref, d_ref, n_ref, wg0_ref, wg1_ref, wbd_ref, wbn_ref, o_ref):
    h = h_ref[...]
    a = _mm(d_ref[...], wbd_ref[...])
    bb = _mm(n_ref[...], wbn_ref[...])
    g0 = jax.nn.sigmoid(_mm(h, wg0_ref[...]))
    g1 = jax.nn.sigmoid(_mm(h, wg1_ref[...]))
    o_ref[...] = (g0 * a + g1 * bb).astype(o_ref.dtype)


def _merge_call(h, diff, nsa, w_mg, w_bd, w_bn):
    m, d = h.shape
    tm = _pick(m, (1024, 512, 256, 128, 64, 32))
    tn = _pick(d, (256, 128))
    nj = d // tn
    row = lambda kk: pl.BlockSpec((tm, kk), lambda i, j: (i, 0))
    return pl.pallas_call(
        _merge_kernel,
        grid=(m // tm, nj),
        in_specs=[row(d), row(diff.shape[1]), row(nsa.shape[1]),
                  pl.BlockSpec((d, tn), lambda i, j: (0, j)),
                  pl.BlockSpec((d, tn), lambda i, j: (0, nj + j)),
                  pl.BlockSpec((w_bd.shape[0], tn), lambda i, j: (0, j)),
                  pl.BlockSpec((w_bn.shape[0], tn), lambda i, j: (0, j))],
        out_specs=pl.BlockSpec((tm, tn), lambda i, j: (i, j)),
        out_shape=jax.ShapeDtypeStruct((m, d), BF16),
        compiler_params=_cp(("parallel", "arbitrary")),
        name="branch_merge",
    )(h, diff, nsa, w_mg, w_mg, w_bd, w_bn)


def _out_proj_kernel(m_ref, w_ref, x_ref, g_ref, o_ref):
    y = _mm(m_ref[...], w_ref[...])
    o_ref[...] = x_ref[...] + _rms(y, g_ref[...])


def _out_proj_call(mix, w_o, x, g):
    m, d = x.shape
    tm = _pick(m, (512, 256, 128, 64, 32))
    return pl.pallas_call(
        _out_proj_kernel,
        grid=(m // tm,),
        in_specs=[pl.BlockSpec((tm, d), lambda i: (i, 0)), pl.BlockSpec((d, d), lambda i: (0, 0)),
                  pl.BlockSpec((tm, d), lambda i: (i, 0)), pl.BlockSpec((1, d), lambda i: (0, 0))],
        out_specs=pl.BlockSpec((tm, d), lambda i: (i, 0)),
        out_shape=jax.ShapeDtypeStruct((m, d), F32),
        compiler_params=_cp(("parallel",)),
        name="out_proj_norm_residual",
    )(mix, w_o, x, g.reshape(1, d))


HALO = 16


def _gelu_glu(ca, cg):
    return jax.nn.gelu(ca, approximate=True) * cg


def _ffn_prompt_kernel(x_ref, xh_ref, gpre_ref, gpost_ref, wa_ref, wg_ref, cwa_ref, cwg_ref, ba_ref, bg_ref,
                       wd_ref, o_ref, h_sc, ua_sc, ug_sc, acc_sc, *, tm, blocks_per_seq):
    i = pl.program_id(0)
    j = pl.program_id(1)

    @pl.when(j == 0)
    def _():
        h_sc[0:HALO, :] = _rms(xh_ref[...], gpre_ref[...]).astype(BF16)
        h_sc[HALO:, :] = _rms(x_ref[...], gpre_ref[...]).astype(BF16)
        acc_sc[...] = jnp.zeros(acc_sc.shape, F32)

    first = (i % blocks_per_seq) == 0
    h = h_sc[...]
    keep = jnp.logical_not(first & (lax.broadcasted_iota(jnp.int32, (tm + HALO, 1), 0) < HALO))
    ua_sc[...] = jnp.where(keep, _mm(h, wa_ref[...]), 0.0)
    ug_sc[...] = jnp.where(keep, _mm(h, wg_ref[...]), 0.0)

    def conv(u_sc, cw_ref, b_ref):
        cw = cw_ref[...]
        out = b_ref[...]
        for tap in range(CONV_W):
            out = out + u_sc[pl.ds(HALO - (CONV_W - 1) + tap, tm), :] * cw[tap:tap + 1]
        return out

    act = _gelu_glu(conv(ua_sc, cwa_ref, ba_ref), conv(ug_sc, cwg_ref, bg_ref))
    acc_sc[...] += _mm(act.astype(BF16), wd_ref[...])

    @pl.when(j == pl.num_programs(1) - 1)
    def _():
        o_ref[...] = x_ref[...] + _rms(acc_sc[...], gpost_ref[...])


def _ffn_prompt_call(x, g_pre, g_post, wa, wg, cwa, cwg, ba, bg, wd, s):
    m, d = x.shape
    fp = wa.shape[1]
    tm = _pick(s, (512, 256, 128))
    tf = _pick(fp, (512, 256, 128))
    hb = tm // HALO
    col = lambda r: pl.BlockSpec((r, tf), lambda i, j: (0, j))
    return pl.pallas_call(
        functools.partial(_ffn_prompt_kernel, tm=tm, blocks_per_seq=s // tm),
        grid=(m // tm, fp // tf),
        in_specs=[pl.BlockSpec((tm, d), lambda i, j: (i, 0)),
                  pl.BlockSpec((HALO, d), lambda i, j: (jnp.maximum(i * hb - 1, 0), 0)),
                  pl.BlockSpec((1, d), lambda i, j: (0, 0)), pl.BlockSpec((1, d), lambda i, j: (0, 0)),
                  col(d), col(d), col(CONV_W), col(CONV_W), col(1), col(1),
                  pl.BlockSpec((tf, d), lambda i, j: (j, 0))],
        out_specs=pl.BlockSpec((tm, d), lambda i, j: (i, 0)),
        out_shape=jax.ShapeDtypeStruct((m, d), F32),
        scratch_shapes=[pltpu.VMEM((tm + HALO, d), BF16), pltpu.VMEM((tm + HALO, tf), F32),
                        pltpu.VMEM((tm + HALO, tf), F32), pltpu.VMEM((tm, d), F32)],
        compiler_params=_cp(("parallel", "arbitrary")),
        name="conv_ffn_prompt",
    )(x, x, g_pre.reshape(1, d), g_post.reshape(1, d), wa, wg, cwa, cwg, ba, bg, wd)


def _ffn_sample_kernel(ua_ref, ug_ref, bufa_ref, bufg_ref, cwa_ref, cwg_ref, ba_ref, bg_ref, wd_ref, x_ref,
                       gpost_ref, o_ref, acc_sc):
    j = pl.program_id(0)

    @pl.when(j == 0)
    def _():
        acc_sc[...] = jnp.zeros(acc_sc.shape, F32)

    def conv(u_ref, buf_ref, cw_ref, b_ref):
        cw = cw_ref[...]
        out = b_ref[...] + u_ref[...] * cw[CONV_W - 1:CONV_W]
        for tap in range(CONV_W - 1):
            out = out + buf_ref[tap] * cw[tap:tap + 1]
        return out

    act = _gelu_glu(conv(ua_ref, bufa_ref, cwa_ref, ba_ref), conv(ug_ref, bufg_ref, cwg_ref, bg_ref))
    acc_sc[...] += _mm(act.astype(BF16), wd_ref[...])

    @pl.when(j == pl.num_programs(0) - 1)
    def _():
        o_ref[...] = x_ref[...] + _rms(acc_sc[...], gpost_ref[...])


def _ffn_sample_call(u, bufa, bufg, cwa, cwg, ba, bg, wd, x, g_post):
    bsz, d = x.shape
    fp = wd.shape[0]
    tf = _pick(fp, (512, 256, 128))
    nf = fp // tf
    col = lambda r: pl.BlockSpec((r, tf), lambda j: (0, j))
    return pl.pallas_call(
        _ffn_sample_kernel,
        grid=(nf,),
        in_specs=[pl.BlockSpec((bsz, tf), lambda j: (0, j)), pl.BlockSpec((bsz, tf), lambda j: (0, nf + j)),
                  pl.BlockSpec((CONV_W - 1, bsz, tf), lambda j: (0, 0, j)),
                  pl.BlockSpec((CONV_W - 1, bsz, tf), lambda j: (0, 0, j)),
                  col(CONV_W), col(CONV_W), col(1), col(1),
                  pl.BlockSpec((tf, d), lambda j: (j, 0)),
                  pl.BlockSpec((bsz, d), lambda j: (0, 0)), pl.BlockSpec((1, d), lambda j: (0, 0))],
        out_specs=pl.BlockSpec((bsz, d), lambda j: (0, 0)),
        out_shape=jax.ShapeDtypeStruct((bsz, d), F32),
        scratch_shapes=[pltpu.VMEM((bsz, d), F32)],
        compiler_params=_cp(("arbitrary",)),
        name="conv_ffn_sample",
    )(u, u, bufa, bufg, cwa, cwg, ba, bg, wd, x, g_post.reshape(1, d))


def _nsa_sample_select_kernel(q_ref, kc_ref, vc_ref, new_ref, w0_ref, ocmp_ref, idx_ref, *, past_len):
    kvh = pl.program_id(1)
    nbp = kc_ref.shape[3]
    nbt = nbp + 1
    qpos = past_len
    scale = NSA_HEAD_DIM ** -0.5
    qg = q_ref[0, 0].astype(BF16)
    kc = kc_ref[0, 0, 0].astype(BF16)
    vc = vc_ref[0, 0, 0].astype(BF16)
    new = new_ref[0]
    pick = lambda r: jnp.where(kvh == 0, new[r:r + 1], new[r + 1:r + 2])
    kc_new = _mm(jnp.broadcast_to(pick(0), (8, LANES)).astype(BF16), w0_ref[0].astype(BF16))
    vc_new = _mm(jnp.broadcast_to(pick(2), (8, LANES)).astype(BF16), w0_ref[1].astype(BF16))

    s_past = _nt(qg, kc) * scale
    s_new = _nt(qg, kc_new.astype(BF16))[:, 0:1] * scale
    n_past = lax.broadcasted_iota(jnp.int32, (8, nbp), 1)
    mask_past = ((n_past + 1) * CMP_BLOCK - 1) <= qpos
    mask_new = ((nbt * CMP_BLOCK - 1) <= qpos)
    s_past = jnp.where(mask_past, s_past, NEG)
    s_new = s_new if mask_new else jnp.full_like(s_new, NEG)
    m = jnp.maximum(jnp.max(s_past, axis=-1, keepdims=True), s_new)
    p_past = jnp.where(mask_past, jnp.exp(s_past - m), 0.0)
    p_new = jnp.exp(s_new - m) * (1.0 if mask_new else 0.0)
    den = jnp.maximum(jnp.sum(p_past, axis=-1, keepdims=True) + p_new, TINY)
    p_past = p_past / den
    p_new = p_new / den
    o_cmp = _mm(p_past.astype(BF16), vc) + p_new.astype(BF16).astype(F32) * vc_new.astype(BF16).astype(F32)[0:1]
    ocmp_ref[0, 0] = o_cmp

    grp = lax.broadcasted_iota(jnp.int32, (8, 1), 0) < NSA_GROUP
    ps_past = jnp.sum(jnp.where(grp, p_past, 0.0), axis=0, keepdims=True)
    ps_new = jnp.sum(jnp.where(grp, p_new, 0.0), axis=0, keepdims=True)

    width = ((nbt + LANES - 1) // LANES) * LANES
    nrow = ((nbt + 7) // 8) * 8
    n = lax.broadcasted_iota(jnp.int32, (1, width), 1)
    p_all = jnp.concatenate([ps_past, jnp.broadcast_to(ps_new, (1, width - nbp))], axis=1)
    cur = qpos >> CMP_SHIFT
    valid = (n * CMP_BLOCK <= qpos) & (n < nbt)
    forced = (n == 0) | (n == cur) | (n == cur - 1)
    score = jnp.where(valid, p_all + jnp.where(forced, FORCE_BONUS, 0.0), NEG)
    mrow = lax.broadcasted_iota(jnp.int32, (nrow, width), 0)
    ncol = lax.broadcasted_iota(jnp.int32, (nrow, width), 1)
    score_b = jnp.broadcast_to(score, (nrow, width))
    score_col = jnp.sum(jnp.where(mrow == ncol, score_b, 0.0), axis=1, keepdims=True)
    col_ok = lax.broadcasted_iota(jnp.int32, (nrow, 1), 0) < nbt
    beats = col_ok & ((score_col > score_b) | ((score_col == score_b) & (mrow < ncol)))
    rank = jnp.sum(jnp.where(beats, 1.0, 0.0), axis=0, keepdims=True)
    n_sel = min(N_SEL, nbt)
    r = lax.broadcasted_iota(jnp.int32, (N_SEL, width), 0)
    hit = ((jnp.broadcast_to(rank, (N_SEL, width)) == r.astype(F32)) & jnp.broadcast_to(valid, (N_SEL, width))
           & (r < n_sel))
    nf = jnp.broadcast_to(n, (N_SEL, width)).astype(F32)
    idx = jnp.sum(jnp.where(hit, nf, 0.0), axis=1, keepdims=True)
    any_hit = jnp.sum(jnp.where(hit, 1.0, 0.0), axis=1, keepdims=True)
    idx = jnp.where(any_hit > 0.5, idx, -1.0).astype(jnp.int32)
    idx_ref[0, 0] = jnp.broadcast_to(idx, (N_SEL, LANES))


def _nsa_sample_select_call(q8, kcvc_p, new_rows, w0, past_len):
    bsz = q8.shape[0]
    nbp = kcvc_p.shape[3]
    return pl.pallas_call(
        functools.partial(_nsa_sample_select_kernel, past_len=past_len),
        grid=(bsz, NSA_KV_HEADS),
        in_specs=[pl.BlockSpec((1, 1, 8, LANES), lambda b, h: (b, h, 0, 0)),
                  pl.BlockSpec((1, 1, 1, nbp, LANES), lambda b, h: (b, 0, h, 0, 0)),
                  pl.BlockSpec((1, 1, 1, nbp, LANES), lambda b, h: (b, 1, h, 0, 0)),
                  pl.BlockSpec((1, 8, LANES), lambda b, h: (b, 0, 0)),
                  pl.BlockSpec((2, LANES, LANES), lambda b, h: (0, 0, 0))],
        out_specs=[pl.BlockSpec((1, 1, 8, LANES), lambda b, h: (b, h, 0, 0)),
                   pl.BlockSpec((1, 1, N_SEL, LANES), lambda b, h: (b, h, 0, 0))],
        out_shape=[jax.ShapeDtypeStruct((bsz, NSA_KV_HEADS, 8, LANES), F32),
                   jax.ShapeDtypeStruct((bsz, NSA_KV_HEADS, N_SEL, LANES), jnp.int32)],
        compiler_params=_cp(("parallel", "parallel")),
        name="nsa_select_sample",
    )(q8, kcvc_p, kcvc_p, new_rows, w0)


def _nsa_sample_attend_kernel(page_ref, half_ref, flag_ref, q_ref, new_ref, wnew_ref, gate_ref, ocmp_ref, win_ref,
                              *rest, past_len):
    blk_refs = rest[:N_SEL]
    o_ref = rest[N_SEL]
    b = pl.program_id(0)
    kvh = pl.program_id(1)
    nkv = NSA_KV_HEADS
    base = (b * nkv + kvh) * (N_SEL + 1)
    scale = NSA_HEAD_DIM ** -0.5
    qg = q_ref[0, 0].astype(BF16)
    new = new_ref[0]
    wnew = wnew_ref[0]
    pick = lambda arr, r: jnp.where(kvh == 0, arr[r:r + 1], arr[r + 1:r + 2])
    rnd = lambda a: a.astype(BF16).astype(F32)
    qf = qg.astype(F32)

    def attend(x, keep, k_new, v_new, new_on):
        s = jnp.where(keep, _nt(qg, x) * scale, NEG)
        s_n = jnp.sum(qf * rnd(k_new), axis=-1, keepdims=True) * scale
        s_n = jnp.where(new_on, s_n, NEG)
        m = jnp.maximum(jnp.max(s, axis=-1, keepdims=True), s_n)
        p = jnp.where(keep, jnp.exp(s - m), 0.0)
        p_n = jnp.where(new_on, jnp.exp(s_n - m), 0.0)
        den = jnp.maximum(jnp.sum(p, axis=-1, keepdims=True) + p_n, TINY)
        p_v = pltpu.roll(p, nkv, 1)
        return (_mm(p_v.astype(BF16), x) + rnd(p_n) * rnd(v_new)) / den

    nslot = 4 * nkv
    brow = CMP_BLOCK * nslot
    x_sel = jnp.concatenate([r[...].reshape(brow, LANES).astype(BF16) for r in blk_refs], axis=0)
    lane = lax.broadcasted_iota(jnp.int32, (1, N_SEL * brow), 1)
    chunk = lax.shift_right_logical(lane, brow.bit_length() - 1)
    live = jnp.zeros((1, N_SEL * brow), jnp.int32)
    for r in range(N_SEL):
        live = jnp.where(chunk == r, flag_ref[base + r], live)
    keep = (live > 0) & ((lane & (nslot - 1)) == 2 * nkv + kvh)
    o_slc = attend(x_sel, jnp.broadcast_to(keep, (8, N_SEL * brow)), pick(new, 4), pick(new, 6),
                   flag_ref[base + N_SEL] > 0)

    wrows = win_ref.shape[1]
    wb = wrows // (2 * nkv)
    wl = lax.broadcasted_iota(jnp.int32, (8, wrows), 1)
    kpos = past_len - wb + lax.shift_right_logical(wl, (2 * nkv).bit_length() - 1)
    dist = past_len - kpos
    wkeep = (dist >= 0) & (dist <= WINDOW) & (kpos >= 0) & ((wl & (2 * nkv - 1)) == kvh)
    o_win = attend(win_ref[0].astype(BF16), wkeep, pick(wnew, 0), pick(wnew, 2), True)

    gate = gate_ref[0, 0]
    o_ref[0, 0] = (gate[:, 0:LANES] * ocmp_ref[0, 0] + gate[:, LANES:2 * LANES] * o_slc
                   + gate[:, 2 * LANES:3 * LANES] * o_win)


def _nsa_sample_attend_call(pages, halves, flags, q8, new_rows, win_new, gates, o_cmp, win_rows, cache_rows,
                            past_len):
    bsz = q8.shape[0]
    kvh = NSA_KV_HEADS
    nslot = 4 * kvh
    sel = lambda ref, b, h, r: ref[(b * kvh + h) * (N_SEL + 1) + r]
    blk_spec = lambda r: pl.BlockSpec((None, CMP_BLOCK, nslot, LANES),
                                      lambda b, h, pg, hf, fl: (sel(pg, b, h, r), sel(hf, b, h, r), 0, 0))
    return pl.pallas_call(
        functools.partial(_nsa_sample_attend_kernel, past_len=past_len),
        grid_spec=pltpu.PrefetchScalarGridSpec(
            num_scalar_prefetch=3,
            grid=(bsz, kvh),
            in_specs=[pl.BlockSpec((1, 1, 8, LANES), lambda b, h, pg, hf, fl: (b, h, 0, 0)),
                      pl.BlockSpec((1, 8, LANES), lambda b, h, pg, hf, fl: (b, 0, 0)),
                      pl.BlockSpec((1, 4, LANES), lambda b, h, pg, hf, fl: (b, 0, 0)),
                      pl.BlockSpec((1, 1, 8, 3 * LANES), lambda b, h, pg, hf, fl: (b, h, 0, 0)),
                      pl.BlockSpec((1, 1, 8, LANES), lambda b, h, pg, hf, fl: (b, h, 0, 0)),
                      pl.BlockSpec((1, win_rows.shape[1], LANES), lambda b, h, pg, hf, fl: (b, 0, 0))]
                     + [blk_spec(r) for r in range(N_SEL)],
            out_specs=pl.BlockSpec((1, 1, 8, LANES), lambda b, h, pg, hf, fl: (b, h, 0, 0)),
        ),
        out_shape=jax.ShapeDtypeStruct((bsz, kvh, 8, LANES), F32),
        compiler_params=_cp(("parallel", "parallel")),
        name="nsa_attend_sample",
    )(pages, halves, flags, q8, new_rows, win_new, gates, o_cmp, win_rows, *([cache_rows] * N_SEL))


def _pad_cols(a, n):
    return jnp.pad(a, [(0, 0)] * (a.ndim - 1) + [(0, n - a.shape[-1])])


def kernel(x_prompt, x_sample, cache_diff_kv, cache_nsa_kv, state_nsa_win, state_ffn_conv, page_table,
           norm_mix_pre, norm_mix_post, w_in, diff_lambda, diff_subln, nsa_w_cmp, w_branch_diff, w_branch_nsa,
           w_out, norm_ffn_pre, norm_ffn_post, ffn_w_up, ffn_conv_w, ffn_conv_b, ffn_w_down):
    b, s, d = x_prompt.shape
    bs, ts, _ = x_sample.shape
    depth = w_in.shape[0]
    assert depth == 1 and ts == 1
    n_pages = page_table.shape[1]
    past_len = n_pages * PAGE_SIZE
    f = ffn_w_down.shape[1]
    t = b * s
    kvh, hd = NSA_KV_HEADS, NSA_HEAD_DIM
    layer = 0
    lam_init = 0.8 - 0.6 * math.exp(-0.3 * layer)

    wq = DIFF_HEADS * 2 * DIFF_QK_DIM
    wv = DIFF_HEADS * DIFF_V_DIM
    wnq = NSA_HEADS * hd
    wkv = 2 * kvh * hd
    o_dq, o_dk, o_dv = 0, wq, 2 * wq
    o_nq = o_dv + wv
    o_cs = o_nq + wnq
    o_win = o_cs + 2 * wkv
    o_ng = o_win + wkv
    o_mg = o_ng + 3 * NSA_HEADS
    w = w_in[layer]
    w_dq = w[:, o_dq:o_dk].astype(BF16)
    w_dkv = w[:, o_dk:o_nq].astype(BF16)
    w_nq = w[:, o_nq:o_cs].astype(BF16)
    w_cs = w[:, o_cs:o_win].astype(BF16)
    w_win = w[:, o_win:o_ng].astype(BF16)
    w_ng = _pad_cols(w[:, o_ng:o_mg], LANES).astype(BF16)
    w_mg = w[:, o_mg:].astype(BF16)
    w_bd = w_branch_diff[layer].astype(BF16)
    w_bn = w_branch_nsa[layer].astype(BF16)
    w_o = w_out[layer].astype(BF16)
    fp = ((f + 511) // 512) * 512
    w_up = ffn_w_up[layer]
    wa = _pad_cols(w_up[:, :f], fp).astype(BF16)
    wg = _pad_cols(w_up[:, f:], fp).astype(BF16)
    cw = ffn_conv_w[layer]
    cwa, cwg = _pad_cols(cw[:, :f], fp), _pad_cols(cw[:, f:], fp)
    cb = ffn_conv_b[layer].reshape(1, 2 * f)
    ba, bg = _pad_cols(cb[:, :f], fp), _pad_cols(cb[:, f:], fp)
    wd = jnp.pad(ffn_w_down[layer], ((0, fp - f), (0, 0))).astype(BF16)
    w_cmp = nsa_w_cmp[layer]

    lam = _lam_call(diff_lambda[layer], lam_init)

    def projections(h, pos, pos_rows, attn_dtype):
        t64 = _rope_tables(pos, DIFF_QK_DIM)
        t128 = _rope_tables(pos, hd)
        r64, r128, nn, sg = MODE_ROPE64, MODE_ROPE128, MODE_NONE, MODE_SIGMOID
        dq = _proj_call(h, w_dq, [r64] * (wq // 512), 512, attn_dtype, t64, pos_rows, "proj_diff_q")
        dkv = _proj_call(h, w_dkv, [r64] * (wq // 512) + [nn] * (wv // 512), 512, F32, t64, pos_rows, "proj_diff_kv")
        nq = _proj_call(h, w_nq, [r128] * (wnq // 512), 512, attn_dtype, t128, pos_rows, "proj_nsa_q")
        cs = _proj_call(h, w_cs, [r128, nn, r128, nn], wkv // 2, F32, t128, pos_rows, "proj_nsa_kv")
        win = _proj_call(h, w_win, [r128, nn], wkv // 2, F32, t128, pos_rows, "proj_nsa_win")
        ng = _proj_call(h, w_ng, [sg], LANES, F32, (), None, "proj_nsa_gate")
        return dq, dkv, nq, cs, win, ng

    xp = x_prompt.reshape(t, d)
    hp = _norm_call(xp, norm_mix_pre[layer])
    pos_p = jnp.arange(s, dtype=jnp.int32)
    dq, dkv, nq, cs, win, ng = projections(hp, pos_p, s, BF16)

    xs = x_sample.reshape(bs, d)
    hs = _norm_call(xs, norm_mix_pre[layer])
    pos_s = jnp.full((bs,), past_len, jnp.int32)
    dq_s, dkv_s, nq_s, cs_s, win_s, ng_s = projections(hs, pos_s, bs, F32)

    n_pool = cache_nsa_kv.shape[1]
    cache_d = cache_diff_kv.reshape(cache_diff_kv.shape[1], PAGE_SIZE, 2 * DIFF_HEADS, LANES)
    cache_n = cache_nsa_kv.reshape(n_pool, PAGE_SIZE, 4 * kvh, hd)
    w_cat = jnp.concatenate([w_cmp[0], w_cmp[1]], axis=-1).astype(BF16)

    diff, diff_s = _diff_attn_call(lam, page_table, dq, dkv, diff_subln[layer], dq_s.reshape(bs, DIFF_HEADS, LANES),
                                   dkv_s.reshape(bs, 2 * DIFF_HEADS, LANES), cache_d, b, s, lam_init)
    kcvc = _compress_call(cs, w_cmp)
    nsa, kcvc_p = _nsa_attn_call(page_table, nq, kcvc, cs, win, ng, w_cat, cache_n, b, s)
    mix = _merge_call(hp, diff, nsa, w_mg, w_bd, w_bn)
    xp1 = _out_proj_call(mix, w_o, xp, norm_mix_post[layer])
    xp2 = _ffn_prompt_call(xp1, norm_ffn_pre[layer], norm_ffn_post[layer], wa, wg, cwa, cwg, ba, bg, wd, s)

    wbp = min(WINDOW, s)
    new_diff_kv_prompt = dkv.reshape(1, b, s, 2, DIFF_HEADS, 2 * DIFF_QK_DIM)
    new_nsa_kv_prompt = cs.reshape(1, b, s, 4, kvh, hd)
    new_win_prompt = win.reshape(b, s, 2 * kvh * hd)[:, s - wbp:].reshape(1, b, wbp, 2, kvh, hd)

    diff_s = diff_s.reshape(bs, DIFF_HEADS * LANES).astype(BF16)
    q8 = jnp.pad(nq_s.reshape(bs, kvh, NSA_GROUP, hd), ((0, 0), (0, 0), (0, 8 - NSA_GROUP), (0, 0)))
    new_rows = cs_s.reshape(bs, 4 * kvh, hd)
    o_cmp_s, idx_s = _nsa_sample_select_call(q8, kcvc_p, new_rows, w_cmp[:, 0], past_len)
    idx = idx_s[..., 0]
    nbp = past_len // CMP_BLOCK
    bpp = PAGE_SIZE // CMP_BLOCK
    is_past = (idx >= 0) & (idx < nbp)
    safe = jnp.clip(idx, 0, nbp - 1)
    page = jnp.take_along_axis(page_table, (safe // bpp).reshape(bs, -1), axis=1).reshape(idx.shape)
    new_sel = jnp.any(idx == nbp, axis=-1, keepdims=True)
    pad1 = lambda a: jnp.concatenate([a, jnp.zeros_like(a[..., :1])], axis=-1).reshape(-1).astype(jnp.int32)
    pages_sel, halves_sel = pad1(page), pad1(safe % bpp)
    flags = jnp.concatenate([is_past, new_sel], axis=-1).reshape(-1).astype(jnp.int32)
    gates = ng_s[:, :3 * NSA_HEADS].reshape(bs, 3, kvh, NSA_GROUP).transpose(0, 2, 3, 1)
    gates = jnp.pad(gates, ((0, 0), (0, 0), (0, 8 - NSA_GROUP), (0, 0)))
    gates = jnp.broadcast_to(gates[..., None], (bs, kvh, 8, 3, LANES)).reshape(bs, kvh, 8, 3 * LANES)
    wbs = state_nsa_win.shape[2]
    win_rows = state_nsa_win.reshape(bs, wbs * 2 * kvh, hd)
    nsa_s = _nsa_sample_attend_call(pages_sel, halves_sel, flags, q8, new_rows, win_s.reshape(bs, 2 * kvh, hd),
                                    gates, o_cmp_s, win_rows, cache_n, past_len)
    nsa_s = nsa_s[:, :, :NSA_GROUP].reshape(bs, NSA_HEADS * hd).astype(BF16)

    mix_s = _merge_call(hs, diff_s, nsa_s, w_mg, w_bd, w_bn)
    xs1 = _out_proj_call(mix_s, w_o, xs, norm_mix_post[layer])

    tail = xp1.reshape(b, s, d)[:, s - (CONV_W - 1):].reshape(b * (CONV_W - 1), d)
    rows = jnp.concatenate([xs1, tail], axis=0)
    pad_r = (-rows.shape[0]) % 16
    rows = jnp.pad(rows, ((0, pad_r), (0, 0)))
    h_rows = _norm_call(rows, norm_ffn_pre[layer])
    w_up_p = jnp.concatenate([wa, wg], axis=1)
    u_rows = _proj_call(h_rows, w_up_p, [MODE_NONE] * (2 * fp // 512), 512, F32, (), None, "proj_ffn_up_rows")
    u_unpad = jnp.concatenate([u_rows[:, :f], u_rows[:, fp:fp + f]], axis=1)
    buf = state_ffn_conv[layer]
    bufa = _pad_cols(buf[..., :f], fp).transpose(1, 0, 2)
    bufg = _pad_cols(buf[..., f:], fp).transpose(1, 0, 2)
    xs2 = _ffn_sample_call(u_rows, bufa, bufg, cwa, cwg, ba, bg, wd, xs1, norm_ffn_post[layer])

    new_conv_prompt = u_unpad[bs:bs + b * (CONV_W - 1)].reshape(1, b, CONV_W - 1, 2 * f)
    new_conv_sample = jnp.concatenate([buf[:, 1:], u_unpad[:bs, None]], axis=1)[None]
    new_diff_kv_sample = dkv_s.reshape(1, bs, 1, 2, DIFF_HEADS, 2 * DIFF_QK_DIM)
    new_nsa_kv_sample = cs_s.reshape(1, bs, 1, 4, kvh, hd)
    new_win_sample = jnp.concatenate([state_nsa_win[layer][:, 1:], win_s.reshape(bs, 1, 2, kvh, hd)], axis=1)[None]

    return (xp2.reshape(b, s, d), xs2.reshape(bs, 1, d), new_diff_kv_prompt, new_diff_kv_sample,
            new_nsa_kv_prompt, new_nsa_kv_sample, new_win_prompt, new_win_sample, new_conv_prompt, new_conv_sample)
```

```python
import functools
import math

import jax
import jax.numpy as jnp
from jax import lax
from jax.experimental import pallas as pl
from jax.experimental.pallas import tpu as pltpu

F32 = jnp.float32
BF16 = jnp.bfloat16

DIFF_HEADS = 8
DIFF_QK_DIM = 64
DIFF_V_DIM = 128
NSA_HEADS = 8
NSA_KV_HEADS = 2
NSA_GROUP = 4
NSA_HEAD_DIM = 128
CMP_BLOCK = 64
CMP_SHIFT = 6
N_SEL = 16
WINDOW = 512
PAGE_SIZE = 128
CONV_W = 3
FORCE_BONUS = 1e4
ROPE_THETA = 10000.0
EPS = 1e-6
NEG = -1e30
TINY = 1e-30
LANES = 128
VMEM_LIMIT = 52 * 1024 * 1024

_NT = (((1,), (1,)), ((), ()))


def _nt(a, b):
    return lax.dot_general(a, b, _NT, preferred_element_type=F32)


def _mm(a, b):
    return jnp.dot(a, b, preferred_element_type=F32)


def _cp(sem):
    return pltpu.CompilerParams(dimension_semantics=sem, vmem_limit_bytes=VMEM_LIMIT)


def _pick(n, cands):
    for c in cands:
        if n % c == 0:
            return c
    return n


def _rms(x, g):
    return x * lax.rsqrt(jnp.mean(x * x, axis=-1, keepdims=True) + EPS) * g


def _lam_kernel(l_ref, o_ref, *, lam_init):
    l = l_ref[...]
    a = jnp.sum(l[0:1] * l[1:2], axis=-1, keepdims=True)
    b = jnp.sum(l[2:3] * l[3:4], axis=-1, keepdims=True)
    o_ref[...] = jnp.broadcast_to(jnp.exp(a) - jnp.exp(b) + lam_init, o_ref.shape)


def _lam_call(lam_params, lam_init):
    out = pl.pallas_call(
        functools.partial(_lam_kernel, lam_init=lam_init),
        out_shape=jax.ShapeDtypeStruct((8, LANES), F32),
        name="diff_lambda",
    )(lam_params)
    return out[0, :1]


def _norm_kernel(x_ref, g_ref, o_ref):
    o_ref[...] = _rms(x_ref[...], g_ref[...]).astype(o_ref.dtype)


def _norm_call(x, g):
    m, d = x.shape
    tm = _pick(m, (512, 256, 128, 64, 32, 16))
    return pl.pallas_call(
        _norm_kernel,
        grid=(m // tm,),
        in_specs=[pl.BlockSpec((tm, d), lambda i: (i, 0)), pl.BlockSpec((1, d), lambda i: (0, 0))],
        out_specs=pl.BlockSpec((tm, d), lambda i: (i, 0)),
        out_shape=jax.ShapeDtypeStruct((m, d), BF16),
        compiler_params=_cp(("parallel",)),
        name="rmsnorm",
    )(x, g.reshape(1, d))


MODE_NONE, MODE_ROPE64, MODE_ROPE128, MODE_SIGMOID = 0, 1, 2, 3


def _rope_tables(pos, d):
    half = d // 2
    inv = 1.0 / (ROPE_THETA ** (jnp.arange(0, d, 2, dtype=F32) / d))
    ang = pos.astype(F32)[:, None] * inv[None, :]
    lane = jnp.arange(LANES)
    cos = jnp.cos(ang)[:, lane % half]
    sin = jnp.sin(ang)[:, lane % half]
    first = ((lane % d) < half)[None, :]
    return cos, jnp.where(first, -sin, 0.0), jnp.where(first, 0.0, sin)


def _apply_rope(z, cos, sa, sb, half):
    outs = []
    for c in range(z.shape[1] // LANES):
        blk = z[:, c * LANES:(c + 1) * LANES]
        outs.append(blk * cos + pltpu.roll(blk, LANES - half, 1) * sa + pltpu.roll(blk, half, 1) * sb)
    return outs[0] if len(outs) == 1 else jnp.concatenate(outs, axis=1)


def _proj_kernel(*refs, runs, n_tab):
    h_ref, w_ref = refs[0], refs[1]
    tabs = refs[2:2 + n_tab]
    o_ref = refs[2 + n_tab]
    j = pl.program_id(1)
    acc = _mm(h_ref[...], w_ref[...])

    def emit(mode):
        if mode == MODE_NONE:
            o_ref[...] = acc.astype(o_ref.dtype)
        elif mode == MODE_SIGMOID:
            o_ref[...] = jax.nn.sigmoid(acc).astype(o_ref.dtype)
        else:
            half = 32 if mode == MODE_ROPE64 else 64
            o_ref[...] = _apply_rope(acc, tabs[0][...], tabs[1][...], tabs[2][...], half).astype(o_ref.dtype)

    if len(runs) == 1:
        emit(runs[0][0])
    else:
        for mode, j0, j1 in runs:
            pl.when((j >= j0) & (j < j1))(functools.partial(emit, mode))


def _proj_call(h, w, modes, tn, out_dtype, tables=(), pos_rows=None, name="proj"):
    m, k = h.shape
    n = w.shape[1]
    assert n % tn == 0 and len(modes) == n // tn
    tm = _pick(m, (1024, 512, 256, 128, 64, 48, 32, 16))
    runs = []
    for jj, md in enumerate(modes):
        if runs and runs[-1][0] == md:
            runs[-1][2] = jj + 1
        else:
            runs.append([md, jj, jj + 1])
    runs = tuple(tuple(r) for r in runs)
    in_specs = [pl.BlockSpec((tm, k), lambda i, j: (i, 0)), pl.BlockSpec((k, tn), lambda i, j: (0, j))]
    if tables:
        nblk = pos_rows // tm
        assert nblk * tm == pos_rows
        in_specs += [pl.BlockSpec((tm, LANES), lambda i, j: (i % nblk, 0)) for _ in tables]
    return pl.pallas_call(
        functools.partial(_proj_kernel, runs=runs, n_tab=len(tables)),
        grid=(m // tm, n // tn),
        in_specs=in_specs,
        out_specs=pl.BlockSpec((tm, tn), lambda i, j: (i, j)),
        out_shape=jax.ShapeDtypeStruct((m, n), out_dtype),
        compiler_params=_cp(("parallel", "arbitrary")),
        name=name,
    )(h, w, *tables)


def _flash_step(carry, s, v_ext):
    m, acc = carry
    m_new = jnp.maximum(m, jnp.max(s, axis=-1, keepdims=True))
    p = jnp.exp(s - m_new)
    return m_new, acc * jnp.exp(m - m_new) + _mm(p.astype(BF16), v_ext)


def _flash_init(rows, d):
    return jnp.full((rows, 1), NEG, F32), jnp.zeros((rows, 2 * d), F32)


def _flash_out(carry, d):
    acc = carry[1]
    return acc[:, :d] / jnp.maximum(acc[:, d:d + 1], TINY)


def _ext_ones(v):
    return jnp.concatenate([v, jnp.ones(v.shape, v.dtype)], axis=1)


def _step_id():
    t = pl.program_id(0)
    for ax in range(1, 3):
        t = t * pl.num_programs(ax) + pl.program_id(ax)
    return t


def _pages_per_step(n_req, n_pages, n_steps):
    for n_pp in range(1, n_pages + 1):
        if n_pages % n_pp == 0 and n_req * (n_pages // n_pp) <= n_steps:
            return n_pp
    raise ValueError("prompt grid too small to carry the sample group's page stream")


def _diff_decode_step(pg, n_pg, lam_ref, q_ref, kvn_ref, g_ref, k_refs, v_refs, o_ref, q_sc, m_sc, l_sc, acc_sc,
                      out_scale):
    nh = DIFF_HEADS
    nr = 2 * nh
    prow = PAGE_SIZE * nh

    @pl.when(pg == 0)
    def _():
        q8 = q_ref[0]
        lane = lax.broadcasted_iota(jnp.int32, (nh, LANES), 1)
        scale = DIFF_QK_DIM ** -0.5
        q_sc[0:nh] = (jnp.where(lane < DIFF_QK_DIM, q8, 0.0) * scale).astype(BF16).astype(F32)
        q_sc[nh:nr] = (jnp.where(lane >= DIFF_QK_DIM, q8, 0.0) * scale).astype(BF16).astype(F32)
        m_sc[...] = jnp.full(m_sc.shape, NEG, F32)
        l_sc[...] = jnp.zeros(l_sc.shape, F32)
        acc_sc[...] = jnp.zeros(acc_sc.shape, F32)

    def own_head(width):
        lane = lax.broadcasted_iota(jnp.int32, (nr, width), 1)
        row = lax.broadcasted_iota(jnp.int32, (nr, width), 0)
        return (lane & (nh - 1)) == (row & (nh - 1))

    def update(s, keep, pv):
        sm = jnp.where(keep, s, NEG)
        m_old = m_sc[...]
        m_new = jnp.maximum(m_old, jnp.max(sm, axis=-1, keepdims=True))
        corr = jnp.exp(m_old - m_new)
        pe = jnp.where(keep, jnp.exp(sm - m_new[:, 0:1]), 0.0)
        m_sc[...] = m_new
        l_sc[...] = l_sc[...] * corr + jnp.sum(pe, axis=-1, keepdims=True)
        acc_sc[...] = acc_sc[...] * corr + pv(pe.astype(BF16))

    qb = q_sc[...].astype(BF16)
    s = jnp.concatenate([_nt(qb, kr[...].reshape(prow, LANES).astype(BF16)) for kr in k_refs], axis=1)

    def pv_pages(pb):
        out = None
        for i, vr in enumerate(v_refs):
            part = _mm(pb[:, i * prow:(i + 1) * prow], vr[...].reshape(prow, LANES).astype(BF16))
            out = part if out is None else out + part
        return out

    update(s, own_head(len(k_refs) * prow), pv_pages)

    @pl.when(pg == n_pg - 1)
    def _():
        kvn = kvn_ref[0]
        vk = jnp.concatenate([kvn[nh:nr], kvn[0:nh]], axis=0).astype(BF16)
        is_key = lax.broadcasted_iota(jnp.int32, (nr, nr), 1) < nh
        update(_nt(qb, kvn.astype(BF16)), own_head(nr) & is_key, lambda pb: _mm(pb, vk))
        o = acc_sc[...] / l_sc[...]
        od = o[0:nh] - lam_ref[0] * o[nh:nr]
        o_ref[0] = _rms(od, g_ref[...]) * out_scale


def _diff_attn_kernel(lam_ref, pt_ref, q_ref, k_ref, v_ref, g_ref, qd_ref, kvn_ref, *rest, tq, tk, out_scale, n_pp,
                      n_pg, dec_steps, n_steps):
    k_pages = rest[:n_pp]
    v_pages = rest[n_pp:2 * n_pp]
    o_ref, od_ref = rest[2 * n_pp:2 * n_pp + 2]
    kb_ref, vb_ref, q_sc, m_sc, l_sc, acc_sc = rest[2 * n_pp + 2:]
    qi = pl.program_id(2)
    dv = DIFF_V_DIM

    @pl.when(qi == 0)
    def _():
        kb_ref[...] = k_ref[...].astype(BF16)
        vb_ref[...] = _ext_ones(v_ref[...].astype(BF16))

    q = q_ref[...]
    lane = lax.broadcasted_iota(jnp.int32, q.shape, 1)
    scale = jnp.asarray(DIFF_QK_DIM ** -0.5, q.dtype)
    q1 = jnp.where(lane < DIFF_QK_DIM, q, 0) * scale
    q2 = jnp.where(lane >= DIFF_QK_DIM, q, 0) * scale

    def block(kb, carry, bias):
        start = pl.multiple_of(kb * tk, tk)
        k = kb_ref[pl.ds(start, tk), :]
        v = vb_ref[pl.ds(start, tk), :]
        s1, s2 = _nt(q1, k), _nt(q2, k)
        if bias is not None:
            s1, s2 = s1 + bias, s2 + bias
        return _flash_step(carry[0], s1, v), _flash_step(carry[1], s2, v)

    init = _flash_init(tq, dv)
    nd = tq // tk
    carry = lax.fori_loop(0, qi * nd, lambda kb, c: block(kb, c, None), (init, init))
    for j in range(nd):
        causal = ((j * tk + lax.broadcasted_iota(jnp.int32, (tq, tk), 1))
                  <= lax.broadcasted_iota(jnp.int32, (tq, tk), 0))
        carry = block(qi * nd + j, carry, jnp.where(causal, 0.0, NEG))
    c1, c2 = carry
    o = _flash_out(c1, dv) - lam_ref[0] * _flash_out(c2, dv)
    o_ref[...] = (_rms(o, g_ref[...]) * out_scale).astype(o_ref.dtype)

    t = _step_id()
    decode = functools.partial(_diff_decode_step, lax.rem(t, n_pg), n_pg, lam_ref, qd_ref, kvn_ref, g_ref, k_pages,
                               v_pages, od_ref, q_sc, m_sc, l_sc, acc_sc, out_scale)
    if dec_steps == n_steps:
        decode()
    else:
        pl.when(t < dec_steps)(decode)


def _diff_attn_call(lam, page_table, q, kv, subln, q_dec, kv_new, cache, b, s, lam_init):
    t = b * s
    tq = _pick(s, (512, 256, 128))
    tk = min(tq, 256)
    nq = s // tq
    nh = DIFF_HEADS
    bs, n_pages = page_table.shape
    n_steps = b * nh * nq
    n_pp = _pages_per_step(bs, n_pages, n_steps)
    n_pg = n_pages // n_pp
    dec_steps = bs * n_pg

    def dec(bi, h, qi):
        td = jnp.minimum((bi * nh + h) * nq + qi, dec_steps - 1)
        return td // n_pg, td % n_pg

    def page_spec(kk, half):
        def imap(bi, h, qi, lam_, pt):
            r, g = dec(bi, h, qi)
            return pt[r, g * n_pp + kk], 0, half, 0
        return pl.BlockSpec((None, PAGE_SIZE, nh, LANES), imap)

    req_spec = lambda rows: pl.BlockSpec((1, rows, LANES), lambda bi, h, qi, lam_, pt: (dec(bi, h, qi)[0], 0, 0))
    st = pltpu.VMEM((2 * nh, LANES), F32)
    return pl.pallas_call(
        functools.partial(_diff_attn_kernel, tq=tq, tk=tk, out_scale=1.0 - lam_init, n_pp=n_pp, n_pg=n_pg,
                          dec_steps=dec_steps, n_steps=n_steps),
        grid_spec=pltpu.PrefetchScalarGridSpec(
            num_scalar_prefetch=2,
            grid=(b, nh, nq),
            in_specs=[
                pl.BlockSpec((tq, LANES), lambda bi, h, qi, lam_, pt: (bi * nq + qi, h)),
                pl.BlockSpec((s, LANES), lambda bi, h, qi, lam_, pt: (bi, h)),
                pl.BlockSpec((s, LANES), lambda bi, h, qi, lam_, pt: (bi, nh + h)),
                pl.BlockSpec((1, LANES), lambda bi, h, qi, lam_, pt: (0, 0)),
                req_spec(nh), req_spec(2 * nh),
            ] + [page_spec(kk, 0) for kk in range(n_pp)] + [page_spec(kk, 1) for kk in range(n_pp)],
            out_specs=[pl.BlockSpec((tq, LANES), lambda bi, h, qi, lam_, pt: (bi * nq + qi, h)), req_spec(nh)],
            scratch_shapes=[pltpu.VMEM((s, LANES), BF16), pltpu.VMEM((s, 2 * LANES), BF16), st, st, st, st],
        ),
        out_shape=[jax.ShapeDtypeStruct((t, nh * DIFF_V_DIM), BF16), jax.ShapeDtypeStruct((bs, nh, LANES), F32)],
        compiler_params=_cp(("arbitrary", "arbitrary", "arbitrary")),
        name="diff_attn",
    )(lam, page_table, q, kv, kv, subln.reshape(1, LANES), q_dec, kv_new, *([cache] * (2 * n_pp)))


def _compress_kernel(x_ref, w_ref, o_ref):
    d = NSA_HEAD_DIM
    nblk = x_ref.shape[0] // CMP_BLOCK
    jc = 8
    acc = None
    for j0 in range(0, CMP_BLOCK, jc):
        cols = [x_ref[pl.ds(j, nblk, stride=CMP_BLOCK), :].astype(BF16) for j in range(j0, j0 + jc)]
        part = _mm(jnp.concatenate(cols, axis=1), w_ref[0, j0:j0 + jc].reshape(jc * d, d).astype(BF16))
        acc = part if acc is None else acc + part
    o_ref[0, 0] = acc


def _compress_call(cs, w_cmp):
    t = cs.shape[0]
    kvh, d = NSA_KV_HEADS, NSA_HEAD_DIM
    return pl.pallas_call(
        _compress_kernel,
        grid=(2, kvh),
        in_specs=[pl.BlockSpec((t, d), lambda s, h: (0, s * kvh + h)),
                  pl.BlockSpec((1, CMP_BLOCK, d, d), lambda s, h: (s, 0, 0, 0))],
        out_specs=pl.BlockSpec((1, 1, t // CMP_BLOCK, d), lambda s, h: (s, h, 0, 0)),
        out_shape=jax.ShapeDtypeStruct((2, kvh, t // CMP_BLOCK, d), F32),
        compiler_params=_cp(("parallel", "parallel")),
        name="nsa_compress_prompt",
    )(cs, w_cmp)


def _topk_mask(score, blk, n_sel):
    nb = score.shape[0]
    rank = jnp.zeros(score.shape, jnp.int32)
    for mm in range(nb):
        rm = score[mm:mm + 1, :]
        beats = (rm > score) | ((rm == score) & (blk > mm))
        rank = rank + beats.astype(jnp.int32)
    return rank < n_sel


def _compress_pages_step(w_ref, page_refs, o_ref, acc_sc):
    bpp = PAGE_SIZE // CMP_BLOCK
    d = NSA_HEAD_DIM
    nslot = 2 * NSA_KV_HEADS
    nblk = len(page_refs) * bpp
    jc = 8
    acc = None
    for j0 in range(0, CMP_BLOCK, jc):
        cols = []
        for j in range(j0, j0 + jc):
            tiles = [pr[blk * CMP_BLOCK + j] for pr in page_refs for blk in range(bpp)]
            cols.append(jnp.concatenate(tiles, axis=0).astype(BF16))
        part = _mm(jnp.concatenate(cols, axis=1), w_ref[j0:j0 + jc].reshape(jc * d, 2 * d))
        acc = part if acc is None else acc + part
    acc_sc[0] = acc[:, :d]
    acc_sc[1] = acc[:, d:]
    for s in range(2):
        for h in range(NSA_KV_HEADS):
            o_ref[0, s, h] = acc_sc[s, pl.ds(s * NSA_KV_HEADS + h, nblk, stride=nslot), :]


def _nsa_prompt_kernel(pt_ref, q_ref, kc_ref, vc_ref, ks_ref, vs_ref, kw_ref, vw_ref, ng_ref, wcat_ref, *rest,
                       s_len, tq, tk, n_pp, dec_steps, n_steps):
    page_refs = rest[:n_pp]
    o_ref, oc_ref = rest[n_pp:n_pp + 2]
    ksb, vsb, kwb, vwb, bias_ref, cacc_sc = rest[n_pp + 2:]
    kvh = pl.program_id(1)
    qi = pl.program_id(2)
    nb = s_len // CMP_BLOCK
    g = NSA_GROUP
    rows = g * tq
    d = NSA_HEAD_DIM
    scale = d ** -0.5

    @pl.when(qi == 0)
    def _():
        ksb[...] = ks_ref[...].astype(BF16)
        vsb[...] = _ext_ones(vs_ref[...].astype(BF16))
        kwb[...] = kw_ref[...].astype(BF16)
        vwb[...] = _ext_ones(vw_ref[...].astype(BF16))

    q = q_ref[...]
    qs = [q[:, i * LANES:(i + 1) * LANES] for i in range(g)]
    qr = jnp.concatenate(qs, axis=0)
    q0 = qi * tq
    tpos = q0 + lax.broadcasted_iota(jnp.int32, (tq, 1), 0)
    qpos = q0 + (lax.broadcasted_iota(jnp.int32, (rows, 1), 0) & (tq - 1))

    kc = kc_ref[0, 0, 0].astype(BF16)
    vc = vc_ref[0, 0, 0].astype(BF16)

    def cmp_probs(s, end_le_qpos, axis):
        sm = jnp.where(end_le_qpos, s, NEG)
        p = jnp.where(end_le_qpos, jnp.exp(sm - jnp.max(sm, axis=axis, keepdims=True)), 0.0)
        return p / jnp.maximum(jnp.sum(p, axis=axis, keepdims=True), TINY)

    nblk = lax.broadcasted_iota(jnp.int32, (rows, nb), 1)
    pc = cmp_probs(_nt(qr, kc) * scale, ((nblk + 1) * CMP_BLOCK - 1) <= qpos, 1)
    o_cmp = _mm(pc.astype(BF16), vc)

    nblk_t = lax.broadcasted_iota(jnp.int32, (nb, rows), 0)
    qpos_t = q0 + (lax.broadcasted_iota(jnp.int32, (nb, rows), 1) & (tq - 1))
    pc_t = cmp_probs(_nt(kc, qr) * scale, ((nblk_t + 1) * CMP_BLOCK - 1) <= qpos_t, 0)
    p_slc = pc_t[:, 0:tq]
    for i in range(1, g):
        p_slc = p_slc + pc_t[:, i * tq:(i + 1) * tq]
    blk = lax.broadcasted_iota(jnp.int32, (nb, tq), 0)
    tpos_t = q0 + lax.broadcasted_iota(jnp.int32, (nb, tq), 1)
    cur = lax.shift_right_logical(tpos_t, CMP_SHIFT)
    valid = blk * CMP_BLOCK <= tpos_t
    forced = (blk == 0) | (blk == cur) | (blk == cur - 1)
    score = jnp.where(valid, p_slc + jnp.where(forced, FORCE_BONUS, 0.0), NEG)
    sel_t = _topk_mask(score, blk, min(N_SEL, nb)) & valid
    eye = (lax.broadcasted_iota(jnp.int32, (tq, tq), 0) == lax.broadcasted_iota(jnp.int32, (tq, tq), 1))
    sel = _nt(eye.astype(BF16), sel_t.astype(BF16))
    expand = (lax.shift_right_logical(lax.broadcasted_iota(jnp.int32, (nb, s_len), 1), CMP_SHIFT)
              == lax.broadcasted_iota(jnp.int32, (nb, s_len), 0))
    selk = _mm(sel.astype(BF16), expand.astype(BF16))
    kpos_all = lax.broadcasted_iota(jnp.int32, (1, s_len), 1)
    bias_ref[...] = jnp.where((selk > 0.5) & (kpos_all <= tpos), 0.0, NEG)

    def slc_step(kb, carry):
        start = pl.multiple_of(kb * tk, tk)
        k = ksb[pl.ds(start, tk), :]
        v = vsb[pl.ds(start, tk), :]
        bias = bias_ref[:, pl.ds(start, tk)]
        return tuple(_flash_step(carry[i], _nt(qs[i], k) * scale + bias, v) for i in range(g))

    def win_step(kb, carry):
        start = pl.multiple_of(kb * tk, tk)
        k = kwb[pl.ds(start, tk), :]
        v = vwb[pl.ds(start, tk), :]
        dist = tpos - (start + lax.broadcasted_iota(jnp.int32, (1, tk), 1))
        bias = jnp.where((dist >= 0) & (dist <= WINDOW), 0.0, NEG)
        return tuple(_flash_step(carry[i], _nt(qs[i], k) * scale + bias, v) for i in range(g))

    init = tuple(_flash_init(tq, d) for _ in range(g))
    lo = jnp.maximum(q0 - WINDOW, 0) // tk
    hi = (q0 + tq + tk - 1) // tk
    c_s = lax.fori_loop(0, lo, slc_step, init)
    c_s, c_w = lax.fori_loop(lo, hi, lambda kb, c: (slc_step(kb, c[0]), win_step(kb, c[1])), (c_s, init))

    ng = ng_ref[...]
    for i in range(g):
        def gate(r):
            c0 = r * NSA_HEADS + i
            c1 = c0 + NSA_GROUP
            return jnp.where(kvh == 0, ng[:, c0:c0 + 1], ng[:, c1:c1 + 1])
        o = (gate(0) * o_cmp[i * tq:(i + 1) * tq] + gate(1) * _flash_out(c_s[i], d)
             + gate(2) * _flash_out(c_w[i], d))
        o_ref[:, i * LANES:(i + 1) * LANES] = o.astype(o_ref.dtype)

    compress = functools.partial(_compress_pages_step, wcat_ref, page_refs, oc_ref, cacc_sc)
    if dec_steps == n_steps:
        compress()
    else:
        pl.when(_step_id() < dec_steps)(compress)


def _nsa_attn_call(page_table, nq, kcvc, nkv_cs, nkv_win, ng, w_cat, cache_halves, b, s):
    t = b * s
    tq = _pick(s, (256, 128))
    nqb = s // tq
    nb = s // CMP_BLOCK
    gw = NSA_GROUP * LANES
    kvh = NSA_KV_HEADS
    d = NSA_HEAD_DIM
    nslot = 2 * kvh
    bpp = PAGE_SIZE // CMP_BLOCK
    bs, n_pages = page_table.shape
    n_steps = b * kvh * nqb
    n_pp = _pages_per_step(bs, n_pages, n_steps)
    n_pg = n_pages // n_pp
    dec_steps = bs * n_pg
    kcvc5 = kcvc.reshape(2, kvh, b, nb, d)

    def dec(bi, h, qi):
        td = jnp.minimum((bi * kvh + h) * nqb + qi, dec_steps - 1)
        return td // n_pg, td % n_pg

    def page_spec(kk):
        def imap(bi, h, qi, pt):
            r, g = dec(bi, h, qi)
            return pt[r, g * n_pp + kk], 0, 0, 0, 0
        return pl.BlockSpec((None, PAGE_SIZE, None, nslot, d), imap)

    def summary_map(bi, h, qi, pt):
        r, g = dec(bi, h, qi)
        return r, 0, 0, g, 0

    kv_spec = lambda col: pl.BlockSpec((s, LANES), lambda bi, h, qi, pt: (bi, col + h))
    return pl.pallas_call(
        functools.partial(_nsa_prompt_kernel, s_len=s, tq=tq, tk=_pick(s, (256, 128)), n_pp=n_pp,
                          dec_steps=dec_steps, n_steps=n_steps),
        grid_spec=pltpu.PrefetchScalarGridSpec(
            num_scalar_prefetch=1,
            grid=(b, kvh, nqb),
            in_specs=[
                pl.BlockSpec((tq, gw), lambda bi, h, qi, pt: (bi * nqb + qi, h)),
                pl.BlockSpec((1, 1, 1, nb, d), lambda bi, h, qi, pt: (0, h, bi, 0, 0)),
                pl.BlockSpec((1, 1, 1, nb, d), lambda bi, h, qi, pt: (1, h, bi, 0, 0)),
                kv_spec(4), kv_spec(6),
                kv_spec(0), kv_spec(2),
                pl.BlockSpec((tq, LANES), lambda bi, h, qi, pt: (bi * nqb + qi, 0)),
                pl.BlockSpec(w_cat.shape, lambda bi, h, qi, pt: (0, 0, 0)),
            ] + [page_spec(kk) for kk in range(n_pp)],
            out_specs=[pl.BlockSpec((tq, gw), lambda bi, h, qi, pt: (bi * nqb + qi, h)),
                       pl.BlockSpec((1, 2, kvh, n_pp * bpp, d), summary_map)],
            scratch_shapes=[pltpu.VMEM((s, LANES), BF16), pltpu.VMEM((s, 2 * LANES), BF16),
                            pltpu.VMEM((s, LANES), BF16), pltpu.VMEM((s, 2 * LANES), BF16),
                            pltpu.VMEM((tq, s), F32), pltpu.VMEM((2, n_pp * bpp * nslot, d), F32)],
        ),
        out_shape=[jax.ShapeDtypeStruct((t, NSA_HEADS * d), BF16),
                   jax.ShapeDtypeStruct((bs, 2, kvh, n_pages * bpp, d), F32)],
        compiler_params=_cp(("arbitrary", "arbitrary", "arbitrary")),
        name="nsa_attn",
    )(page_table, nq, kcvc5, kcvc5, nkv_cs, nkv_cs, nkv_win, nkv_win, ng, w_cat, *([cache_halves] * n_pp))


def _merge_kernel(h_ref, d_ref, n_ref, wg0_ref, wg1_ref, wbd_ref, wbn_ref, o_ref):
    h = h_ref[...]
    a = _mm(d_ref[...], wbd_ref[...])
    bb = _mm(n_ref[...], wbn_ref[...])
    g0 = jax.nn.sigmoid(_mm(h, wg0_ref[...]))
    g1 = jax.nn.sigmoid(_mm(h, wg1_ref[...]))
    o_ref[...] = (g0 * a + g1 * bb).astype(o_ref.dtype)


def _merge_call(h, diff, nsa, w_mg, w_bd, w_bn):
    m, d = h.shape
    tm = _pick(m, (1024, 512, 256, 128, 64, 32))
    tn = _pick(d, (256, 128))
    nj = d // tn
    row = lambda kk: pl.BlockSpec((tm, kk), lambda i, j: (i, 0))
    return pl.pallas_call(
        _merge_kernel,
        grid=(m // tm, nj),
        in_specs=[row(d), row(diff.shape[1]), row(nsa.shape[1]),
                  pl.BlockSpec((d, tn), lambda i, j: (0, j)),
                  pl.BlockSpec((d, tn), lambda i, j: (0, nj + j)),
                  pl.BlockSpec((w_bd.shape[0], tn), lambda i, j: (0, j)),
                  pl.BlockSpec((w_bn.shape[0], tn), lambda i, j: (0, j))],
        out_specs=pl.BlockSpec((tm, tn), lambda i, j: (i, j)),
        out_shape=jax.ShapeDtypeStruct((m, d), BF16),
        compiler_params=_cp(("parallel", "arbitrary")),
        name="branch_merge",
    )(h, diff, nsa, w_mg, w_mg, w_bd, w_bn)


def _out_proj_kernel(m_ref, w_ref, x_ref, g_ref, o_ref):
    y = _mm(m_ref[...], w_ref[...])
    o_ref[...] = x_ref[...] + _rms(y, g_ref[...])


def _out_proj_call(mix, w_o, x, g):
    m, d = x.shape
    tm = _pick(m, (512, 256, 128, 64, 32))
    return pl.pallas_call(
        _out_proj_kernel,
        grid=(m // tm,),
        in_specs=[pl.BlockSpec((tm, d), lambda i: (i, 0)), pl.BlockSpec((d, d), lambda i: (0, 0)),
                  pl.BlockSpec((tm, d), lambda i: (i, 0)), pl.BlockSpec((1, d), lambda i: (0, 0))],
        out_specs=pl.BlockSpec((tm, d), lambda i: (i, 0)),
        out_shape=jax.ShapeDtypeStruct((m, d), F32),
        compiler_params=_cp(("parallel",)),
        name="out_proj_norm_residual",
    )(mix, w_o, x, g.reshape(1, d))


HALO = 16


def _gelu_glu(ca, cg):
    return jax.nn.gelu(ca, approximate=True) * cg


def _ffn_prompt_kernel(x_ref, xh_ref, gpre_ref, gpost_ref, wa_ref, wg_ref, cwa_ref, cwg_ref, ba_ref, bg_ref,
                       wd_ref, o_ref, h_sc, ua_sc, ug_sc, acc_sc, *, tm, blocks_per_seq):
    i = pl.program_id(0)
    j = pl.program_id(1)

    @pl.when(j == 0)
    def _():
        h_sc[0:HALO, :] = _rms(xh_ref[...], gpre_ref[...]).astype(BF16)
        h_sc[HALO:, :] = _rms(x_ref[...], gpre_ref[...]).astype(BF16)
        acc_sc[...] = jnp.zeros(acc_sc.shape, F32)

    first = (i % blocks_per_seq) == 0
    h = h_sc[...]
    keep = jnp.logical_not(first & (lax.broadcasted_iota(jnp.int32, (tm + HALO, 1), 0) < HALO))
    ua_sc[...] = jnp.where(keep, _mm(h, wa_ref[...]), 0.0)
    ug_sc[...] = jnp.where(keep, _mm(h, wg_ref[...]), 0.0)

    def conv(u_sc, cw_ref, b_ref):
        cw = cw_ref[...]
        out = b_ref[...]
        for tap in range(CONV_W):
            out = out + u_sc[pl.ds(HALO - (CONV_W - 1) + tap, tm), :] * cw[tap:tap + 1]
        return out

    act = _gelu_glu(conv(ua_sc, cwa_ref, ba_ref), conv(ug_sc, cwg_ref, bg_ref))
    acc_sc[...] += _mm(act.astype(BF16), wd_ref[...])

    @pl.when(j == pl.num_programs(1) - 1)
    def _():
        o_ref[...] = x_ref[...] + _rms(acc_sc[...], gpost_ref[...])


def _ffn_prompt_call(x, g_pre, g_post, wa, wg, cwa, cwg, ba, bg, wd, s):
    m, d = x.shape
    fp = wa.shape[1]
    tm = _pick(s, (512, 256, 128))
    tf = _pick(fp, (512, 256, 128))
    hb = tm // HALO
    col = lambda r: pl.BlockSpec((r, tf), lambda i, j: (0, j))
    return pl.pallas_call(
        functools.partial(_ffn_prompt_kernel, tm=tm, blocks_per_seq=s // tm),
        grid=(m // tm, fp // tf),
        in_specs=[pl.BlockSpec((tm, d), lambda i, j: (i, 0)),
                  pl.BlockSpec((HALO, d), lambda i, j: (jnp.maximum(i * hb - 1, 0), 0)),
                  pl.BlockSpec((1, d), lambda i, j: (0, 0)), pl.BlockSpec((1, d), lambda i, j: (0, 0)),
                  col(d), col(d), col(CONV_W), col(CONV_W), col(1), col(1),
                  pl.BlockSpec((tf, d), lambda i, j: (j, 0))],
        out_specs=pl.BlockSpec((tm, d), lambda i, j: (i, 0)),
        out_shape=jax.ShapeDtypeStruct((m, d), F32),
        scratch_shapes=[pltpu.VMEM((tm + HALO, d), BF16), pltpu.VMEM((tm + HALO, tf), F32),
                        pltpu.VMEM((tm + HALO, tf), F32), pltpu.VMEM((tm, d), F32)],
        compiler_params=_cp(("parallel", "arbitrary")),
        name="conv_ffn_prompt",
    )(x, x, g_pre.reshape(1, d), g_post.reshape(1, d), wa, wg, cwa, cwg, ba, bg, wd)


def _ffn_sample_kernel(ua_ref, ug_ref, bufa_ref, bufg_ref, cwa_ref, cwg_ref, ba_ref, bg_ref, wd_ref, x_ref,
                       gpost_ref, o_ref, acc_sc):
    j = pl.program_id(0)

    @pl.when(j == 0)
    def _():
        acc_sc[...] = jnp.zeros(acc_sc.shape, F32)

    def conv(u_ref, buf_ref, cw_ref, b_ref):
        cw = cw_ref[...]
        out = b_ref[...] + u_ref[...] * cw[CONV_W - 1:CONV_W]
        for tap in range(CONV_W - 1):
            out = out + buf_ref[tap] * cw[tap:tap + 1]
        return out

    act = _gelu_glu(conv(ua_ref, bufa_ref, cwa_ref, ba_ref), conv(ug_ref, bufg_ref, cwg_ref, bg_ref))
    acc_sc[...] += _mm(act.astype(BF16), wd_ref[...])

    @pl.when(j == pl.num_programs(0) - 1)
    def _():
        o_ref[...] = x_ref[...] + _rms(acc_sc[...], gpost_ref[...])


def _ffn_sample_call(u, bufa, bufg, cwa, cwg, ba, bg, wd, x, g_post):
    bsz, d = x.shape
    fp = wd.shape[0]
    tf = _pick(fp, (512, 256, 128))
    nf = fp // tf
    col = lambda r: pl.BlockSpec((r, tf), lambda j: (0, j))
    return pl.pallas_call(
        _ffn_sample_kernel,
        grid=(nf,),
        in_specs=[pl.BlockSpec((bsz, tf), lambda j: (0, j)), pl.BlockSpec((bsz, tf), lambda j: (0, nf + j)),
                  pl.BlockSpec((CONV_W - 1, bsz, tf), lambda j: (0, 0, j)),
                  pl.BlockSpec((CONV_W - 1, bsz, tf), lambda j: (0, 0, j)),
                  col(CONV_W), col(CONV_W), col(1), col(1),
                  pl.BlockSpec((tf, d), lambda j: (j, 0)),
                  pl.BlockSpec((bsz, d), lambda j: (0, 0)), pl.BlockSpec((1, d), lambda j: (0, 0))],
        out_specs=pl.BlockSpec((bsz, d), lambda j: (0, 0)),
        out_shape=jax.ShapeDtypeStruct((bsz, d), F32),
        scratch_shapes=[pltpu.VMEM((bsz, d), F32)],
        compiler_params=_cp(("arbitrary",)),
        name="conv_ffn_sample",
    )(u, u, bufa, bufg, cwa, cwg, ba, bg, wd, x, g_post.reshape(1, d))


def _nsa_sample_select_kernel(q_ref, kc_ref, vc_ref, new_ref, w0_ref, ocmp_ref, idx_ref, *, past_len):
    kvh = pl.program_id(1)
    nbp = kc_ref.shape[3]
    nbt = nbp + 1
    qpos = past_len
    scale = NSA_HEAD_DIM ** -0.5
    qg = q_ref[0, 0].astype(BF16)
    kc = kc_ref[0, 0, 0].astype(BF16)
    vc = vc_ref[0, 0, 0].astype(BF16)
    new = new_ref[0]
    pick = lambda r: jnp.where(kvh == 0, new[r:r + 1], new[r + 1:r + 2])
    kc_new = _mm(jnp.broadcast_to(pick(0), (8, LANES)).astype(BF16), w0_ref[0].astype(BF16))
    vc_new = _mm(jnp.broadcast_to(pick(2), (8, LANES)).astype(BF16), w0_ref[1].astype(BF16))

    s_past = _nt(qg, kc) * scale
    s_new = _nt(qg, kc_new.astype(BF16))[:, 0:1] * scale
    n_past = lax.broadcasted_iota(jnp.int32, (8, nbp), 1)
    mask_past = ((n_past + 1) * CMP_BLOCK - 1) <= qpos
    mask_new = ((nbt * CMP_BLOCK - 1) <= qpos)
    s_past = jnp.where(mask_past, s_past, NEG)
    s_new = s_new if mask_new else jnp.full_like(s_new, NEG)
    m = jnp.maximum(jnp.max(s_past, axis=-1, keepdims=True), s_new)
    p_past = jnp.where(mask_past, jnp.exp(s_past - m), 0.0)
    p_new = jnp.exp(s_new - m) * (1.0 if mask_new else 0.0)
    den = jnp.maximum(jnp.sum(p_past, axis=-1, keepdims=True) + p_new, TINY)
    p_past = p_past / den
    p_new = p_new / den
    o_cmp = _mm(p_past.astype(BF16), vc) + p_new.astype(BF16).astype(F32) * vc_new.astype(BF16).astype(F32)[0:1]
    ocmp_ref[0, 0] = o_cmp

    grp = lax.broadcasted_iota(jnp.int32, (8, 1), 0) < NSA_GROUP
    ps_past = jnp.sum(jnp.where(grp, p_past, 0.0), axis=0, keepdims=True)
    ps_new = jnp.sum(jnp.where(grp, p_new, 0.0), axis=0, keepdims=True)

    width = ((nbt + LANES - 1) // LANES) * LANES
    nrow = ((nbt + 7) // 8) * 8
    n = lax.broadcasted_iota(jnp.int32, (1, width), 1)
    p_all = jnp.concatenate([ps_past, jnp.broadcast_to(ps_new, (1, width - nbp))], axis=1)
    cur = qpos >> CMP_SHIFT
    valid = (n * CMP_BLOCK <= qpos) & (n < nbt)
    forced = (n == 0) | (n == cur) | (n == cur - 1)
    score = jnp.where(valid, p_all + jnp.where(forced, FORCE_BONUS, 0.0), NEG)
    mrow = lax.broadcasted_iota(jnp.int32, (nrow, width), 0)
    ncol = lax.broadcasted_iota(jnp.int32, (nrow, width), 1)
    score_b = jnp.broadcast_to(score, (nrow, width))
    score_col = jnp.sum(jnp.where(mrow == ncol, score_b, 0.0), axis=1, keepdims=True)
    col_ok = lax.broadcasted_iota(jnp.int32, (nrow, 1), 0) < nbt
    beats = col_ok & ((score_col > score_b) | ((score_col == score_b) & (mrow < ncol)))
    rank = jnp.sum(jnp.where(beats, 1.0, 0.0), axis=0, keepdims=True)
    n_sel = min(N_SEL, nbt)
    r = lax.broadcasted_iota(jnp.int32, (N_SEL, width), 0)
    hit = ((jnp.broadcast_to(rank, (N_SEL, width)) == r.astype(F32)) & jnp.broadcast_to(valid, (N_SEL, width))
           & (r < n_sel))
    nf = jnp.broadcast_to(n, (N_SEL, width)).astype(F32)
    idx = jnp.sum(jnp.where(hit, nf, 0.0), axis=1, keepdims=True)
    any_hit = jnp.sum(jnp.where(hit, 1.0, 0.0), axis=1, keepdims=True)
    idx = jnp.where(any_hit > 0.5, idx, -1.0).astype(jnp.int32)
    idx_ref[0, 0] = jnp.broadcast_to(idx, (N_SEL, LANES))


def _nsa_sample_select_call(q8, kcvc_p, new_rows, w0, past_len):
    bsz = q8.shape[0]
    nbp = kcvc_p.shape[3]
    return pl.pallas_call(
        functools.partial(_nsa_sample_select_kernel, past_len=past_len),
        grid=(bsz, NSA_KV_HEADS),
        in_specs=[pl.BlockSpec((1, 1, 8, LANES), lambda b, h: (b, h, 0, 0)),
                  pl.BlockSpec((1, 1, 1, nbp, LANES), lambda b, h: (b, 0, h, 0, 0)),
                  pl.BlockSpec((1, 1, 1, nbp, LANES), lambda b, h: (b, 1, h, 0, 0)),
                  pl.BlockSpec((1, 8, LANES), lambda b, h: (b, 0, 0)),
                  pl.BlockSpec((2, LANES, LANES), lambda b, h: (0, 0, 0))],
        out_specs=[pl.BlockSpec((1, 1, 8, LANES), lambda b, h: (b, h, 0, 0)),
                   pl.BlockSpec((1, 1, N_SEL, LANES), lambda b, h: (b, h, 0, 0))],
        out_shape=[jax.ShapeDtypeStruct((bsz, NSA_KV_HEADS, 8, LANES), F32),
                   jax.ShapeDtypeStruct((bsz, NSA_KV_HEADS, N_SEL, LANES), jnp.int32)],
        compiler_params=_cp(("parallel", "parallel")),
        name="nsa_select_sample",
    )(q8, kcvc_p, kcvc_p, new_rows, w0)


def _nsa_sample_attend_kernel(page_ref, half_ref, flag_ref, q_ref, new_ref, wnew_ref, gate_ref, ocmp_ref, win_ref,
                              *rest, past_len):
    blk_refs = rest[:N_SEL]
    o_ref = rest[N_SEL]
    b = pl.program_id(0)
    kvh = pl.program_id(1)
    nkv = NSA_KV_HEADS
    base = (b * nkv + kvh) * (N_SEL + 1)
    scale = NSA_HEAD_DIM ** -0.5
    qg = q_ref[0, 0].astype(BF16)
    new = new_ref[0]
    wnew = wnew_ref[0]
    pick = lambda arr, r: jnp.where(kvh == 0, arr[r:r + 1], arr[r + 1:r + 2])
    rnd = lambda a: a.astype(BF16).astype(F32)
    qf = qg.astype(F32)

    def attend(x, keep, k_new, v_new, new_on):
        s = jnp.where(keep, _nt(qg, x) * scale, NEG)
        s_n = jnp.sum(qf * rnd(k_new), axis=-1, keepdims=True) * scale
        s_n = jnp.where(new_on, s_n, NEG)
        m = jnp.maximum(jnp.max(s, axis=-1, keepdims=True), s_n)
        p = jnp.where(keep, jnp.exp(s - m), 0.0)
        p_n = jnp.where(new_on, jnp.exp(s_n - m), 0.0)
        den = jnp.maximum(jnp.sum(p, axis=-1, keepdims=True) + p_n, TINY)
        p_v = pltpu.roll(p, nkv, 1)
        return (_mm(p_v.astype(BF16), x) + rnd(p_n) * rnd(v_new)) / den

    nslot = 4 * nkv
    brow = CMP_BLOCK * nslot
    x_sel = jnp.concatenate([r[...].reshape(brow, LANES).astype(BF16) for r in blk_refs], axis=0)
    lane = lax.broadcasted_iota(jnp.int32, (1, N_SEL * brow), 1)
    chunk = lax.shift_right_logical(lane, brow.bit_length() - 1)
    live = jnp.zeros((1, N_SEL * brow), jnp.int32)
    for r in range(N_SEL):
        live = jnp.where(chunk == r, flag_ref[base + r], live)
    keep = (live > 0) & ((lane & (nslot - 1)) == 2 * nkv + kvh)
    o_slc = attend(x_sel, jnp.broadcast_to(keep, (8, N_SEL * brow)), pick(new, 4), pick(new, 6),
                   flag_ref[base + N_SEL] > 0)

    wrows = win_ref.shape[1]
    wb = wrows // (2 * nkv)
    wl = lax.broadcasted_iota(jnp.int32, (8, wrows), 1)
    kpos = past_len - wb + lax.shift_right_logical(wl, (2 * nkv).bit_length() - 1)
    dist = past_len - kpos
    wkeep = (dist >= 0) & (dist <= WINDOW) & (kpos >= 0) & ((wl & (2 * nkv - 1)) == kvh)
    o_win = attend(win_ref[0].astype(BF16), wkeep, pick(wnew, 0), pick(wnew, 2), True)

    gate = gate_ref[0, 0]
    o_ref[0, 0] = (gate[:, 0:LANES] * ocmp_ref[0, 0] + gate[:, LANES:2 * LANES] * o_slc
                   + gate[:, 2 * LANES:3 * LANES] * o_win)


def _nsa_sample_attend_call(pages, halves, flags, q8, new_rows, win_new, gates, o_cmp, win_rows, cache_rows,
                            past_len):
    bsz = q8.shape[0]
    kvh = NSA_KV_HEADS
    nslot = 4 * kvh
    sel = lambda ref, b, h, r: ref[(b * kvh + h) * (N_SEL + 1) + r]
    blk_spec = lambda r: pl.BlockSpec((None, CMP_BLOCK, nslot, LANES),
                                      lambda b, h, pg, hf, fl: (sel(pg, b, h, r), sel(hf, b, h, r), 0, 0))
    return pl.pallas_call(
        functools.partial(_nsa_sample_attend_kernel, past_len=past_len),
        grid_spec=pltpu.PrefetchScalarGridSpec(
            num_scalar_prefetch=3,
            grid=(bsz, kvh),
            in_specs=[pl.BlockSpec((1, 1, 8, LANES), lambda b, h, pg, hf, fl: (b, h, 0, 0)),
                      pl.BlockSpec((1, 8, LANES), lambda b, h, pg, hf, fl: (b, 0, 0)),
                      pl.BlockSpec((1, 4, LANES), lambda b, h, pg, hf, fl: (b, 0, 0)),
                      pl.BlockSpec((1, 1, 8, 3 * LANES), lambda b, h, pg, hf, fl: (b, h, 0, 0)),
                      pl.BlockSpec((1, 1, 8, LANES), lambda b, h, pg, hf, fl: (b, h, 0, 0)),
                      pl.BlockSpec((1, win_rows.shape[1], LANES), lambda b, h, pg, hf, fl: (b, 0, 0))]
                     + [blk_spec(r) for r in range(N_SEL)],
            out_specs=pl.BlockSpec((1, 1, 8, LANES), lambda b, h, pg, hf, fl: (b, h, 0, 0)),
        ),
        out_shape=jax.ShapeDtypeStruct((bsz, kvh, 8, LANES), F32),
        compiler_params=_cp(("parallel", "parallel")),
        name="nsa_attend_sample",
    )(pages, halves, flags, q8, new_rows, win_new, gates, o_cmp, win_rows, *([cache_rows] * N_SEL))


def _pad_cols(a, n):
    return jnp.pad(a, [(0, 0)] * (a.ndim - 1) + [(0, n - a.shape[-1])])


def kernel(x_prompt, x_sample, cache_diff_kv, cache_nsa_kv, state_nsa_win, state_ffn_conv, page_table,
           norm_mix_pre, norm_mix_post, w_in, diff_lambda, diff_subln, nsa_w_cmp, w_branch_diff, w_branch_nsa,
           w_out, norm_ffn_pre, norm_ffn_post, ffn_w_up, ffn_conv_w, ffn_conv_b, ffn_w_down):
    b, s, d = x_prompt.shape
    bs, ts, _ = x_sample.shape
    depth = w_in.shape[0]
    assert depth == 1 and ts == 1
    n_pages = page_table.shape[1]
    past_len = n_pages * PAGE_SIZE
    f = ffn_w_down.shape[1]
    t = b * s
    kvh, hd = NSA_KV_HEADS, NSA_HEAD_DIM
    layer = 0
    lam_init = 0.8 - 0.6 * math.exp(-0.3 * layer)

    wq = DIFF_HEADS * 2 * DIFF_QK_DIM
    wv = DIFF_HEADS * DIFF_V_DIM
    wnq = NSA_HEADS * hd
    wkv = 2 * kvh * hd
    o_dq, o_dk, o_dv = 0, wq, 2 * wq
    o_nq = o_dv + wv
    o_cs = o_nq + wnq
    o_win = o_cs + 2 * wkv
    o_ng = o_win + wkv
    o_mg = o_ng + 3 * NSA_HEADS
    w = w_in[layer]
    w_dq = w[:, o_dq:o_dk].astype(BF16)
    w_dkv = w[:, o_dk:o_nq].astype(BF16)
    w_nq = w[:, o_nq:o_cs].astype(BF16)
    w_cs = w[:, o_cs:o_win].astype(BF16)
    w_win = w[:, o_win:o_ng].astype(BF16)
    w_ng = _pad_cols(w[:, o_ng:o_mg], LANES).astype(BF16)
    w_mg = w[:, o_mg:].astype(BF16)
    w_bd = w_branch_diff[layer].astype(BF16)
    w_bn = w_branch_nsa[layer].astype(BF16)
    w_o = w_out[layer].astype(BF16)
    fp = ((f + 511) // 512) * 512
    w_up = ffn_w_up[layer]
    wa = _pad_cols(w_up[:, :f], fp).astype(BF16)
    wg = _pad_cols(w_up[:, f:], fp).astype(BF16)
    cw = ffn_conv_w[layer]
    cwa, cwg = _pad_cols(cw[:, :f], fp), _pad_cols(cw[:, f:], fp)
    cb = ffn_conv_b[layer].reshape(1, 2 * f)
    ba, bg = _pad_cols(cb[:, :f], fp), _pad_cols(cb[:, f:], fp)
    wd = jnp.pad(ffn_w_down[layer], ((0, fp - f), (0, 0))).astype(BF16)
    w_cmp = nsa_w_cmp[layer]

    lam = _lam_call(diff_lambda[layer], lam_init)

    def projections(h, pos, pos_rows, attn_dtype):
        t64 = _rope_tables(pos, DIFF_QK_DIM)
        t128 = _rope_tables(pos, hd)
        r64, r128, nn, sg = MODE_ROPE64, MODE_ROPE128, MODE_NONE, MODE_SIGMOID
        dq = _proj_call(h, w_dq, [r64] * (wq // 512), 512, attn_dtype, t64, pos_rows, "proj_diff_q")
        dkv = _proj_call(h, w_dkv, [r64] * (wq // 512) + [nn] * (wv // 512), 512, F32, t64, pos_rows, "proj_diff_kv")
        nq = _proj_call(h, w_nq, [r128] * (wnq // 512), 512, attn_dtype, t128, pos_rows, "proj_nsa_q")
        cs = _proj_call(h, w_cs, [r128, nn, r128, nn], wkv // 2, F32, t128, pos_rows, "proj_nsa_kv")
        win = _proj_call(h, w_win, [r128, nn], wkv // 2, F32, t128, pos_rows, "proj_nsa_win")
        ng = _proj_call(h, w_ng, [sg], LANES, F32, (), None, "proj_nsa_gate")
        return dq, dkv, nq, cs, win, ng

    xp = x_prompt.reshape(t, d)
    hp = _norm_call(xp, norm_mix_pre[layer])
    pos_p = jnp.arange(s, dtype=jnp.int32)
    dq, dkv, nq, cs, win, ng = projections(hp, pos_p, s, BF16)

    xs = x_sample.reshape(bs, d)
    hs = _norm_call(xs, norm_mix_pre[layer])
    pos_s = jnp.full((bs,), past_len, jnp.int32)
    dq_s, dkv_s, nq_s, cs_s, win_s, ng_s = projections(hs, pos_s, bs, F32)

    n_pool = cache_nsa_kv.shape[1]
    cache_d = cache_diff_kv.reshape(cache_diff_kv.shape[1], PAGE_SIZE, 2 * DIFF_HEADS, LANES)
    cache_n = cache_nsa_kv.reshape(n_pool, PAGE_SIZE, 4 * kvh, hd)
    w_cat = jnp.concatenate([w_cmp[0], w_cmp[1]], axis=-1).astype(BF16)

    diff, diff_s = _diff_attn_call(lam, page_table, dq, dkv, diff_subln[layer], dq_s.reshape(bs, DIFF_HEADS, LANES),
                                   dkv_s.reshape(bs, 2 * DIFF_HEADS, LANES), cache_d, b, s, lam_init)
    kcvc = _compress_call(cs, w_cmp)
    cache_n_halves = cache_nsa_kv.reshape(n_pool, PAGE_SIZE, 2, 2 * kvh, hd)
    nsa, kcvc_p = _nsa_attn_call(page_table, nq, kcvc, cs, win, ng, w_cat, cache_n_halves, b, s)
    mix = _merge_call(hp, diff, nsa, w_mg, w_bd, w_bn)
    xp1 = _out_proj_call(mix, w_o, xp, norm_mix_post[layer])
    xp2 = _ffn_prompt_call(xp1, norm_ffn_pre[layer], norm_ffn_post[layer], wa, wg, cwa, cwg, ba, bg, wd, s)

    wbp = min(WINDOW, s)
    new_diff_kv_prompt = dkv.reshape(1, b, s, 2, DIFF_HEADS, 2 * DIFF_QK_DIM)
    new_nsa_kv_prompt = cs.reshape(1, b, s, 4, kvh, hd)
    new_win_prompt = win.reshape(b, s, 2 * kvh * hd)[:, s - wbp:].reshape(1, b, wbp, 2, kvh, hd)

    diff_s = diff_s.reshape(bs, DIFF_HEADS * LANES).astype(BF16)
    q8 = jnp.pad(nq_s.reshape(bs, kvh, NSA_GROUP, hd), ((0, 0), (0, 0), (0, 8 - NSA_GROUP), (0, 0)))
    new_rows = cs_s.reshape(bs, 4 * kvh, hd)
    o_cmp_s, idx_s = _nsa_sample_select_call(q8, kcvc_p, new_rows, w_cmp[:, 0], past_len)
    idx = idx_s[..., 0]
    nbp = past_len // CMP_BLOCK
    bpp = PAGE_SIZE // CMP_BLOCK
    is_past = (idx >= 0) & (idx < nbp)
    safe = jnp.clip(idx, 0, nbp - 1)
    page = jnp.take_along_axis(page_table, (safe // bpp).reshape(bs, -1), axis=1).reshape(idx.shape)
    new_sel = jnp.any(idx == nbp, axis=-1, keepdims=True)
    pad1 = lambda a: jnp.concatenate([a, jnp.zeros_like(a[..., :1])], axis=-1).reshape(-1).astype(jnp.int32)
    pages_sel, halves_sel = pad1(page), pad1(safe % bpp)
    flags = jnp.concatenate([is_past, new_sel], axis=-1).reshape(-1).astype(jnp.int32)
    gates = ng_s[:, :3 * NSA_HEADS].reshape(bs, 3, kvh, NSA_GROUP).transpose(0, 2, 3, 1)
    gates = jnp.pad(gates, ((0, 0), (0, 0), (0, 8 - NSA_GROUP), (0, 0)))
    gates = jnp.broadcast_to(gates[..., None], (bs, kvh, 8, 3, LANES)).reshape(bs, kvh, 8, 3 * LANES)
    wbs = state_nsa_win.shape[2]
    win_rows = state_nsa_win.reshape(bs, wbs * 2 * kvh, hd)
    nsa_s = _nsa_sample_attend_call(pages_sel, halves_sel, flags, q8, new_rows, win_s.reshape(bs, 2 * kvh, hd),
                                    gates, o_cmp_s, win_rows, cache_n, past_len)
    nsa_s = nsa_s[:, :, :NSA_GROUP].reshape(bs, NSA_HEADS * hd).astype(BF16)

    mix_s = _merge_call(hs, diff_s, nsa_s, w_mg, w_bd, w_bn)
    xs1 = _out_proj_call(mix_s, w_o, xs, norm_mix_post[layer])

    tail = xp1.reshape(b, s, d)[:, s - (CONV_W - 1):].reshape(b * (CONV_W - 1), d)
    rows = jnp.concatenate([xs1, tail], axis=0)
    pad_r = (-rows.shape[0]) % 16
    rows = jnp.pad(rows, ((0, pad_r), (0, 0)))
    h_rows = _norm_call(rows, norm_ffn_pre[layer])
    w_up_p = jnp.concatenate([wa, wg], axis=1)
    u_rows = _proj_call(h_rows, w_up_p, [MODE_NONE] * (2 * fp // 512), 512, F32, (), None, "proj_ffn_up_rows")
    u_unpad = jnp.concatenate([u_rows[:, :f], u_rows[:, fp:fp + f]], axis=1)
    buf = state_ffn_conv[layer]
    bufa = _pad_cols(buf[..., :f], fp).transpose(1, 0, 2)
    bufg = _pad_cols(buf[..., f:], fp).transpose(1, 0, 2)
    xs2 = _ffn_sample_call(u_rows, bufa, bufg, cwa, cwg, ba, bg, wd, xs1, norm_ffn_post[layer])

    new_conv_prompt = u_unpad[bs:bs + b * (CONV_W - 1)].reshape(1, b, CONV_W - 1, 2 * f)
    new_conv_sample = jnp.concatenate([buf[:, 1:], u_unpad[:bs, None]], axis=1)[None]
    new_diff_kv_sample = dkv_s.reshape(1, bs, 1, 2, DIFF_HEADS, 2 * DIFF_QK_DIM)
    new_nsa_kv_sample = cs_s.reshape(1, bs, 1, 4, kvh, hd)
    new_win_sample = jnp.concatenate([state_nsa_win[layer][:, 1:], win_s.reshape(bs, 1, 2, kvh, hd)], axis=1)[None]

    return (xp2.reshape(b, s, d), xs2.reshape(bs, 1, d), new_diff_kv_prompt, new_diff_kv_sample,
            new_nsa_kv_prompt, new_nsa_kv_sample, new_win_prompt, new_win_sample, new_conv_prompt, new_conv_sample)
```

```python
import functools
import math

import jax
import jax.numpy as jnp
from jax import lax
from jax.experimental import pallas as pl
from jax.experimental.pallas import tpu as pltpu

F32 = jnp.float32
BF16 = jnp.bfloat16

DIFF_HEADS = 8
DIFF_QK_DIM = 64
DIFF_V_DIM = 128
NSA_HEADS = 8
NSA_KV_HEADS = 2
NSA_GROUP = 4
NSA_HEAD_DIM = 128
CMP_BLOCK = 64
CMP_SHIFT = 6
N_SEL = 16
WINDOW = 512
PAGE_SIZE = 128
CONV_W = 3
FORCE_BONUS = 1e4
ROPE_THETA = 10000.0
EPS = 1e-6
NEG = -1e30
TINY = 1e-30
LANES = 128
VMEM_LIMIT = 52 * 1024 * 1024

_NT = (((1,), (1,)), ((), ()))


def _nt(a, b):
    return lax.dot_general(a, b, _NT, preferred_element_type=F32)


def _mm(a, b):
    return jnp.dot(a, b, preferred_element_type=F32)


def _cp(sem):
    return pltpu.CompilerParams(dimension_semantics=sem, vmem_limit_bytes=VMEM_LIMIT)


def _pick(n, cands):
    for c in cands:
        if n % c == 0:
            return c
    return n


def _rms(x, g):
    return x * lax.rsqrt(jnp.mean(x * x, axis=-1, keepdims=True) + EPS) * g


def _lam_kernel(l_ref, o_ref, *, lam_init):
    l = l_ref[...]
    a = jnp.sum(l[0:1] * l[1:2], axis=-1, keepdims=True)
    b = jnp.sum(l[2:3] * l[3:4], axis=-1, keepdims=True)
    o_ref[...] = jnp.broadcast_to(jnp.exp(a) - jnp.exp(b) + lam_init, o_ref.shape)


def _lam_call(lam_params, lam_init):
    out = pl.pallas_call(
        functools.partial(_lam_kernel, lam_init=lam_init),
        out_shape=jax.ShapeDtypeStruct((8, LANES), F32),
        name="diff_lambda",
    )(lam_params)
    return out[0, :1]


def _norm_kernel(x_ref, g_ref, o_ref):
    o_ref[...] = _rms(x_ref[...], g_ref[...]).astype(o_ref.dtype)


def _norm_call(x, g):
    m, d = x.shape
    tm = _pick(m, (512, 256, 128, 64, 32, 16))
    return pl.pallas_call(
        _norm_kernel,
        grid=(m // tm,),
        in_specs=[pl.BlockSpec((tm, d), lambda i: (i, 0)), pl.BlockSpec((1, d), lambda i: (0, 0))],
        out_specs=pl.BlockSpec((tm, d), lambda i: (i, 0)),
        out_shape=jax.ShapeDtypeStruct((m, d), BF16),
        compiler_params=_cp(("parallel",)),
        name="rmsnorm",
    )(x, g.reshape(1, d))


MODE_NONE, MODE_ROPE64, MODE_ROPE128, MODE_SIGMOID = 0, 1, 2, 3


def _rope_tables(pos, d):
    half = d // 2
    inv = 1.0 / (ROPE_THETA ** (jnp.arange(0, d, 2, dtype=F32) / d))
    ang = pos.astype(F32)[:, None] * inv[None, :]
    lane = jnp.arange(LANES)
    cos = jnp.cos(ang)[:, lane % half]
    sin = jnp.sin(ang)[:, lane % half]
    first = ((lane % d) < half)[None, :]
    return cos, jnp.where(first, -sin, 0.0), jnp.where(first, 0.0, sin)


def _apply_rope(z, cos, sa, sb, half):
    outs = []
    for c in range(z.shape[1] // LANES):
        blk = z[:, c * LANES:(c + 1) * LANES]
        outs.append(blk * cos + pltpu.roll(blk, LANES - half, 1) * sa + pltpu.roll(blk, half, 1) * sb)
    return outs[0] if len(outs) == 1 else jnp.concatenate(outs, axis=1)


def _proj_kernel(*refs, modes, sub, n_tab):
    h_ref, w_ref = refs[0], refs[1]
    tabs = refs[2:2 + n_tab]
    o_ref = refs[2 + n_tab]
    acc = _mm(h_ref[...], w_ref[...])
    for c, mode in enumerate(modes):
        blk = acc[:, c * sub:(c + 1) * sub]
        if mode == MODE_SIGMOID:
            blk = jax.nn.sigmoid(blk)
        elif mode != MODE_NONE:
            half = 32 if mode == MODE_ROPE64 else 64
            blk = _apply_rope(blk, tabs[0][...], tabs[1][...], tabs[2][...], half)
        o_ref[:, c * sub:(c + 1) * sub] = blk.astype(o_ref.dtype)


def _proj_call(h, w, modes, sub, tn, out_dtype, tables=(), pos_rows=None, name="proj"):
    m, k = h.shape
    n = w.shape[1]
    per = tn // sub
    assert n % tn == 0 and per * sub == tn and len(modes) * sub == n
    assert all(tuple(modes[j:j + per]) == tuple(modes[:per]) for j in range(0, len(modes), per))
    tm = _pick(m, (1024, 512, 256, 128, 64, 48, 32, 16))
    in_specs = [pl.BlockSpec((tm, k), lambda i, j: (i, 0)), pl.BlockSpec((k, tn), lambda i, j: (0, j))]
    if tables:
        nblk = pos_rows // tm
        assert nblk * tm == pos_rows
        in_specs += [pl.BlockSpec((tm, LANES), lambda i, j: (i % nblk, 0)) for _ in tables]
    return pl.pallas_call(
        functools.partial(_proj_kernel, modes=tuple(modes[:per]), sub=sub, n_tab=len(tables)),
        grid=(m // tm, n // tn),
        in_specs=in_specs,
        out_specs=pl.BlockSpec((tm, tn), lambda i, j: (i, j)),
        out_shape=jax.ShapeDtypeStruct((m, n), out_dtype),
        compiler_params=_cp(("parallel", "arbitrary")),
        name=name,
    )(h, w, *tables)


def _flash_step(carry, s, v_ext):
    m, acc = carry
    m_new = jnp.maximum(m, jnp.max(s, axis=-1, keepdims=True))
    p = jnp.exp(s - m_new)
    return m_new, acc * jnp.exp(m - m_new) + _mm(p.astype(BF16), v_ext)


def _flash_init(rows, d):
    return jnp.full((rows, 1), NEG, F32), jnp.zeros((rows, 2 * d), F32)


def _flash_out(carry, d):
    acc = carry[1]
    return acc[:, :d] / jnp.maximum(acc[:, d:d + 1], TINY)


def _ext_ones(v):
    return jnp.concatenate([v, jnp.ones(v.shape, v.dtype)], axis=1)


def _step_id():
    t = pl.program_id(0)
    for ax in range(1, 3):
        t = t * pl.num_programs(ax) + pl.program_id(ax)
    return t


def _pages_per_step(n_req, n_pages, n_steps):
    for n_pp in range(1, n_pages + 1):
        if n_pages % n_pp == 0 and n_req * (n_pages // n_pp) <= n_steps:
            return n_pp
    raise ValueError("prompt grid too small to carry the sample group's page stream")


def _diff_decode_step(pg, n_pg, lam_ref, q_ref, kvn_ref, g_ref, k_refs, v_refs, o_ref, q_sc, m_sc, l_sc, acc_sc,
                      out_scale):
    nh = DIFF_HEADS
    nr = 2 * nh
    prow = PAGE_SIZE * nh

    @pl.when(pg == 0)
    def _():
        q8 = q_ref[0]
        lane = lax.broadcasted_iota(jnp.int32, (nh, LANES), 1)
        scale = DIFF_QK_DIM ** -0.5
        q_sc[0:nh] = (jnp.where(lane < DIFF_QK_DIM, q8, 0.0) * scale).astype(BF16).astype(F32)
        q_sc[nh:nr] = (jnp.where(lane >= DIFF_QK_DIM, q8, 0.0) * scale).astype(BF16).astype(F32)
        m_sc[...] = jnp.full(m_sc.shape, NEG, F32)
        l_sc[...] = jnp.zeros(l_sc.shape, F32)
        acc_sc[...] = jnp.zeros(acc_sc.shape, F32)

    def own_head(width):
        lane = lax.broadcasted_iota(jnp.int32, (nr, width), 1)
        row = lax.broadcasted_iota(jnp.int32, (nr, width), 0)
        return (lane & (nh - 1)) == (row & (nh - 1))

    def update(s, keep, pv):
        sm = jnp.where(keep, s, NEG)
        m_old = m_sc[...]
        m_new = jnp.maximum(m_old, jnp.max(sm, axis=-1, keepdims=True))
        corr = jnp.exp(m_old - m_new)
        pe = jnp.where(keep, jnp.exp(sm - m_new[:, 0:1]), 0.0)
        m_sc[...] = m_new
        l_sc[...] = l_sc[...] * corr + jnp.sum(pe, axis=-1, keepdims=True)
        acc_sc[...] = acc_sc[...] * corr + pv(pe.astype(BF16))

    qb = q_sc[...].astype(BF16)
    s = jnp.concatenate([_nt(qb, kr[...].reshape(prow, LANES).astype(BF16)) for kr in k_refs], axis=1)

    def pv_pages(pb):
        out = None
        for i, vr in enumerate(v_refs):
            part = _mm(pb[:, i * prow:(i + 1) * prow], vr[...].reshape(prow, LANES).astype(BF16))
            out = part if out is None else out + part
        return out

    update(s, own_head(len(k_refs) * prow), pv_pages)

    @pl.when(pg == n_pg - 1)
    def _():
        kvn = kvn_ref[0]
        vk = jnp.concatenate([kvn[nh:nr], kvn[0:nh]], axis=0).astype(BF16)
        is_key = lax.broadcasted_iota(jnp.int32, (nr, nr), 1) < nh
        update(_nt(qb, kvn.astype(BF16)), own_head(nr) & is_key, lambda pb: _mm(pb, vk))
        o = acc_sc[...] / l_sc[...]
        od = o[0:nh] - lam_ref[0] * o[nh:nr]
        o_ref[0] = _rms(od, g_ref[...]) * out_scale


def _diff_attn_kernel(lam_ref, pt_ref, q_ref, k_ref, v_ref, g_ref, qd_ref, kvn_ref, *rest, tq, tk, out_scale, n_pp,
                      n_pg, dec_steps, n_steps):
    k_pages = rest[:n_pp]
    v_pages = rest[n_pp:2 * n_pp]
    o_ref, od_ref = rest[2 * n_pp:2 * n_pp + 2]
    kb_ref, vb_ref, q_sc, m_sc, l_sc, acc_sc = rest[2 * n_pp + 2:]
    qi = pl.program_id(2)
    dv = DIFF_V_DIM

    @pl.when(qi == 0)
    def _():
        kb_ref[...] = k_ref[...].astype(BF16)
        vb_ref[...] = _ext_ones(v_ref[...].astype(BF16))

    q = q_ref[...]
    lane = lax.broadcasted_iota(jnp.int32, q.shape, 1)
    scale = jnp.asarray(DIFF_QK_DIM ** -0.5, q.dtype)
    q1 = jnp.where(lane < DIFF_QK_DIM, q, 0) * scale
    q2 = jnp.where(lane >= DIFF_QK_DIM, q, 0) * scale

    def block(kb, carry, bias):
        start = pl.multiple_of(kb * tk, tk)
        k = kb_ref[pl.ds(start, tk), :]
        v = vb_ref[pl.ds(start, tk), :]
        s1, s2 = _nt(q1, k), _nt(q2, k)
        if bias is not None:
            s1, s2 = s1 + bias, s2 + bias
        return _flash_step(carry[0], s1, v), _flash_step(carry[1], s2, v)

    init = _flash_init(tq, dv)
    nd = tq // tk
    carry = lax.fori_loop(0, qi * nd, lambda kb, c: block(kb, c, None), (init, init))
    for j in range(nd):
        causal = ((j * tk + lax.broadcasted_iota(jnp.int32, (tq, tk), 1))
                  <= lax.broadcasted_iota(jnp.int32, (tq, tk), 0))
        carry = block(qi * nd + j, carry, jnp.where(causal, 0.0, NEG))
    c1, c2 = carry
    o = _flash_out(c1, dv) - lam_ref[0] * _flash_out(c2, dv)
    o_ref[...] = (_rms(o, g_ref[...]) * out_scale).astype(o_ref.dtype)

    t = _step_id()
    decode = functools.partial(_diff_decode_step, lax.rem(t, n_pg), n_pg, lam_ref, qd_ref, kvn_ref, g_ref, k_pages,
                               v_pages, od_ref, q_sc, m_sc, l_sc, acc_sc, out_scale)
    if dec_steps == n_steps:
        decode()
    else:
        pl.when(t < dec_steps)(decode)


def _diff_attn_call(lam, page_table, q, kv, subln, q_dec, kv_new, cache, b, s, lam_init):
    t = b * s
    tq = _pick(s, (512, 256, 128))
    tk = min(tq, 256)
    nq = s // tq
    nh = DIFF_HEADS
    bs, n_pages = page_table.shape
    n_steps = b * nh * nq
    n_pp = _pages_per_step(bs, n_pages, n_steps)
    n_pg = n_pages // n_pp
    dec_steps = bs * n_pg

    def dec(bi, h, qi):
        td = jnp.minimum((bi * nh + h) * nq + qi, dec_steps - 1)
        return td // n_pg, td % n_pg

    def page_spec(kk, half):
        def imap(bi, h, qi, lam_, pt):
            r, g = dec(bi, h, qi)
            return pt[r, g * n_pp + kk], 0, half, 0
        return pl.BlockSpec((None, PAGE_SIZE, nh, LANES), imap)

    req_spec = lambda rows: pl.BlockSpec((1, rows, LANES), lambda bi, h, qi, lam_, pt: (dec(bi, h, qi)[0], 0, 0))
    st = pltpu.VMEM((2 * nh, LANES), F32)
    return pl.pallas_call(
        functools.partial(_diff_attn_kernel, tq=tq, tk=tk, out_scale=1.0 - lam_init, n_pp=n_pp, n_pg=n_pg,
                          dec_steps=dec_steps, n_steps=n_steps),
        grid_spec=pltpu.PrefetchScalarGridSpec(
            num_scalar_prefetch=2,
            grid=(b, nh, nq),
            in_specs=[
                pl.BlockSpec((tq, LANES), lambda bi, h, qi, lam_, pt: (bi * nq + qi, h)),
                pl.BlockSpec((s, LANES), lambda bi, h, qi, lam_, pt: (bi, h)),
                pl.BlockSpec((s, LANES), lambda bi, h, qi, lam_, pt: (bi, nh + h)),
                pl.BlockSpec((1, LANES), lambda bi, h, qi, lam_, pt: (0, 0)),
                req_spec(nh), req_spec(2 * nh),
            ] + [page_spec(kk, 0) for kk in range(n_pp)] + [page_spec(kk, 1) for kk in range(n_pp)],
            out_specs=[pl.BlockSpec((tq, LANES), lambda bi, h, qi, lam_, pt: (bi * nq + qi, h)), req_spec(nh)],
            scratch_shapes=[pltpu.VMEM((s, LANES), BF16), pltpu.VMEM((s, 2 * LANES), BF16), st, st, st, st],
        ),
        out_shape=[jax.ShapeDtypeStruct((t, nh * DIFF_V_DIM), BF16), jax.ShapeDtypeStruct((bs, nh, LANES), F32)],
        compiler_params=_cp(("arbitrary", "arbitrary", "arbitrary")),
        name="diff_attn",
    )(lam, page_table, q, kv, kv, subln.reshape(1, LANES), q_dec, kv_new, *([cache] * (2 * n_pp)))


def _compress_kernel(x_ref, w_ref, o_ref):
    d = NSA_HEAD_DIM
    nblk = x_ref.shape[0] // CMP_BLOCK
    jc = 8
    acc = None
    for j0 in range(0, CMP_BLOCK, jc):
        cols = [x_ref[pl.ds(j, nblk, stride=CMP_BLOCK), :].astype(BF16) for j in range(j0, j0 + jc)]
        part = _mm(jnp.concatenate(cols, axis=1), w_ref[0, j0:j0 + jc].reshape(jc * d, d).astype(BF16))
        acc = part if acc is None else acc + part
    o_ref[0, 0] = acc


def _compress_call(cs, w_cmp):
    t = cs.shape[0]
    kvh, d = NSA_KV_HEADS, NSA_HEAD_DIM
    return pl.pallas_call(
        _compress_kernel,
        grid=(2, kvh),
        in_specs=[pl.BlockSpec((t, d), lambda s, h: (0, s * kvh + h)),
                  pl.BlockSpec((1, CMP_BLOCK, d, d), lambda s, h: (s, 0, 0, 0))],
        out_specs=pl.BlockSpec((1, 1, t // CMP_BLOCK, d), lambda s, h: (s, h, 0, 0)),
        out_shape=jax.ShapeDtypeStruct((2, kvh, t // CMP_BLOCK, d), F32),
        compiler_params=_cp(("parallel", "parallel")),
        name="nsa_compress_prompt",
    )(cs, w_cmp)


def _topk_mask(score, blk, n_sel):
    nb = score.shape[0]
    rank = jnp.zeros(score.shape, jnp.int32)
    for mm in range(nb):
        rm = score[mm:mm + 1, :]
        beats = (rm > score) | ((rm == score) & (blk > mm))
        rank = rank + beats.astype(jnp.int32)
    return rank < n_sel


def _compress_pages_step(w_ref, page_refs, o_ref, acc_sc):
    bpp = PAGE_SIZE // CMP_BLOCK
    d = NSA_HEAD_DIM
    nslot = 2 * NSA_KV_HEADS
    nblk = len(page_refs) * bpp
    jc = 8
    acc = None
    for j0 in range(0, CMP_BLOCK, jc):
        cols = []
        for j in range(j0, j0 + jc):
            tiles = [pr[blk * CMP_BLOCK + j] for pr in page_refs for blk in range(bpp)]
            cols.append(jnp.concatenate(tiles, axis=0).astype(BF16))
        part = _mm(jnp.concatenate(cols, axis=1), w_ref[j0:j0 + jc].reshape(jc * d, 2 * d))
        acc = part if acc is None else acc + part
    acc_sc[0] = acc[:, :d]
    acc_sc[1] = acc[:, d:]
    for s in range(2):
        for h in range(NSA_KV_HEADS):
            o_ref[0, s, h] = acc_sc[s, pl.ds(s * NSA_KV_HEADS + h, nblk, stride=nslot), :]


def _nsa_prompt_kernel(pt_ref, q_ref, kc_ref, vc_ref, ks_ref, vs_ref, kw_ref, vw_ref, ng_ref, wcat_ref, *rest,
                       s_len, tq, tk, n_pp, dec_steps, n_steps):
    page_refs = rest[:n_pp]
    o_ref, oc_ref = rest[n_pp:n_pp + 2]
    ksb, vsb, kwb, vwb, bias_ref, cacc_sc = rest[n_pp + 2:]
    kvh = pl.program_id(1)
    qi = pl.program_id(2)
    nb = s_len // CMP_BLOCK
    g = NSA_GROUP
    rows = g * tq
    d = NSA_HEAD_DIM
    scale = d ** -0.5

    @pl.when(qi == 0)
    def _():
        ksb[...] = ks_ref[...].astype(BF16)
        vsb[...] = _ext_ones(vs_ref[...].astype(BF16))
        kwb[...] = kw_ref[...].astype(BF16)
        vwb[...] = _ext_ones(vw_ref[...].astype(BF16))

    q = q_ref[...]
    qs = [q[:, i * LANES:(i + 1) * LANES] for i in range(g)]
    qr = jnp.concatenate(qs, axis=0)
    q0 = qi * tq
    tpos = q0 + lax.broadcasted_iota(jnp.int32, (tq, 1), 0)
    qpos = q0 + (lax.broadcasted_iota(jnp.int32, (rows, 1), 0) & (tq - 1))

    kc = kc_ref[0, 0, 0].astype(BF16)
    vc = vc_ref[0, 0, 0].astype(BF16)

    def cmp_probs(s, end_le_qpos, axis):
        sm = jnp.where(end_le_qpos, s, NEG)
        p = jnp.where(end_le_qpos, jnp.exp(sm - jnp.max(sm, axis=axis, keepdims=True)), 0.0)
        return p / jnp.maximum(jnp.sum(p, axis=axis, keepdims=True), TINY)

    nblk = lax.broadcasted_iota(jnp.int32, (rows, nb), 1)
    pc = cmp_probs(_nt(qr, kc) * scale, ((nblk + 1) * CMP_BLOCK - 1) <= qpos, 1)
    o_cmp = _mm(pc.astype(BF16), vc)

    nblk_t = lax.broadcasted_iota(jnp.int32, (nb, rows), 0)
    qpos_t = q0 + (lax.broadcasted_iota(jnp.int32, (nb, rows), 1) & (tq - 1))
    pc_t = cmp_probs(_nt(kc, qr) * scale, ((nblk_t + 1) * CMP_BLOCK - 1) <= qpos_t, 0)
    p_slc = pc_t[:, 0:tq]
    for i in range(1, g):
        p_slc = p_slc + pc_t[:, i * tq:(i + 1) * tq]
    blk = lax.broadcasted_iota(jnp.int32, (nb, tq), 0)
    tpos_t = q0 + lax.broadcasted_iota(jnp.int32, (nb, tq), 1)
    cur = lax.shift_right_logical(tpos_t, CMP_SHIFT)
    valid = blk * CMP_BLOCK <= tpos_t
    forced = (blk == 0) | (blk == cur) | (blk == cur - 1)
    score = jnp.where(valid, p_slc + jnp.where(forced, FORCE_BONUS, 0.0), NEG)
    sel_t = _topk_mask(score, blk, min(N_SEL, nb)) & valid
    eye = (lax.broadcasted_iota(jnp.int32, (tq, tq), 0) == lax.broadcasted_iota(jnp.int32, (tq, tq), 1))
    sel = _nt(eye.astype(BF16), sel_t.astype(BF16))
    expand = (lax.shift_right_logical(lax.broadcasted_iota(jnp.int32, (nb, s_len), 1), CMP_SHIFT)
              == lax.broadcasted_iota(jnp.int32, (nb, s_len), 0))
    selk = _mm(sel.astype(BF16), expand.astype(BF16))
    kpos_all = lax.broadcasted_iota(jnp.int32, (1, s_len), 1)
    bias_ref[...] = jnp.where((selk > 0.5) & (kpos_all <= tpos), 0.0, NEG)

    def slc_step(kb, carry):
        start = pl.multiple_of(kb * tk, tk)
        k = ksb[pl.ds(start, tk), :]
        v = vsb[pl.ds(start, tk), :]
        bias = bias_ref[:, pl.ds(start, tk)]
        return tuple(_flash_step(carry[i], _nt(qs[i], k) * scale + bias, v) for i in range(g))

    def win_step(kb, carry):
        start = pl.multiple_of(kb * tk, tk)
        k = kwb[pl.ds(start, tk), :]
        v = vwb[pl.ds(start, tk), :]
        dist = tpos - (start + lax.broadcasted_iota(jnp.int32, (1, tk), 1))
        bias = jnp.where((dist >= 0) & (dist <= WINDOW), 0.0, NEG)
        return tuple(_flash_step(carry[i], _nt(qs[i], k) * scale + bias, v) for i in range(g))

    init = tuple(_flash_init(tq, d) for _ in range(g))
    lo = jnp.maximum(q0 - WINDOW, 0) // tk
    hi = (q0 + tq + tk - 1) // tk
    c_s = lax.fori_loop(0, lo, slc_step, init)
    c_s, c_w = lax.fori_loop(lo, hi, lambda kb, c: (slc_step(kb, c[0]), win_step(kb, c[1])), (c_s, init))

    ng = ng_ref[...]
    for i in range(g):
        def gate(r):
            c0 = r * NSA_HEADS + i
            c1 = c0 + NSA_GROUP
            return jnp.where(kvh == 0, ng[:, c0:c0 + 1], ng[:, c1:c1 + 1])
        o = (gate(0) * o_cmp[i * tq:(i + 1) * tq] + gate(1) * _flash_out(c_s[i], d)
             + gate(2) * _flash_out(c_w[i], d))
        o_ref[:, i * LANES:(i + 1) * LANES] = o.astype(o_ref.dtype)

    compress = functools.partial(_compress_pages_step, wcat_ref, page_refs, oc_ref, cacc_sc)
    if dec_steps == n_steps:
        compress()
    else:
        pl.when(_step_id() < dec_steps)(compress)


def _nsa_attn_call(page_table, nq, kcvc, nkv_cs, nkv_win, ng, w_cat, cache_halves, b, s):
    t = b * s
    tq = _pick(s, (256, 128))
    nqb = s // tq
    nb = s // CMP_BLOCK
    gw = NSA_GROUP * LANES
    kvh = NSA_KV_HEADS
    d = NSA_HEAD_DIM
    nslot = 2 * kvh
    bpp = PAGE_SIZE // CMP_BLOCK
    bs, n_pages = page_table.shape
    n_steps = b * kvh * nqb
    n_pp = _pages_per_step(bs, n_pages, n_steps)
    n_pg = n_pages // n_pp
    dec_steps = bs * n_pg
    kcvc5 = kcvc.reshape(2, kvh, b, nb, d)

    def dec(bi, h, qi):
        td = jnp.minimum((bi * kvh + h) * nqb + qi, dec_steps - 1)
        return td // n_pg, td % n_pg

    def page_spec(kk):
        def imap(bi, h, qi, pt):
            r, g = dec(bi, h, qi)
            return pt[r, g * n_pp + kk], 0, 0, 0, 0
        return pl.BlockSpec((None, PAGE_SIZE, None, nslot, d), imap)

    def summary_map(bi, h, qi, pt):
        r, g = dec(bi, h, qi)
        return r, 0, 0, g, 0

    kv_spec = lambda col: pl.BlockSpec((s, LANES), lambda bi, h, qi, pt: (bi, col + h))
    return pl.pallas_call(
        functools.partial(_nsa_prompt_kernel, s_len=s, tq=tq, tk=_pick(s, (256, 128)), n_pp=n_pp,
                          dec_steps=dec_steps, n_steps=n_steps),
        grid_spec=pltpu.PrefetchScalarGridSpec(
            num_scalar_prefetch=1,
            grid=(b, kvh, nqb),
            in_specs=[
                pl.BlockSpec((tq, gw), lambda bi, h, qi, pt: (bi * nqb + qi, h)),
                pl.BlockSpec((1, 1, 1, nb, d), lambda bi, h, qi, pt: (0, h, bi, 0, 0)),
                pl.BlockSpec((1, 1, 1, nb, d), lambda bi, h, qi, pt: (1, h, bi, 0, 0)),
                kv_spec(4), kv_spec(6),
                kv_spec(0), kv_spec(2),
                pl.BlockSpec((tq, LANES), lambda bi, h, qi, pt: (bi * nqb + qi, 0)),
                pl.BlockSpec(w_cat.shape, lambda bi, h, qi, pt: (0, 0, 0)),
            ] + [page_spec(kk) for kk in range(n_pp)],
            out_specs=[pl.BlockSpec((tq, gw), lambda bi, h, qi, pt: (bi * nqb + qi, h)),
                       pl.BlockSpec((1, 2, kvh, n_pp * bpp, d), summary_map)],
            scratch_shapes=[pltpu.VMEM((s, LANES), BF16), pltpu.VMEM((s, 2 * LANES), BF16),
                            pltpu.VMEM((s, LANES), BF16), pltpu.VMEM((s, 2 * LANES), BF16),
                            pltpu.VMEM((tq, s), F32), pltpu.VMEM((2, n_pp * bpp * nslot, d), F32)],
        ),
        out_shape=[jax.ShapeDtypeStruct((t, NSA_HEADS * d), BF16),
                   jax.ShapeDtypeStruct((bs, 2, kvh, n_pages * bpp, d), F32)],
        compiler_params=_cp(("arbitrary", "arbitrary", "arbitrary")),
        name="nsa_attn",
    )(page_table, nq, kcvc5, kcvc5, nkv_cs, nkv_cs, nkv_win, nkv_win, ng, w_cat, *([cache_halves] * n_pp))


def _merge_kernel(h_ref, d_ref, n_ref, wg0_ref, wg1_ref, wbd_ref, wbn_ref, o_ref):
    h = h_ref[...]
    a = _mm(d_ref[...], wbd_ref[...])
    bb = _mm(n_ref[...], wbn_ref[...])
    g0 = jax.nn.sigmoid(_mm(h, wg0_ref[...]))
    g1 = jax.nn.sigmoid(_mm(h, wg1_ref[...]))
    o_ref[...] = (g0 * a + g1 * bb).astype(o_ref.dtype)


def _merge_call(h, diff, nsa, w_mg, w_bd, w_bn):
    m, d = h.shape
    tm = _pick(m, (1024, 512, 256, 128, 64, 32))
    tn = _pick(d, (256, 128))
    nj = d // tn
    row = lambda kk: pl.BlockSpec((tm, kk), lambda i, j: (i, 0))
    return pl.pallas_call(
        _merge_kernel,
        grid=(m // tm, nj),
        in_specs=[row(d), row(diff.shape[1]), row(nsa.shape[1]),
                  pl.BlockSpec((d, tn), lambda i, j: (0, j)),
                  pl.BlockSpec((d, tn), lambda i, j: (0, nj + j)),
                  pl.BlockSpec((w_bd.shape[0], tn), lambda i, j: (0, j)),
                  pl.BlockSpec((w_bn.shape[0], tn), lambda i, j: (0, j))],
        out_specs=pl.BlockSpec((tm, tn), lambda i, j: (i, j)),
        out_shape=jax.ShapeDtypeStruct((m, d), BF16),
        compiler_params=_cp(("parallel", "arbitrary")),
        name="branch_merge",
    )(h, diff, nsa, w_mg, w_mg, w_bd, w_bn)


def _out_proj_kernel(m_ref, w_ref, x_ref, g_ref, o_ref):
    y = _mm(m_ref[...], w_ref[...])
    o_ref[...] = x_ref[...] + _rms(y, g_ref[...])


def _out_proj_call(mix, w_o, x, g):
    m, d = x.shape
    tm = _pick(m, (512, 256, 128, 64, 32))
    return pl.pallas_call(
        _out_proj_kernel,
        grid=(m // tm,),
        in_specs=[pl.BlockSpec((tm, d), lambda i: (i, 0)), pl.BlockSpec((d, d), lambda i: (0, 0)),
                  pl.BlockSpec((tm, d), lambda i: (i, 0)), pl.BlockSpec((1, d), lambda i: (0, 0))],
        out_specs=pl.BlockSpec((tm, d), lambda i: (i, 0)),
        out_shape=jax.ShapeDtypeStruct((m, d), F32),
        compiler_params=_cp(("parallel",)),
        name="out_proj_norm_residual",
    )(mix, w_o, x, g.reshape(1, d))


HALO = 16


def _gelu_glu(ca, cg):
    return jax.nn.gelu(ca, approximate=True) * cg


def _ffn_prompt_kernel(x_ref, xh_ref, gpre_ref, gpost_ref, wa_ref, wg_ref, cwa_ref, cwg_ref, ba_ref, bg_ref,
                       wd_ref, o_ref, h_sc, ua_sc, ug_sc, acc_sc, *, tm, blocks_per_seq):
    i = pl.program_id(0)
    j = pl.program_id(1)

    @pl.when(j == 0)
    def _():
        h_sc[0:HALO, :] = _rms(xh_ref[...], gpre_ref[...]).astype(BF16)
        h_sc[HALO:, :] = _rms(x_ref[...], gpre_ref[...]).astype(BF16)
        acc_sc[...] = jnp.zeros(acc_sc.shape, F32)

    first = (i % blocks_per_seq) == 0
    h = h_sc[...]
    keep = jnp.logical_not(first & (lax.broadcasted_iota(jnp.int32, (tm + HALO, 1), 0) < HALO))
    ua_sc[...] = jnp.where(keep, _mm(h, wa_ref[...]), 0.0)
    ug_sc[...] = jnp.where(keep, _mm(h, wg_ref[...]), 0.0)

    def conv(u_sc, cw_ref, b_ref):
        cw = cw_ref[...]
        out = b_ref[...]
        for tap in range(CONV_W):
            out = out + u_sc[pl.ds(HALO - (CONV_W - 1) + tap, tm), :] * cw[tap:tap + 1]
        return out

    act = _gelu_glu(conv(ua_sc, cwa_ref, ba_ref), conv(ug_sc, cwg_ref, bg_ref))
    acc_sc[...] += _mm(act.astype(BF16), wd_ref[...])

    @pl.when(j == pl.num_programs(1) - 1)
    def _():
        o_ref[...] = x_ref[...] + _rms(acc_sc[...], gpost_ref[...])


def _ffn_prompt_call(x, g_pre, g_post, wa, wg, cwa, cwg, ba, bg, wd, s):
    m, d = x.shape
    fp = wa.shape[1]
    tm = _pick(s, (512, 256, 128))
    tf = _pick(fp, (512, 256, 128))
    hb = tm // HALO
    col = lambda r: pl.BlockSpec((r, tf), lambda i, j: (0, j))
    return pl.pallas_call(
        functools.partial(_ffn_prompt_kernel, tm=tm, blocks_per_seq=s // tm),
        grid=(m // tm, fp // tf),
        in_specs=[pl.BlockSpec((tm, d), lambda i, j: (i, 0)),
                  pl.BlockSpec((HALO, d), lambda i, j: (jnp.maximum(i * hb - 1, 0), 0)),
                  pl.BlockSpec((1, d), lambda i, j: (0, 0)), pl.BlockSpec((1, d), lambda i, j: (0, 0)),
                  col(d), col(d), col(CONV_W), col(CONV_W), col(1), col(1),
                  pl.BlockSpec((tf, d), lambda i, j: (j, 0))],
        out_specs=pl.BlockSpec((tm, d), lambda i, j: (i, 0)),
        out_shape=jax.ShapeDtypeStruct((m, d), F32),
        scratch_shapes=[pltpu.VMEM((tm + HALO, d), BF16), pltpu.VMEM((tm + HALO, tf), F32),
                        pltpu.VMEM((tm + HALO, tf), F32), pltpu.VMEM((tm, d), F32)],
        compiler_params=_cp(("parallel", "arbitrary")),
        name="conv_ffn_prompt",
    )(x, x, g_pre.reshape(1, d), g_post.reshape(1, d), wa, wg, cwa, cwg, ba, bg, wd)


def _ffn_sample_kernel(ua_ref, ug_ref, bufa_ref, bufg_ref, cwa_ref, cwg_ref, ba_ref, bg_ref, wd_ref, x_ref,
                       gpost_ref, o_ref, acc_sc):
    j = pl.program_id(0)

    @pl.when(j == 0)
    def _():
        acc_sc[...] = jnp.zeros(acc_sc.shape, F32)

    def conv(u_ref, buf_ref, cw_ref, b_ref):
        cw = cw_ref[...]
        out = b_ref[...] + u_ref[...] * cw[CONV_W - 1:CONV_W]
        for tap in range(CONV_W - 1):
            out = out + buf_ref[tap] * cw[tap:tap + 1]
        return out

    act = _gelu_glu(conv(ua_ref, bufa_ref, cwa_ref, ba_ref), conv(ug_ref, bufg_ref, cwg_ref, bg_ref))
    acc_sc[...] += _mm(act.astype(BF16), wd_ref[...])

    @pl.when(j == pl.num_programs(0) - 1)
    def _():
        o_ref[...] = x_ref[...] + _rms(acc_sc[...], gpost_ref[...])


def _ffn_sample_call(u, bufa, bufg, cwa, cwg, ba, bg, wd, x, g_post):
    bsz, d = x.shape
    fp = wd.shape[0]
    tf = _pick(fp, (512, 256, 128))
    nf = fp // tf
    col = lambda r: pl.BlockSpec((r, tf), lambda j: (0, j))
    return pl.pallas_call(
        _ffn_sample_kernel,
        grid=(nf,),
        in_specs=[pl.BlockSpec((bsz, tf), lambda j: (0, j)), pl.BlockSpec((bsz, tf), lambda j: (0, nf + j)),
                  pl.BlockSpec((CONV_W - 1, bsz, tf), lambda j: (0, 0, j)),
                  pl.BlockSpec((CONV_W - 1, bsz, tf), lambda j: (0, 0, j)),
                  col(CONV_W), col(CONV_W), col(1), col(1),
                  pl.BlockSpec((tf, d), lambda j: (j, 0)),
                  pl.BlockSpec((bsz, d), lambda j: (0, 0)), pl.BlockSpec((1, d), lambda j: (0, 0))],
        out_specs=pl.BlockSpec((bsz, d), lambda j: (0, 0)),
        out_shape=jax.ShapeDtypeStruct((bsz, d), F32),
        scratch_shapes=[pltpu.VMEM((bsz, d), F32)],
        compiler_params=_cp(("arbitrary",)),
        name="conv_ffn_sample",
    )(u, u, bufa, bufg, cwa, cwg, ba, bg, wd, x, g_post.reshape(1, d))


def _nsa_sample_select_kernel(q_ref, kc_ref, vc_ref, new_ref, w0_ref, ocmp_ref, idx_ref, *, past_len):
    kvh = pl.program_id(1)
    nbp = kc_ref.shape[3]
    nbt = nbp + 1
    qpos = past_len
    scale = NSA_HEAD_DIM ** -0.5
    qg = q_ref[0, 0].astype(BF16)
    kc = kc_ref[0, 0, 0].astype(BF16)
    vc = vc_ref[0, 0, 0].astype(BF16)
    new = new_ref[0]
    pick = lambda r: jnp.where(kvh == 0, new[r:r + 1], new[r + 1:r + 2])
    kc_new = _mm(jnp.broadcast_to(pick(0), (8, LANES)).astype(BF16), w0_ref[0].astype(BF16))
    vc_new = _mm(jnp.broadcast_to(pick(2), (8, LANES)).astype(BF16), w0_ref[1].astype(BF16))

    s_past = _nt(qg, kc) * scale
    s_new = _nt(qg, kc_new.astype(BF16))[:, 0:1] * scale
    n_past = lax.broadcasted_iota(jnp.int32, (8, nbp), 1)
    mask_past = ((n_past + 1) * CMP_BLOCK - 1) <= qpos
    mask_new = ((nbt * CMP_BLOCK - 1) <= qpos)
    s_past = jnp.where(mask_past, s_past, NEG)
    s_new = s_new if mask_new else jnp.full_like(s_new, NEG)
    m = jnp.maximum(jnp.max(s_past, axis=-1, keepdims=True), s_new)
    p_past = jnp.where(mask_past, jnp.exp(s_past - m), 0.0)
    p_new = jnp.exp(s_new - m) * (1.0 if mask_new else 0.0)
    den = jnp.maximum(jnp.sum(p_past, axis=-1, keepdims=True) + p_new, TINY)
    p_past = p_past / den
    p_new = p_new / den
    o_cmp = _mm(p_past.astype(BF16), vc) + p_new.astype(BF16).astype(F32) * vc_new.astype(BF16).astype(F32)[0:1]
    ocmp_ref[0, 0] = o_cmp

    grp = lax.broadcasted_iota(jnp.int32, (8, 1), 0) < NSA_GROUP
    ps_past = jnp.sum(jnp.where(grp, p_past, 0.0), axis=0, keepdims=True)
    ps_new = jnp.sum(jnp.where(grp, p_new, 0.0), axis=0, keepdims=True)

    width = ((nbt + LANES - 1) // LANES) * LANES
    nrow = ((nbt + 7) // 8) * 8
    n = lax.broadcasted_iota(jnp.int32, (1, width), 1)
    p_all = jnp.concatenate([ps_past, jnp.broadcast_to(ps_new, (1, width - nbp))], axis=1)
    cur = qpos >> CMP_SHIFT
    valid = (n * CMP_BLOCK <= qpos) & (n < nbt)
    forced = (n == 0) | (n == cur) | (n == cur - 1)
    score = jnp.where(valid, p_all + jnp.where(forced, FORCE_BONUS, 0.0), NEG)
    mrow = lax.broadcasted_iota(jnp.int32, (nrow, width), 0)
    ncol = lax.broadcasted_iota(jnp.int32, (nrow, width), 1)
    score_b = jnp.broadcast_to(score, (nrow, width))
    score_col = jnp.sum(jnp.where(mrow == ncol, score_b, 0.0), axis=1, keepdims=True)
    col_ok = lax.broadcasted_iota(jnp.int32, (nrow, 1), 0) < nbt
    beats = col_ok & ((score_col > score_b) | ((score_col == score_b) & (mrow < ncol)))
    rank = jnp.sum(jnp.where(beats, 1.0, 0.0), axis=0, keepdims=True)
    n_sel = min(N_SEL, nbt)
    r = lax.broadcasted_iota(jnp.int32, (N_SEL, width), 0)
    hit = ((jnp.broadcast_to(rank, (N_SEL, width)) == r.astype(F32)) & jnp.broadcast_to(valid, (N_SEL, width))
           & (r < n_sel))
    nf = jnp.broadcast_to(n, (N_SEL, width)).astype(F32)
    idx = jnp.sum(jnp.where(hit, nf, 0.0), axis=1, keepdims=True)
    any_hit = jnp.sum(jnp.where(hit, 1.0, 0.0), axis=1, keepdims=True)
    idx = jnp.where(any_hit > 0.5, idx, -1.0).astype(jnp.int32)
    idx_ref[0, 0] = jnp.broadcast_to(idx, (N_SEL, LANES))


def _nsa_sample_select_call(q8, kcvc_p, new_rows, w0, past_len):
    bsz = q8.shape[0]
    nbp = kcvc_p.shape[3]
    return pl.pallas_call(
        functools.partial(_nsa_sample_select_kernel, past_len=past_len),
        grid=(bsz, NSA_KV_HEADS),
        in_specs=[pl.BlockSpec((1, 1, 8, LANES), lambda b, h: (b, h, 0, 0)),
                  pl.BlockSpec((1, 1, 1, nbp, LANES), lambda b, h: (b, 0, h, 0, 0)),
                  pl.BlockSpec((1, 1, 1, nbp, LANES), lambda b, h: (b, 1, h, 0, 0)),
                  pl.BlockSpec((1, 8, LANES), lambda b, h: (b, 0, 0)),
                  pl.BlockSpec((2, LANES, LANES), lambda b, h: (0, 0, 0))],
        out_specs=[pl.BlockSpec((1, 1, 8, LANES), lambda b, h: (b, h, 0, 0)),
                   pl.BlockSpec((1, 1, N_SEL, LANES), lambda b, h: (b, h, 0, 0))],
        out_shape=[jax.ShapeDtypeStruct((bsz, NSA_KV_HEADS, 8, LANES), F32),
                   jax.ShapeDtypeStruct((bsz, NSA_KV_HEADS, N_SEL, LANES), jnp.int32)],
        compiler_params=_cp(("parallel", "parallel")),
        name="nsa_select_sample",
    )(q8, kcvc_p, kcvc_p, new_rows, w0)


def _nsa_sample_attend_kernel(page_ref, half_ref, flag_ref, q_ref, new_ref, wnew_ref, gate_ref, ocmp_ref, win_ref,
                              *rest, past_len):
    blk_refs = rest[:N_SEL]
    o_ref = rest[N_SEL]
    b = pl.program_id(0)
    kvh = pl.program_id(1)
    nkv = NSA_KV_HEADS
    base = (b * nkv + kvh) * (N_SEL + 1)
    scale = NSA_HEAD_DIM ** -0.5
    qg = q_ref[0, 0].astype(BF16)
    new = new_ref[0]
    wnew = wnew_ref[0]
    pick = lambda arr, r: jnp.where(kvh == 0, arr[r:r + 1], arr[r + 1:r + 2])
    rnd = lambda a: a.astype(BF16).astype(F32)
    qf = qg.astype(F32)

    def attend(x, keep, k_new, v_new, new_on):
        s = jnp.where(keep, _nt(qg, x) * scale, NEG)
        s_n = jnp.sum(qf * rnd(k_new), axis=-1, keepdims=True) * scale
        s_n = jnp.where(new_on, s_n, NEG)
        m = jnp.maximum(jnp.max(s, axis=-1, keepdims=True), s_n)
        p = jnp.where(keep, jnp.exp(s - m), 0.0)
        p_n = jnp.where(new_on, jnp.exp(s_n - m), 0.0)
        den = jnp.maximum(jnp.sum(p, axis=-1, keepdims=True) + p_n, TINY)
        p_v = pltpu.roll(p, nkv, 1)
        return (_mm(p_v.astype(BF16), x) + rnd(p_n) * rnd(v_new)) / den

    nslot = 4 * nkv
    brow = CMP_BLOCK * nslot
    x_sel = jnp.concatenate([r[...].reshape(brow, LANES).astype(BF16) for r in blk_refs], axis=0)
    lane = lax.broadcasted_iota(jnp.int32, (1, N_SEL * brow), 1)
    chunk = lax.shift_right_logical(lane, brow.bit_length() - 1)
    live = jnp.zeros((1, N_SEL * brow), jnp.int32)
    for r in range(N_SEL):
        live = jnp.where(chunk == r, flag_ref[base + r], live)
    keep = (live > 0) & ((lane & (nslot - 1)) == 2 * nkv + kvh)
    o_slc = attend(x_sel, jnp.broadcast_to(keep, (8, N_SEL * brow)), pick(new, 4), pick(new, 6),
                   flag_ref[base + N_SEL] > 0)

    wrows = win_ref.shape[1]
    wb = wrows // (2 * nkv)
    wl = lax.broadcasted_iota(jnp.int32, (8, wrows), 1)
    kpos = past_len - wb + lax.shift_right_logical(wl, (2 * nkv).bit_length() - 1)
    dist = past_len - kpos
    wkeep = (dist >= 0) & (dist <= WINDOW) & (kpos >= 0) & ((wl & (2 * nkv - 1)) == kvh)
    o_win = attend(win_ref[0].astype(BF16), wkeep, pick(wnew, 0), pick(wnew, 2), True)

    gate = gate_ref[0, 0]
    o_ref[0, 0] = (gate[:, 0:LANES] * ocmp_ref[0, 0] + gate[:, LANES:2 * LANES] * o_slc
                   + gate[:, 2 * LANES:3 * LANES] * o_win)


def _nsa_sample_attend_call(pages, halves, flags, q8, new_rows, win_new, gates, o_cmp, win_rows, cache_rows,
                            past_len):
    bsz = q8.shape[0]
    kvh = NSA_KV_HEADS
    nslot = 4 * kvh
    sel = lambda ref, b, h, r: ref[(b * kvh + h) * (N_SEL + 1) + r]
    blk_spec = lambda r: pl.BlockSpec((None, CMP_BLOCK, nslot, LANES),
                                      lambda b, h, pg, hf, fl: (sel(pg, b, h, r), sel(hf, b, h, r), 0, 0))
    return pl.pallas_call(
        functools.partial(_nsa_sample_attend_kernel, past_len=past_len),
        grid_spec=pltpu.PrefetchScalarGridSpec(
            num_scalar_prefetch=3,
            grid=(bsz, kvh),
            in_specs=[pl.BlockSpec((1, 1, 8, LANES), lambda b, h, pg, hf, fl: (b, h, 0, 0)),
                      pl.BlockSpec((1, 8, LANES), lambda b, h, pg, hf, fl: (b, 0, 0)),
                      pl.BlockSpec((1, 4, LANES), lambda b, h, pg, hf, fl: (b, 0, 0)),
                      pl.BlockSpec((1, 1, 8, 3 * LANES), lambda b, h, pg, hf, fl: (b, h, 0, 0)),
                      pl.BlockSpec((1, 1, 8, LANES), lambda b, h, pg, hf, fl: (b, h, 0, 0)),
                      pl.BlockSpec((1, win_rows.shape[1], LANES), lambda b, h, pg, hf, fl: (b, 0, 0))]
                     + [blk_spec(r) for r in range(N_SEL)],
            out_specs=pl.BlockSpec((1, 1, 8, LANES), lambda b, h, pg, hf, fl: (b, h, 0, 0)),
        ),
        out_shape=jax.ShapeDtypeStruct((bsz, kvh, 8, LANES), F32),
        compiler_params=_cp(("parallel", "parallel")),
        name="nsa_attend_sample",
    )(pages, halves, flags, q8, new_rows, win_new, gates, o_cmp, win_rows, *([cache_rows] * N_SEL))


def _pad_cols(a, n):
    return jnp.pad(a, [(0, 0)] * (a.ndim - 1) + [(0, n - a.shape[-1])])


def kernel(x_prompt, x_sample, cache_diff_kv, cache_nsa_kv, state_nsa_win, state_ffn_conv, page_table,
           norm_mix_pre, norm_mix_post, w_in, diff_lambda, diff_subln, nsa_w_cmp, w_branch_diff, w_branch_nsa,
           w_out, norm_ffn_pre, norm_ffn_post, ffn_w_up, ffn_conv_w, ffn_conv_b, ffn_w_down):
    b, s, d = x_prompt.shape
    bs, ts, _ = x_sample.shape
    depth = w_in.shape[0]
    assert depth == 1 and ts == 1
    n_pages = page_table.shape[1]
    past_len = n_pages * PAGE_SIZE
    f = ffn_w_down.shape[1]
    t = b * s
    kvh, hd = NSA_KV_HEADS, NSA_HEAD_DIM
    layer = 0
    lam_init = 0.8 - 0.6 * math.exp(-0.3 * layer)

    wq = DIFF_HEADS * 2 * DIFF_QK_DIM
    wv = DIFF_HEADS * DIFF_V_DIM
    wnq = NSA_HEADS * hd
    wkv = 2 * kvh * hd
    o_dq, o_dk, o_dv = 0, wq, 2 * wq
    o_nq = o_dv + wv
    o_cs = o_nq + wnq
    o_win = o_cs + 2 * wkv
    o_ng = o_win + wkv
    o_mg = o_ng + 3 * NSA_HEADS
    w = w_in[layer]
    w_dq = w[:, o_dq:o_dk].astype(BF16)
    w_dkv = w[:, o_dk:o_nq].astype(BF16)
    w_nq = w[:, o_nq:o_cs].astype(BF16)
    w_cs = w[:, o_cs:o_win].astype(BF16)
    w_win = w[:, o_win:o_ng].astype(BF16)
    w_ng = _pad_cols(w[:, o_ng:o_mg], LANES).astype(BF16)
    w_mg = w[:, o_mg:].astype(BF16)
    w_bd = w_branch_diff[layer].astype(BF16)
    w_bn = w_branch_nsa[layer].astype(BF16)
    w_o = w_out[layer].astype(BF16)
    fp = ((f + 511) // 512) * 512
    w_up = ffn_w_up[layer]
    wa = _pad_cols(w_up[:, :f], fp).astype(BF16)
    wg = _pad_cols(w_up[:, f:], fp).astype(BF16)
    cw = ffn_conv_w[layer]
    cwa, cwg = _pad_cols(cw[:, :f], fp), _pad_cols(cw[:, f:], fp)
    cb = ffn_conv_b[layer].reshape(1, 2 * f)
    ba, bg = _pad_cols(cb[:, :f], fp), _pad_cols(cb[:, f:], fp)
    wd = jnp.pad(ffn_w_down[layer], ((0, fp - f), (0, 0))).astype(BF16)
    w_cmp = nsa_w_cmp[layer]

    lam = _lam_call(diff_lambda[layer], lam_init)

    def projections(h, pos, pos_rows, attn_dtype):
        t64 = _rope_tables(pos, DIFF_QK_DIM)
        t128 = _rope_tables(pos, hd)
        r64, r128, nn, sg = MODE_ROPE64, MODE_ROPE128, MODE_NONE, MODE_SIGMOID
        sub = wkv // 2
        dq = _proj_call(h, w_dq, [r64] * (wq // sub), sub, wq, attn_dtype, t64, pos_rows, "proj_diff_q")
        dkv = _proj_call(h, w_dkv, [r64] * (wq // sub) + [nn] * (wv // sub), sub, wq + wv, F32, t64, pos_rows,
                         "proj_diff_kv")
        nq = _proj_call(h, w_nq, [r128] * (wnq // sub), sub, wnq, attn_dtype, t128, pos_rows, "proj_nsa_q")
        cs = _proj_call(h, w_cs, [r128, nn, r128, nn], sub, 2 * wkv, F32, t128, pos_rows, "proj_nsa_kv")
        win = _proj_call(h, w_win, [r128, nn], sub, wkv, F32, t128, pos_rows, "proj_nsa_win")
        ng = _proj_call(h, w_ng, [sg], LANES, LANES, F32, (), None, "proj_nsa_gate")
        return dq, dkv, nq, cs, win, ng

    xp = x_prompt.reshape(t, d)
    hp = _norm_call(xp, norm_mix_pre[layer])
    pos_p = jnp.arange(s, dtype=jnp.int32)
    dq, dkv, nq, cs, win, ng = projections(hp, pos_p, s, BF16)

    xs = x_sample.reshape(bs, d)
    hs = _norm_call(xs, norm_mix_pre[layer])
    pos_s = jnp.full((bs,), past_len, jnp.int32)
    dq_s, dkv_s, nq_s, cs_s, win_s, ng_s = projections(hs, pos_s, bs, F32)

    n_pool = cache_nsa_kv.shape[1]
    cache_d = cache_diff_kv.reshape(cache_diff_kv.shape[1], PAGE_SIZE, 2 * DIFF_HEADS, LANES)
    cache_n = cache_nsa_kv.reshape(n_pool, PAGE_SIZE, 4 * kvh, hd)
    w_cat = jnp.concatenate([w_cmp[0], w_cmp[1]], axis=-1).astype(BF16)

    diff, diff_s = _diff_attn_call(lam, page_table, dq, dkv, diff_subln[layer], dq_s.reshape(bs, DIFF_HEADS, LANES),
                                   dkv_s.reshape(bs, 2 * DIFF_HEADS, LANES), cache_d, b, s, lam_init)
    kcvc = _compress_call(cs, w_cmp)
    cache_n_halves = cache_nsa_kv.reshape(n_pool, PAGE_SIZE, 2, 2 * kvh, hd)
    nsa, kcvc_p = _nsa_attn_call(page_table, nq, kcvc, cs, win, ng, w_cat, cache_n_halves, b, s)
    mix = _merge_call(hp, diff, nsa, w_mg, w_bd, w_bn)
    xp1 = _out_proj_call(mix, w_o, xp, norm_mix_post[layer])
    xp2 = _ffn_prompt_call(xp1, norm_ffn_pre[layer], norm_ffn_post[layer], wa, wg, cwa, cwg, ba, bg, wd, s)

    wbp = min(WINDOW, s)
    new_diff_kv_prompt = dkv.reshape(1, b, s, 2, DIFF_HEADS, 2 * DIFF_QK_DIM)
    new_nsa_kv_prompt = cs.reshape(1, b, s, 4, kvh, hd)
    new_win_prompt = win.reshape(b, s, 2 * kvh * hd)[:, s - wbp:].reshape(1, b, wbp, 2, kvh, hd)

    diff_s = diff_s.reshape(bs, DIFF_HEADS * LANES).astype(BF16)
    q8 = jnp.pad(nq_s.reshape(bs, kvh, NSA_GROUP, hd), ((0, 0), (0, 0), (0, 8 - NSA_GROUP), (0, 0)))
    new_rows = cs_s.reshape(bs, 4 * kvh, hd)
    o_cmp_s, idx_s = _nsa_sample_select_call(q8, kcvc_p, new_rows, w_cmp[:, 0], past_len)
    idx = idx_s[..., 0]
    nbp = past_len // CMP_BLOCK
    bpp = PAGE_SIZE // CMP_BLOCK
    is_past = (idx >= 0) & (idx < nbp)
    safe = jnp.clip(idx, 0, nbp - 1)
    page = jnp.take_along_axis(page_table, (safe // bpp).reshape(bs, -1), axis=1).reshape(idx.shape)
    new_sel = jnp.any(idx == nbp, axis=-1, keepdims=True)
    pad1 = lambda a: jnp.concatenate([a, jnp.zeros_like(a[..., :1])], axis=-1).reshape(-1).astype(jnp.int32)
    pages_sel, halves_sel = pad1(page), pad1(safe % bpp)
    flags = jnp.concatenate([is_past, new_sel], axis=-1).reshape(-1).astype(jnp.int32)
    gates = ng_s[:, :3 * NSA_HEADS].reshape(bs, 3, kvh, NSA_GROUP).transpose(0, 2, 3, 1)
    gates = jnp.pad(gates, ((0, 0), (0, 0), (0, 8 - NSA_GROUP), (0, 0)))
    gates = jnp.broadcast_to(gates[..., None], (bs, kvh, 8, 3, LANES)).reshape(bs, kvh, 8, 3 * LANES)
    wbs = state_nsa_win.shape[2]
    win_rows = state_nsa_win.reshape(bs, wbs * 2 * kvh, hd)
    nsa_s = _nsa_sample_attend_call(pages_sel, halves_sel, flags, q8, new_rows, win_s.reshape(bs, 2 * kvh, hd),
                                    gates, o_cmp_s, win_rows, cache_n, past_len)
    nsa_s = nsa_s[:, :, :NSA_GROUP].reshape(bs, NSA_HEADS * hd).astype(BF16)

    mix_s = _merge_call(hs, diff_s, nsa_s, w_mg, w_bd, w_bn)
    xs1 = _out_proj_call(mix_s, w_o, xs, norm_mix_post[layer])

    tail = xp1.reshape(b, s, d)[:, s - (CONV_W - 1):].reshape(b * (CONV_W - 1), d)
    rows = jnp.concatenate([xs1, tail], axis=0)
    pad_r = (-rows.shape[0]) % 16
    rows = jnp.pad(rows, ((0, pad_r), (0, 0)))
    h_rows = _norm_call(rows, norm_ffn_pre[layer])
    w_up_p = jnp.concatenate([wa, wg], axis=1)
    u_rows = _proj_call(h_rows, w_up_p, [MODE_NONE] * (2 * fp // 512), 512, 512, F32, (), None, "proj_ffn_up_rows")
    u_unpad = jnp.concatenate([u_rows[:, :f], u_rows[:, fp:fp + f]], axis=1)
    buf = state_ffn_conv[layer]
    bufa = _pad_cols(buf[..., :f], fp).transpose(1, 0, 2)
    bufg = _pad_cols(buf[..., f:], fp).transpose(1, 0, 2)
    xs2 = _ffn_sample_call(u_rows, bufa, bufg, cwa, cwg, ba, bg, wd, xs1, norm_ffn_post[layer])

    new_conv_prompt = u_unpad[bs:bs + b * (CONV_W - 1)].reshape(1, b, CONV_W - 1, 2 * f)
    new_conv_sample = jnp.concatenate([buf[:, 1:], u_unpad[:bs, None]], axis=1)[None]
    new_diff_kv_sample = dkv_s.reshape(1, bs, 1, 2, DIFF_HEADS, 2 * DIFF_QK_DIM)
    new_nsa_kv_sample = cs_s.reshape(1, bs, 1, 4, kvh, hd)
    new_win_sample = jnp.concatenate([state_nsa_win[layer][:, 1:], win_s.reshape(bs, 1, 2, kvh, hd)], axis=1)[None]

    return (xp2.reshape(b, s, d), xs2.reshape(bs, 1, d), new_diff_kv_prompt, new_diff_kv_sample,
            new_nsa_kv_prompt, new_nsa_kv_sample, new_win_prompt, new_win_sample, new_conv_prompt, new_conv_sample)
```

```python
import functools
import math

import jax
import jax.numpy as jnp
from jax import lax
from jax.experimental import pallas as pl
from jax.experimental.pallas import tpu as pltpu

F32 = jnp.float32
BF16 = jnp.bfloat16

DIFF_HEADS = 8
DIFF_QK_DIM = 64
DIFF_V_DIM = 128
NSA_HEADS = 8
NSA_KV_HEADS = 2
NSA_GROUP = 4
NSA_HEAD_DIM = 128
CMP_BLOCK = 64
CMP_SHIFT = 6
N_SEL = 16
WINDOW = 512
PAGE_SIZE = 128
CONV_W = 3
FORCE_BONUS = 1e4
ROPE_THETA = 10000.0
EPS = 1e-6
NEG = -1e30
TINY = 1e-30
LANES = 128
VMEM_LIMIT = 52 * 1024 * 1024

_NT = (((1,), (1,)), ((), ()))


def _nt(a, b):
    return lax.dot_general(a, b, _NT, preferred_element_type=F32)


def _mm(a, b):
    return jnp.dot(a, b, preferred_element_type=F32)


def _cp(sem):
    return pltpu.CompilerParams(dimension_semantics=sem, vmem_limit_bytes=VMEM_LIMIT)


def _pick(n, cands):
    for c in cands:
        if n % c == 0:
            return c
    return n


def _rms(x, g):
    return x * lax.rsqrt(jnp.mean(x * x, axis=-1, keepdims=True) + EPS) * g


def _lam_kernel(l_ref, o_ref, *, lam_init):
    l = l_ref[...]
    a = jnp.sum(l[0:1] * l[1:2], axis=-1, keepdims=True)
    b = jnp.sum(l[2:3] * l[3:4], axis=-1, keepdims=True)
    o_ref[...] = jnp.broadcast_to(jnp.exp(a) - jnp.exp(b) + lam_init, o_ref.shape)


def _lam_call(lam_params, lam_init):
    out = pl.pallas_call(
        functools.partial(_lam_kernel, lam_init=lam_init),
        out_shape=jax.ShapeDtypeStruct((8, LANES), F32),
        name="diff_lambda",
    )(lam_params)
    return out[0, :1]


def _norm_kernel(x_ref, g_ref, o_ref):
    o_ref[...] = _rms(x_ref[...], g_ref[...]).astype(o_ref.dtype)


def _norm_call(x, g):
    m, d = x.shape
    tm = _pick(m, (512, 256, 128, 64, 32, 16))
    return pl.pallas_call(
        _norm_kernel,
        grid=(m // tm,),
        in_specs=[pl.BlockSpec((tm, d), lambda i: (i, 0)), pl.BlockSpec((1, d), lambda i: (0, 0))],
        out_specs=pl.BlockSpec((tm, d), lambda i: (i, 0)),
        out_shape=jax.ShapeDtypeStruct((m, d), BF16),
        compiler_params=_cp(("parallel",)),
        name="rmsnorm",
    )(x, g.reshape(1, d))


MODE_NONE, MODE_ROPE64, MODE_ROPE128, MODE_SIGMOID = 0, 1, 2, 3


def _rope_tables(pos, d):
    half = d // 2
    inv = 1.0 / (ROPE_THETA ** (jnp.arange(0, d, 2, dtype=F32) / d))
    ang = pos.astype(F32)[:, None] * inv[None, :]
    lane = jnp.arange(LANES)
    cos = jnp.cos(ang)[:, lane % half]
    sin = jnp.sin(ang)[:, lane % half]
    first = ((lane % d) < half)[None, :]
    return cos, jnp.where(first, -sin, 0.0), jnp.where(first, 0.0, sin)


def _apply_rope(z, cos, sa, sb, half):
    outs = []
    for c in range(z.shape[1] // LANES):
        blk = z[:, c * LANES:(c + 1) * LANES]
        outs.append(blk * cos + pltpu.roll(blk, LANES - half, 1) * sa + pltpu.roll(blk, half, 1) * sb)
    return outs[0] if len(outs) == 1 else jnp.concatenate(outs, axis=1)


def _proj_kernel(*refs, modes, sub, n_tab):
    h_ref, w_ref = refs[0], refs[1]
    tabs = refs[2:2 + n_tab]
    o_ref = refs[2 + n_tab]
    acc = _mm(h_ref[...], w_ref[...])
    for c, mode in enumerate(modes):
        blk = acc[:, c * sub:(c + 1) * sub]
        if mode == MODE_SIGMOID:
            blk = jax.nn.sigmoid(blk)
        elif mode != MODE_NONE:
            half = 32 if mode == MODE_ROPE64 else 64
            blk = _apply_rope(blk, tabs[0][...], tabs[1][...], tabs[2][...], half)
        o_ref[:, c * sub:(c + 1) * sub] = blk.astype(o_ref.dtype)


def _proj_call(h, w, modes, sub, tn, out_dtype, tables=(), pos_rows=None, name="proj"):
    m, k = h.shape
    n = w.shape[1]
    per = tn // sub
    assert n % tn == 0 and per * sub == tn and len(modes) * sub == n
    assert all(tuple(modes[j:j + per]) == tuple(modes[:per]) for j in range(0, len(modes), per))
    tm = _pick(m, (1024, 512, 256, 128, 64, 48, 32, 16))
    in_specs = [pl.BlockSpec((tm, k), lambda i, j: (i, 0)), pl.BlockSpec((k, tn), lambda i, j: (0, j))]
    if tables:
        nblk = pos_rows // tm
        assert nblk * tm == pos_rows
        in_specs += [pl.BlockSpec((tm, LANES), lambda i, j: (i % nblk, 0)) for _ in tables]
    return pl.pallas_call(
        functools.partial(_proj_kernel, modes=tuple(modes[:per]), sub=sub, n_tab=len(tables)),
        grid=(m // tm, n // tn),
        in_specs=in_specs,
        out_specs=pl.BlockSpec((tm, tn), lambda i, j: (i, j)),
        out_shape=jax.ShapeDtypeStruct((m, n), out_dtype),
        compiler_params=_cp(("parallel", "arbitrary")),
        name=name,
    )(h, w, *tables)


def _flash_step(carry, s, v_ext):
    m, acc = carry
    m_new = jnp.maximum(m, jnp.max(s, axis=-1, keepdims=True))
    p = jnp.exp(s - m_new)
    return m_new, acc * jnp.exp(m - m_new) + _mm(p.astype(BF16), v_ext)


def _flash_init(rows, d):
    return jnp.full((rows, 1), NEG, F32), jnp.zeros((rows, 2 * d), F32)


def _flash_out(carry, d):
    acc = carry[1]
    return acc[:, :d] / jnp.maximum(acc[:, d:d + 1], TINY)


def _ext_ones(v):
    return jnp.concatenate([v, jnp.ones(v.shape, v.dtype)], axis=1)


def _step_id():
    t = pl.program_id(0)
    for ax in range(1, 3):
        t = t * pl.num_programs(ax) + pl.program_id(ax)
    return t


def _pages_per_step(n_req, n_pages, n_steps):
    for n_pp in range(1, n_pages + 1):
        if n_pages % n_pp == 0 and n_req * (n_pages // n_pp) <= n_steps:
            return n_pp
    raise ValueError("prompt grid too small to carry the sample group's page stream")


def _diff_decode_step(pg, n_pg, lam_ref, q_ref, kvn_ref, g_ref, k_refs, v_refs, o_ref, q_sc, m_sc, l_sc, acc_sc,
                      out_scale):
    nh = DIFF_HEADS
    nr = 2 * nh
    prow = PAGE_SIZE * nh

    @pl.when(pg == 0)
    def _():
        q8 = q_ref[0]
        lane = lax.broadcasted_iota(jnp.int32, (nh, LANES), 1)
        scale = DIFF_QK_DIM ** -0.5
        q_sc[0:nh] = (jnp.where(lane < DIFF_QK_DIM, q8, 0.0) * scale).astype(BF16).astype(F32)
        q_sc[nh:nr] = (jnp.where(lane >= DIFF_QK_DIM, q8, 0.0) * scale).astype(BF16).astype(F32)
        m_sc[...] = jnp.full(m_sc.shape, NEG, F32)
        l_sc[...] = jnp.zeros(l_sc.shape, F32)
        acc_sc[...] = jnp.zeros(acc_sc.shape, F32)

    def own_head(width):
        lane = lax.broadcasted_iota(jnp.int32, (nr, width), 1)
        row = lax.broadcasted_iota(jnp.int32, (nr, width), 0)
        return (lane & (nh - 1)) == (row & (nh - 1))

    def update(s, keep, pv):
        sm = jnp.where(keep, s, NEG)
        m_old = m_sc[...]
        m_new = jnp.maximum(m_old, jnp.max(sm, axis=-1, keepdims=True))
        corr = jnp.exp(m_old - m_new)
        pe = jnp.where(keep, jnp.exp(sm - m_new[:, 0:1]), 0.0)
        m_sc[...] = m_new
        l_sc[...] = l_sc[...] * corr + jnp.sum(pe, axis=-1, keepdims=True)
        acc_sc[...] = acc_sc[...] * corr + pv(pe.astype(BF16))

    qb = q_sc[...].astype(BF16)
    s = jnp.concatenate([_nt(qb, kr[...].reshape(prow, LANES).astype(BF16)) for kr in k_refs], axis=1)

    def pv_pages(pb):
        out = None
        for i, vr in enumerate(v_refs):
            part = _mm(pb[:, i * prow:(i + 1) * prow], vr[...].reshape(prow, LANES).astype(BF16))
            out = part if out is None else out + part
        return out

    update(s, own_head(len(k_refs) * prow), pv_pages)

    @pl.when(pg == n_pg - 1)
    def _():
        kvn = kvn_ref[0]
        vk = jnp.concatenate([kvn[nh:nr], kvn[0:nh]], axis=0).astype(BF16)
        is_key = lax.broadcasted_iota(jnp.int32, (nr, nr), 1) < nh
        update(_nt(qb, kvn.astype(BF16)), own_head(nr) & is_key, lambda pb: _mm(pb, vk))
        o = acc_sc[...] / l_sc[...]
        od = o[0:nh] - lam_ref[0] * o[nh:nr]
        o_ref[0] = _rms(od, g_ref[...]) * out_scale


def _diff_attn_kernel(lam_ref, pt_ref, q_ref, k_ref, v_ref, g_ref, qd_ref, kvn_ref, *rest, tq, tk, out_scale, n_pp,
                      n_pg, dec_steps, n_steps):
    k_pages = rest[:n_pp]
    v_pages = rest[n_pp:2 * n_pp]
    o_ref, od_ref = rest[2 * n_pp:2 * n_pp + 2]
    kb_ref, vb_ref, q_sc, m_sc, l_sc, acc_sc = rest[2 * n_pp + 2:]
    qi = pl.program_id(2)
    dv = DIFF_V_DIM

    @pl.when(qi == 0)
    def _():
        kb_ref[...] = k_ref[...].astype(BF16)
        vb_ref[...] = _ext_ones(v_ref[...].astype(BF16))

    q = q_ref[...]
    lane = lax.broadcasted_iota(jnp.int32, q.shape, 1)
    scale = jnp.asarray(DIFF_QK_DIM ** -0.5, q.dtype)
    q1 = jnp.where(lane < DIFF_QK_DIM, q, 0) * scale
    q2 = jnp.where(lane >= DIFF_QK_DIM, q, 0) * scale

    def block(kb, carry, bias):
        start = pl.multiple_of(kb * tk, tk)
        k = kb_ref[pl.ds(start, tk), :]
        v = vb_ref[pl.ds(start, tk), :]
        s1, s2 = _nt(q1, k), _nt(q2, k)
        if bias is not None:
            s1, s2 = s1 + bias, s2 + bias
        return _flash_step(carry[0], s1, v), _flash_step(carry[1], s2, v)

    init = _flash_init(tq, dv)
    nd = tq // tk
    carry = lax.fori_loop(0, qi * nd, lambda kb, c: block(kb, c, None), (init, init))
    for j in range(nd):
        causal = ((j * tk + lax.broadcasted_iota(jnp.int32, (tq, tk), 1))
                  <= lax.broadcasted_iota(jnp.int32, (tq, tk), 0))
        carry = block(qi * nd + j, carry, jnp.where(causal, 0.0, NEG))
    c1, c2 = carry
    o = _flash_out(c1, dv) - lam_ref[0] * _flash_out(c2, dv)
    o_ref[...] = (_rms(o, g_ref[...]) * out_scale).astype(o_ref.dtype)

    t = _step_id()
    decode = functools.partial(_diff_decode_step, lax.rem(t, n_pg), n_pg, lam_ref, qd_ref, kvn_ref, g_ref, k_pages,
                               v_pages, od_ref, q_sc, m_sc, l_sc, acc_sc, out_scale)
    if dec_steps == n_steps:
        decode()
    else:
        pl.when(t < dec_steps)(decode)


def _diff_attn_call(lam, page_table, q, kv, subln, q_dec, kv_new, cache, b, s, lam_init):
    t = b * s
    tq = _pick(s, (512, 256, 128))
    tk = min(tq, 256)
    nq = s // tq
    nh = DIFF_HEADS
    bs, n_pages = page_table.shape
    n_steps = b * nh * nq
    n_pp = _pages_per_step(bs, n_pages, n_steps)
    n_pg = n_pages // n_pp
    dec_steps = bs * n_pg

    def dec(bi, h, qi):
        td = jnp.minimum((bi * nh + h) * nq + qi, dec_steps - 1)
        return td // n_pg, td % n_pg

    def page_spec(kk, half):
        def imap(bi, h, qi, lam_, pt):
            r, g = dec(bi, h, qi)
            return pt[r, g * n_pp + kk], 0, half, 0
        return pl.BlockSpec((None, PAGE_SIZE, nh, LANES), imap)

    req_spec = lambda rows: pl.BlockSpec((1, rows, LANES), lambda bi, h, qi, lam_, pt: (dec(bi, h, qi)[0], 0, 0))
    st = pltpu.VMEM((2 * nh, LANES), F32)
    return pl.pallas_call(
        functools.partial(_diff_attn_kernel, tq=tq, tk=tk, out_scale=1.0 - lam_init, n_pp=n_pp, n_pg=n_pg,
                          dec_steps=dec_steps, n_steps=n_steps),
        grid_spec=pltpu.PrefetchScalarGridSpec(
            num_scalar_prefetch=2,
            grid=(b, nh, nq),
            in_specs=[
                pl.BlockSpec((tq, LANES), lambda bi, h, qi, lam_, pt: (bi * nq + qi, h)),
                pl.BlockSpec((s, LANES), lambda bi, h, qi, lam_, pt: (bi, h)),
                pl.BlockSpec((s, LANES), lambda bi, h, qi, lam_, pt: (bi, nh + h)),
                pl.BlockSpec((1, LANES), lambda bi, h, qi, lam_, pt: (0, 0)),
                req_spec(nh), req_spec(2 * nh),
            ] + [page_spec(kk, 0) for kk in range(n_pp)] + [page_spec(kk, 1) for kk in range(n_pp)],
            out_specs=[pl.BlockSpec((tq, LANES), lambda bi, h, qi, lam_, pt: (bi * nq + qi, h)), req_spec(nh)],
            scratch_shapes=[pltpu.VMEM((s, LANES), BF16), pltpu.VMEM((s, 2 * LANES), BF16), st, st, st, st],
        ),
        out_shape=[jax.ShapeDtypeStruct((t, nh * DIFF_V_DIM), BF16), jax.ShapeDtypeStruct((bs, nh, LANES), F32)],
        compiler_params=_cp(("arbitrary", "arbitrary", "arbitrary")),
        name="diff_attn",
    )(lam, page_table, q, kv, kv, subln.reshape(1, LANES), q_dec, kv_new, *([cache] * (2 * n_pp)))


def _compress_kernel(x_ref, w_ref, o_ref):
    d = NSA_HEAD_DIM
    nblk = x_ref.shape[0] // CMP_BLOCK
    jc = 8
    acc = None
    for j0 in range(0, CMP_BLOCK, jc):
        cols = [x_ref[pl.ds(j, nblk, stride=CMP_BLOCK), :].astype(BF16) for j in range(j0, j0 + jc)]
        part = _mm(jnp.concatenate(cols, axis=1), w_ref[0, j0:j0 + jc].reshape(jc * d, d).astype(BF16))
        acc = part if acc is None else acc + part
    o_ref[0, 0] = acc


def _compress_call(cs, w_cmp):
    t = cs.shape[0]
    kvh, d = NSA_KV_HEADS, NSA_HEAD_DIM
    return pl.pallas_call(
        _compress_kernel,
        grid=(2, kvh),
        in_specs=[pl.BlockSpec((t, d), lambda s, h: (0, s * kvh + h)),
                  pl.BlockSpec((1, CMP_BLOCK, d, d), lambda s, h: (s, 0, 0, 0))],
        out_specs=pl.BlockSpec((1, 1, t // CMP_BLOCK, d), lambda s, h: (s, h, 0, 0)),
        out_shape=jax.ShapeDtypeStruct((2, kvh, t // CMP_BLOCK, d), F32),
        compiler_params=_cp(("parallel", "parallel")),
        name="nsa_compress_prompt",
    )(cs, w_cmp)


def _topk_mask(score, blk, n_sel):
    nb = score.shape[0]
    rank = jnp.zeros(score.shape, jnp.int32)
    for mm in range(nb):
        rm = score[mm:mm + 1, :]
        beats = (rm > score) | ((rm == score) & (blk > mm))
        rank = rank + beats.astype(jnp.int32)
    return rank < n_sel


def _compress_pages_step(w_ref, page_refs, o_ref, acc_sc):
    bpp = PAGE_SIZE // CMP_BLOCK
    d = NSA_HEAD_DIM
    nslot = 2 * NSA_KV_HEADS
    nblk = len(page_refs) * bpp
    jc = 8
    acc = None
    for j0 in range(0, CMP_BLOCK, jc):
        cols = []
        for j in range(j0, j0 + jc):
            tiles = [pr[blk * CMP_BLOCK + j] for pr in page_refs for blk in range(bpp)]
            cols.append(jnp.concatenate(tiles, axis=0).astype(BF16))
        part = _mm(jnp.concatenate(cols, axis=1), w_ref[j0:j0 + jc].reshape(jc * d, 2 * d))
        acc = part if acc is None else acc + part
    acc_sc[0] = acc[:, :d]
    acc_sc[1] = acc[:, d:]
    for s in range(2):
        for h in range(NSA_KV_HEADS):
            o_ref[0, s, h] = acc_sc[s, pl.ds(s * NSA_KV_HEADS + h, nblk, stride=nslot), :]


def _nsa_prompt_kernel(pt_ref, q_ref, kc_ref, vc_ref, ks_ref, vs_ref, kw_ref, vw_ref, ng_ref, wcat_ref, *rest,
                       s_len, tq, tk, n_pp, dec_steps, n_steps):
    page_refs = rest[:n_pp]
    o_ref, oc_ref = rest[n_pp:n_pp + 2]
    ksb, vsb, kwb, vwb, bias_ref, cacc_sc = rest[n_pp + 2:]
    kvh = pl.program_id(1)
    qi = pl.program_id(2)
    nb = s_len // CMP_BLOCK
    g = NSA_GROUP
    rows = g * tq
    d = NSA_HEAD_DIM
    scale = d ** -0.5

    @pl.when(qi == 0)
    def _():
        ksb[...] = ks_ref[...].astype(BF16)
        vsb[...] = _ext_ones(vs_ref[...].astype(BF16))
        kwb[...] = kw_ref[...].astype(BF16)
        vwb[...] = _ext_ones(vw_ref[...].astype(BF16))

    q = q_ref[...]
    qs = [q[:, i * LANES:(i + 1) * LANES] for i in range(g)]
    qr = jnp.concatenate(qs, axis=0)
    q0 = qi * tq
    tpos = q0 + lax.broadcasted_iota(jnp.int32, (tq, 1), 0)
    qpos = q0 + (lax.broadcasted_iota(jnp.int32, (rows, 1), 0) & (tq - 1))

    kc = kc_ref[0, 0, 0].astype(BF16)
    vc = vc_ref[0, 0, 0].astype(BF16)

    def cmp_probs(s, end_le_qpos, axis):
        sm = jnp.where(end_le_qpos, s, NEG)
        p = jnp.where(end_le_qpos, jnp.exp(sm - jnp.max(sm, axis=axis, keepdims=True)), 0.0)
        return p / jnp.maximum(jnp.sum(p, axis=axis, keepdims=True), TINY)

    nblk = lax.broadcasted_iota(jnp.int32, (rows, nb), 1)
    pc = cmp_probs(_nt(qr, kc) * scale, ((nblk + 1) * CMP_BLOCK - 1) <= qpos, 1)
    o_cmp = _mm(pc.astype(BF16), vc)

    nblk_t = lax.broadcasted_iota(jnp.int32, (nb, rows), 0)
    qpos_t = q0 + (lax.broadcasted_iota(jnp.int32, (nb, rows), 1) & (tq - 1))
    pc_t = cmp_probs(_nt(kc, qr) * scale, ((nblk_t + 1) * CMP_BLOCK - 1) <= qpos_t, 0)
    p_slc = pc_t[:, 0:tq]
    for i in range(1, g):
        p_slc = p_slc + pc_t[:, i * tq:(i + 1) * tq]
    blk = lax.broadcasted_iota(jnp.int32, (nb, tq), 0)
    tpos_t = q0 + lax.broadcasted_iota(jnp.int32, (nb, tq), 1)
    cur = lax.shift_right_logical(tpos_t, CMP_SHIFT)
    valid = blk * CMP_BLOCK <= tpos_t
    forced = (blk == 0) | (blk == cur) | (blk == cur - 1)
    score = jnp.where(valid, p_slc + jnp.where(forced, FORCE_BONUS, 0.0), NEG)
    sel_t = _topk_mask(score, blk, min(N_SEL, nb)) & valid
    eye = (lax.broadcasted_iota(jnp.int32, (tq, tq), 0) == lax.broadcasted_iota(jnp.int32, (tq, tq), 1))
    sel = _nt(eye.astype(BF16), sel_t.astype(BF16))
    expand = (lax.shift_right_logical(lax.broadcasted_iota(jnp.int32, (nb, s_len), 1), CMP_SHIFT)
              == lax.broadcasted_iota(jnp.int32, (nb, s_len), 0))
    selk = _mm(sel.astype(BF16), expand.astype(BF16))
    kpos_all = lax.broadcasted_iota(jnp.int32, (1, s_len), 1)
    bias_ref[...] = jnp.where((selk > 0.5) & (kpos_all <= tpos), 0.0, NEG)

    def slc_step(kb, carry):
        start = pl.multiple_of(kb * tk, tk)
        k = ksb[pl.ds(start, tk), :]
        v = vsb[pl.ds(start, tk), :]
        bias = bias_ref[:, pl.ds(start, tk)]
        return tuple(_flash_step(carry[i], _nt(qs[i], k) * scale + bias, v) for i in range(g))

    def win_step(kb, carry):
        start = pl.multiple_of(kb * tk, tk)
        k = kwb[pl.ds(start, tk), :]
        v = vwb[pl.ds(start, tk), :]
        dist = tpos - (start + lax.broadcasted_iota(jnp.int32, (1, tk), 1))
        bias = jnp.where((dist >= 0) & (dist <= WINDOW), 0.0, NEG)
        return tuple(_flash_step(carry[i], _nt(qs[i], k) * scale + bias, v) for i in range(g))

    init = tuple(_flash_init(tq, d) for _ in range(g))
    lo = jnp.maximum(q0 - WINDOW, 0) // tk
    hi = (q0 + tq + tk - 1) // tk
    c_s = lax.fori_loop(0, lo, slc_step, init)
    c_s, c_w = lax.fori_loop(lo, hi, lambda kb, c: (slc_step(kb, c[0]), win_step(kb, c[1])), (c_s, init))

    ng = ng_ref[...]
    for i in range(g):
        def gate(r):
            c0 = r * NSA_HEADS + i
            c1 = c0 + NSA_GROUP
            return jnp.where(kvh == 0, ng[:, c0:c0 + 1], ng[:, c1:c1 + 1])
        o = (gate(0) * o_cmp[i * tq:(i + 1) * tq] + gate(1) * _flash_out(c_s[i], d)
             + gate(2) * _flash_out(c_w[i], d))
        o_ref[:, i * LANES:(i + 1) * LANES] = o.astype(o_ref.dtype)

    compress = functools.partial(_compress_pages_step, wcat_ref, page_refs, oc_ref, cacc_sc)
    if dec_steps == n_steps:
        compress()
    else:
        pl.when(_step_id() < dec_steps)(compress)


def _nsa_attn_call(page_table, nq, kcvc, nkv_cs, nkv_win, ng, w_cat, cache_halves, b, s):
    t = b * s
    tq = _pick(s, (256, 128))
    nqb = s // tq
    nb = s // CMP_BLOCK
    gw = NSA_GROUP * LANES
    kvh = NSA_KV_HEADS
    d = NSA_HEAD_DIM
    nslot = 2 * kvh
    bpp = PAGE_SIZE // CMP_BLOCK
    bs, n_pages = page_table.shape
    n_steps = b * kvh * nqb
    n_pp = _pages_per_step(bs, n_pages, n_steps)
    n_pg = n_pages // n_pp
    dec_steps = bs * n_pg
    kcvc5 = kcvc.reshape(2, kvh, b, nb, d)

    def dec(bi, h, qi):
        td = jnp.minimum((bi * kvh + h) * nqb + qi, dec_steps - 1)
        return td // n_pg, td % n_pg

    def page_spec(kk):
        def imap(bi, h, qi, pt):
            r, g = dec(bi, h, qi)
            return pt[r, g * n_pp + kk], 0, 0, 0, 0
        return pl.BlockSpec((None, PAGE_SIZE, None, nslot, d), imap)

    def summary_map(bi, h, qi, pt):
        r, g = dec(bi, h, qi)
        return r, 0, 0, g, 0

    kv_spec = lambda col: pl.BlockSpec((s, LANES), lambda bi, h, qi, pt: (bi, col + h))
    return pl.pallas_call(
        functools.partial(_nsa_prompt_kernel, s_len=s, tq=tq, tk=_pick(s, (256, 128)), n_pp=n_pp,
                          dec_steps=dec_steps, n_steps=n_steps),
        grid_spec=pltpu.PrefetchScalarGridSpec(
            num_scalar_prefetch=1,
            grid=(b, kvh, nqb),
            in_specs=[
                pl.BlockSpec((tq, gw), lambda bi, h, qi, pt: (bi * nqb + qi, h)),
                pl.BlockSpec((1, 1, 1, nb, d), lambda bi, h, qi, pt: (0, h, bi, 0, 0)),
                pl.BlockSpec((1, 1, 1, nb, d), lambda bi, h, qi, pt: (1, h, bi, 0, 0)),
                kv_spec(4), kv_spec(6),
                kv_spec(0), kv_spec(2),
                pl.BlockSpec((tq, LANES), lambda bi, h, qi, pt: (bi * nqb + qi, 0)),
                pl.BlockSpec(w_cat.shape, lambda bi, h, qi, pt: (0, 0, 0)),
            ] + [page_spec(kk) for kk in range(n_pp)],
            out_specs=[pl.BlockSpec((tq, gw), lambda bi, h, qi, pt: (bi * nqb + qi, h)),
                       pl.BlockSpec((1, 2, kvh, n_pp * bpp, d), summary_map)],
            scratch_shapes=[pltpu.VMEM((s, LANES), BF16), pltpu.VMEM((s, 2 * LANES), BF16),
                            pltpu.VMEM((s, LANES), BF16), pltpu.VMEM((s, 2 * LANES), BF16),
                            pltpu.VMEM((tq, s), F32), pltpu.VMEM((2, n_pp * bpp * nslot, d), F32)],
        ),
        out_shape=[jax.ShapeDtypeStruct((t, NSA_HEADS * d), BF16),
                   jax.ShapeDtypeStruct((bs, 2, kvh, n_pages * bpp, d), F32)],
        compiler_params=_cp(("arbitrary", "arbitrary", "arbitrary")),
        name="nsa_attn",
    )(page_table, nq, kcvc5, kcvc5, nkv_cs, nkv_cs, nkv_win, nkv_win, ng, w_cat, *([cache_halves] * n_pp))


def _merge_kernel(h_ref, d_ref, n_ref, wg0_ref, wg1_ref, wbd_ref, wbn_ref, o_ref):
    h = h_ref[...]
    a = _mm(d_ref[...], wbd_ref[...])
    bb = _mm(n_ref[...], wbn_ref[...])
    g0 = jax.nn.sigmoid(_mm(h, wg0_ref[...]))
    g1 = jax.nn.sigmoid(_mm(h, wg1_ref[...]))
    o_ref[...] = (g0 * a + g1 * bb).astype(o_ref.dtype)


def _merge_call(h, diff, nsa, w_mg, w_bd, w_bn):
    m, d = h.shape
    tm = _pick(m, (1024, 512, 256, 128, 64, 32))
    tn = _pick(d, (512, 256, 128))
    nj = d // tn
    row = lambda kk: pl.BlockSpec((tm, kk), lambda i, j: (i, 0))
    return pl.pallas_call(
        _merge_kernel,
        grid=(m // tm, nj),
        in_specs=[row(d), row(diff.shape[1]), row(nsa.shape[1]),
                  pl.BlockSpec((d, tn), lambda i, j: (0, j)),
                  pl.BlockSpec((d, tn), lambda i, j: (0, nj + j)),
                  pl.BlockSpec((w_bd.shape[0], tn), lambda i, j: (0, j)),
                  pl.BlockSpec((w_bn.shape[0], tn), lambda i, j: (0, j))],
        out_specs=pl.BlockSpec((tm, tn), lambda i, j: (i, j)),
        out_shape=jax.ShapeDtypeStruct((m, d), BF16),
        compiler_params=_cp(("parallel", "arbitrary")),
        name="branch_merge",
    )(h, diff, nsa, w_mg, w_mg, w_bd, w_bn)


def _out_proj_kernel(m_ref, w_ref, x_ref, g_ref, o_ref):
    y = _mm(m_ref[...], w_ref[...])
    o_ref[...] = x_ref[...] + _rms(y, g_ref[...])


def _out_proj_call(mix, w_o, x, g):
    m, d = x.shape
    tm = _pick(m, (512, 256, 128, 64, 32))
    return pl.pallas_call(
        _out_proj_kernel,
        grid=(m // tm,),
        in_specs=[pl.BlockSpec((tm, d), lambda i: (i, 0)), pl.BlockSpec((d, d), lambda i: (0, 0)),
                  pl.BlockSpec((tm, d), lambda i: (i, 0)), pl.BlockSpec((1, d), lambda i: (0, 0))],
        out_specs=pl.BlockSpec((tm, d), lambda i: (i, 0)),
        out_shape=jax.ShapeDtypeStruct((m, d), F32),
        compiler_params=_cp(("parallel",)),
        name="out_proj_norm_residual",
    )(mix, w_o, x, g.reshape(1, d))


HALO = 16


def _gelu_glu(ca, cg):
    return jax.nn.gelu(ca, approximate=True) * cg


def _ffn_prompt_kernel(x_ref, xh_ref, gpre_ref, gpost_ref, wa_ref, wg_ref, cwa_ref, cwg_ref, ba_ref, bg_ref,
                       wd_ref, o_ref, h_sc, ua_sc, ug_sc, acc_sc, *, tm, blocks_per_seq):
    i = pl.program_id(0)
    j = pl.program_id(1)

    @pl.when(j == 0)
    def _():
        h_sc[0:HALO, :] = _rms(xh_ref[...], gpre_ref[...]).astype(BF16)
        h_sc[HALO:, :] = _rms(x_ref[...], gpre_ref[...]).astype(BF16)
        acc_sc[...] = jnp.zeros(acc_sc.shape, F32)

    first = (i % blocks_per_seq) == 0
    h = h_sc[...]
    keep = jnp.logical_not(first & (lax.broadcasted_iota(jnp.int32, (tm + HALO, 1), 0) < HALO))
    ua_sc[...] = jnp.where(keep, _mm(h, wa_ref[...]), 0.0)
    ug_sc[...] = jnp.where(keep, _mm(h, wg_ref[...]), 0.0)

    def conv(u_sc, cw_ref, b_ref):
        cw = cw_ref[...]
        out = b_ref[...]
        for tap in range(CONV_W):
            out = out + u_sc[pl.ds(HALO - (CONV_W - 1) + tap, tm), :] * cw[tap:tap + 1]
        return out

    act = _gelu_glu(conv(ua_sc, cwa_ref, ba_ref), conv(ug_sc, cwg_ref, bg_ref))
    acc_sc[...] += _mm(act.astype(BF16), wd_ref[...])

    @pl.when(j == pl.num_programs(1) - 1)
    def _():
        o_ref[...] = x_ref[...] + _rms(acc_sc[...], gpost_ref[...])


def _ffn_prompt_call(x, g_pre, g_post, wa, wg, cwa, cwg, ba, bg, wd, s):
    m, d = x.shape
    fp = wa.shape[1]
    tm = _pick(s, (512, 256, 128))
    tf = _pick(fp, (512, 256, 128))
    hb = tm // HALO
    col = lambda r: pl.BlockSpec((r, tf), lambda i, j: (0, j))
    return pl.pallas_call(
        functools.partial(_ffn_prompt_kernel, tm=tm, blocks_per_seq=s // tm),
        grid=(m // tm, fp // tf),
        in_specs=[pl.BlockSpec((tm, d), lambda i, j: (i, 0)),
                  pl.BlockSpec((HALO, d), lambda i, j: (jnp.maximum(i * hb - 1, 0), 0)),
                  pl.BlockSpec((1, d), lambda i, j: (0, 0)), pl.BlockSpec((1, d), lambda i, j: (0, 0)),
                  col(d), col(d), col(CONV_W), col(CONV_W), col(1), col(1),
                  pl.BlockSpec((tf, d), lambda i, j: (j, 0))],
        out_specs=pl.BlockSpec((tm, d), lambda i, j: (i, 0)),
        out_shape=jax.ShapeDtypeStruct((m, d), F32),
        scratch_shapes=[pltpu.VMEM((tm + HALO, d), BF16), pltpu.VMEM((tm + HALO, tf), F32),
                        pltpu.VMEM((tm + HALO, tf), F32), pltpu.VMEM((tm, d), F32)],
        compiler_params=_cp(("parallel", "arbitrary")),
        name="conv_ffn_prompt",
    )(x, x, g_pre.reshape(1, d), g_post.reshape(1, d), wa, wg, cwa, cwg, ba, bg, wd)


def _ffn_sample_kernel(ua_ref, ug_ref, bufa_ref, bufg_ref, cwa_ref, cwg_ref, ba_ref, bg_ref, wd_ref, x_ref,
                       gpost_ref, o_ref, acc_sc):
    j = pl.program_id(0)

    @pl.when(j == 0)
    def _():
        acc_sc[...] = jnp.zeros(acc_sc.shape, F32)

    def conv(u_ref, buf_ref, cw_ref, b_ref):
        cw = cw_ref[...]
        out = b_ref[...] + u_ref[...] * cw[CONV_W - 1:CONV_W]
        for tap in range(CONV_W - 1):
            out = out + buf_ref[tap] * cw[tap:tap + 1]
        return out

    act = _gelu_glu(conv(ua_ref, bufa_ref, cwa_ref, ba_ref), conv(ug_ref, bufg_ref, cwg_ref, bg_ref))
    acc_sc[...] += _mm(act.astype(BF16), wd_ref[...])

    @pl.when(j == pl.num_programs(0) - 1)
    def _():
        o_ref[...] = x_ref[...] + _rms(acc_sc[...], gpost_ref[...])


def _ffn_sample_call(u, bufa, bufg, cwa, cwg, ba, bg, wd, x, g_post):
    bsz, d = x.shape
    fp = wd.shape[0]
    tf = _pick(fp, (512, 256, 128))
    nf = fp // tf
    col = lambda r: pl.BlockSpec((r, tf), lambda j: (0, j))
    return pl.pallas_call(
        _ffn_sample_kernel,
        grid=(nf,),
        in_specs=[pl.BlockSpec((bsz, tf), lambda j: (0, j)), pl.BlockSpec((bsz, tf), lambda j: (0, nf + j)),
                  pl.BlockSpec((CONV_W - 1, bsz, tf), lambda j: (0, 0, j)),
                  pl.BlockSpec((CONV_W - 1, bsz, tf), lambda j: (0, 0, j)),
                  col(CONV_W), col(CONV_W), col(1), col(1),
                  pl.BlockSpec((tf, d), lambda j: (j, 0)),
                  pl.BlockSpec((bsz, d), lambda j: (0, 0)), pl.BlockSpec((1, d), lambda j: (0, 0))],
        out_specs=pl.BlockSpec((bsz, d), lambda j: (0, 0)),
        out_shape=jax.ShapeDtypeStruct((bsz, d), F32),
        scratch_shapes=[pltpu.VMEM((bsz, d), F32)],
        compiler_params=_cp(("arbitrary",)),
        name="conv_ffn_sample",
    )(u, u, bufa, bufg, cwa, cwg, ba, bg, wd, x, g_post.reshape(1, d))


def _nsa_sample_select_kernel(q_ref, kc_ref, vc_ref, new_ref, w0_ref, ocmp_ref, idx_ref, *, past_len):
    kvh = pl.program_id(1)
    nbp = kc_ref.shape[3]
    nbt = nbp + 1
    qpos = past_len
    scale = NSA_HEAD_DIM ** -0.5
    qg = q_ref[0, 0].astype(BF16)
    kc = kc_ref[0, 0, 0].astype(BF16)
    vc = vc_ref[0, 0, 0].astype(BF16)
    new = new_ref[0]
    pick = lambda r: jnp.where(kvh == 0, new[r:r + 1], new[r + 1:r + 2])
    kc_new = _mm(jnp.broadcast_to(pick(0), (8, LANES)).astype(BF16), w0_ref[0].astype(BF16))
    vc_new = _mm(jnp.broadcast_to(pick(2), (8, LANES)).astype(BF16), w0_ref[1].astype(BF16))

    s_past = _nt(qg, kc) * scale
    s_new = _nt(qg, kc_new.astype(BF16))[:, 0:1] * scale
    n_past = lax.broadcasted_iota(jnp.int32, (8, nbp), 1)
    mask_past = ((n_past + 1) * CMP_BLOCK - 1) <= qpos
    mask_new = ((nbt * CMP_BLOCK - 1) <= qpos)
    s_past = jnp.where(mask_past, s_past, NEG)
    s_new = s_new if mask_new else jnp.full_like(s_new, NEG)
    m = jnp.maximum(jnp.max(s_past, axis=-1, keepdims=True), s_new)
    p_past = jnp.where(mask_past, jnp.exp(s_past - m), 0.0)
    p_new = jnp.exp(s_new - m) * (1.0 if mask_new else 0.0)
    den = jnp.maximum(jnp.sum(p_past, axis=-1, keepdims=True) + p_new, TINY)
    p_past = p_past / den
    p_new = p_new / den
    o_cmp = _mm(p_past.astype(BF16), vc) + p_new.astype(BF16).astype(F32) * vc_new.astype(BF16).astype(F32)[0:1]
    ocmp_ref[0, 0] = o_cmp

    grp = lax.broadcasted_iota(jnp.int32, (8, 1), 0) < NSA_GROUP
    ps_past = jnp.sum(jnp.where(grp, p_past, 0.0), axis=0, keepdims=True)
    ps_new = jnp.sum(jnp.where(grp, p_new, 0.0), axis=0, keepdims=True)

    width = ((nbt + LANES - 1) // LANES) * LANES
    nrow = ((nbt + 7) // 8) * 8
    n = lax.broadcasted_iota(jnp.int32, (1, width), 1)
    p_all = jnp.concatenate([ps_past, jnp.broadcast_to(ps_new, (1, width - nbp))], axis=1)
    cur = qpos >> CMP_SHIFT
    valid = (n * CMP_BLOCK <= qpos) & (n < nbt)
    forced = (n == 0) | (n == cur) | (n == cur - 1)
    score = jnp.where(valid, p_all + jnp.where(forced, FORCE_BONUS, 0.0), NEG)
    mrow = lax.broadcasted_iota(jnp.int32, (nrow, width), 0)
    ncol = lax.broadcasted_iota(jnp.int32, (nrow, width), 1)
    score_b = jnp.broadcast_to(score, (nrow, width))
    score_col = jnp.sum(jnp.where(mrow == ncol, score_b, 0.0), axis=1, keepdims=True)
    col_ok = lax.broadcasted_iota(jnp.int32, (nrow, 1), 0) < nbt
    beats = col_ok & ((score_col > score_b) | ((score_col == score_b) & (mrow < ncol)))
    rank = jnp.sum(jnp.where(beats, 1.0, 0.0), axis=0, keepdims=True)
    n_sel = min(N_SEL, nbt)
    r = lax.broadcasted_iota(jnp.int32, (N_SEL, width), 0)
    hit = ((jnp.broadcast_to(rank, (N_SEL, width)) == r.astype(F32)) & jnp.broadcast_to(valid, (N_SEL, width))
           & (r < n_sel))
    nf = jnp.broadcast_to(n, (N_SEL, width)).astype(F32)
    idx = jnp.sum(jnp.where(hit, nf, 0.0), axis=1, keepdims=True)
    any_hit = jnp.sum(jnp.where(hit, 1.0, 0.0), axis=1, keepdims=True)
    idx = jnp.where(any_hit > 0.5, idx, -1.0).astype(jnp.int32)
    idx_ref[0, 0] = jnp.broadcast_to(idx, (N_SEL, LANES))


def _nsa_sample_select_call(q8, kcvc_p, new_rows, w0, past_len):
    bsz = q8.shape[0]
    nbp = kcvc_p.shape[3]
    return pl.pallas_call(
        functools.partial(_nsa_sample_select_kernel, past_len=past_len),
        grid=(bsz, NSA_KV_HEADS),
        in_specs=[pl.BlockSpec((1, 1, 8, LANES), lambda b, h: (b, h, 0, 0)),
                  pl.BlockSpec((1, 1, 1, nbp, LANES), lambda b, h: (b, 0, h, 0, 0)),
                  pl.BlockSpec((1, 1, 1, nbp, LANES), lambda b, h: (b, 1, h, 0, 0)),
                  pl.BlockSpec((1, 8, LANES), lambda b, h: (b, 0, 0)),
                  pl.BlockSpec((2, LANES, LANES), lambda b, h: (0, 0, 0))],
        out_specs=[pl.BlockSpec((1, 1, 8, LANES), lambda b, h: (b, h, 0, 0)),
                   pl.BlockSpec((1, 1, N_SEL, LANES), lambda b, h: (b, h, 0, 0))],
        out_shape=[jax.ShapeDtypeStruct((bsz, NSA_KV_HEADS, 8, LANES), F32),
                   jax.ShapeDtypeStruct((bsz, NSA_KV_HEADS, N_SEL, LANES), jnp.int32)],
        compiler_params=_cp(("parallel", "parallel")),
        name="nsa_select_sample",
    )(q8, kcvc_p, kcvc_p, new_rows, w0)


def _nsa_sample_attend_kernel(page_ref, half_ref, flag_ref, q_ref, new_ref, wnew_ref, gate_ref, ocmp_ref, win_ref,
                              *rest, past_len):
    blk_refs = rest[:N_SEL]
    o_ref = rest[N_SEL]
    b = pl.program_id(0)
    kvh = pl.program_id(1)
    nkv = NSA_KV_HEADS
    base = (b * nkv + kvh) * (N_SEL + 1)
    scale = NSA_HEAD_DIM ** -0.5
    qg = q_ref[0, 0].astype(BF16)
    new = new_ref[0]
    wnew = wnew_ref[0]
    pick = lambda arr, r: jnp.where(kvh == 0, arr[r:r + 1], arr[r + 1:r + 2])
    rnd = lambda a: a.astype(BF16).astype(F32)
    qf = qg.astype(F32)

    def attend(x, keep, k_new, v_new, new_on):
        s = jnp.where(keep, _nt(qg, x) * scale, NEG)
        s_n = jnp.sum(qf * rnd(k_new), axis=-1, keepdims=True) * scale
        s_n = jnp.where(new_on, s_n, NEG)
        m = jnp.maximum(jnp.max(s, axis=-1, keepdims=True), s_n)
        p = jnp.where(keep, jnp.exp(s - m), 0.0)
        p_n = jnp.where(new_on, jnp.exp(s_n - m), 0.0)
        den = jnp.maximum(jnp.sum(p, axis=-1, keepdims=True) + p_n, TINY)
        p_v = pltpu.roll(p, nkv, 1)
        return (_mm(p_v.astype(BF16), x) + rnd(p_n) * rnd(v_new)) / den

    nslot = 4 * nkv
    brow = CMP_BLOCK * nslot
    x_sel = jnp.concatenate([r[...].reshape(brow, LANES).astype(BF16) for r in blk_refs], axis=0)
    lane = lax.broadcasted_iota(jnp.int32, (1, N_SEL * brow), 1)
    chunk = lax.shift_right_logical(lane, brow.bit_length() - 1)
    live = jnp.zeros((1, N_SEL * brow), jnp.int32)
    for r in range(N_SEL):
        live = jnp.where(chunk == r, flag_ref[base + r], live)
    keep = (live > 0) & ((lane & (nslot - 1)) == 2 * nkv + kvh)
    o_slc = attend(x_sel, jnp.broadcast_to(keep, (8, N_SEL * brow)), pick(new, 4), pick(new, 6),
                   flag_ref[base + N_SEL] > 0)

    wrows = win_ref.shape[1]
    wb = wrows // (2 * nkv)
    wl = lax.broadcasted_iota(jnp.int32, (8, wrows), 1)
    kpos = past_len - wb + lax.shift_right_logical(wl, (2 * nkv).bit_length() - 1)
    dist = past_len - kpos
    wkeep = (dist >= 0) & (dist <= WINDOW) & (kpos >= 0) & ((wl & (2 * nkv - 1)) == kvh)
    o_win = attend(win_ref[0].astype(BF16), wkeep, pick(wnew, 0), pick(wnew, 2), True)

    gate = gate_ref[0, 0]
    o_ref[0, 0] = (gate[:, 0:LANES] * ocmp_ref[0, 0] + gate[:, LANES:2 * LANES] * o_slc
                   + gate[:, 2 * LANES:3 * LANES] * o_win)


def _nsa_sample_attend_call(pages, halves, flags, q8, new_rows, win_new, gates, o_cmp, win_rows, cache_rows,
                            past_len):
    bsz = q8.shape[0]
    kvh = NSA_KV_HEADS
    nslot = 4 * kvh
    sel = lambda ref, b, h, r: ref[(b * kvh + h) * (N_SEL + 1) + r]
    blk_spec = lambda r: pl.BlockSpec((None, CMP_BLOCK, nslot, LANES),
                                      lambda b, h, pg, hf, fl: (sel(pg, b, h, r), sel(hf, b, h, r), 0, 0))
    return pl.pallas_call(
        functools.partial(_nsa_sample_attend_kernel, past_len=past_len),
        grid_spec=pltpu.PrefetchScalarGridSpec(
            num_scalar_prefetch=3,
            grid=(bsz, kvh),
            in_specs=[pl.BlockSpec((1, 1, 8, LANES), lambda b, h, pg, hf, fl: (b, h, 0, 0)),
                      pl.BlockSpec((1, 8, LANES), lambda b, h, pg, hf, fl: (b, 0, 0)),
                      pl.BlockSpec((1, 4, LANES), lambda b, h, pg, hf, fl: (b, 0, 0)),
                      pl.BlockSpec((1, 1, 8, 3 * LANES), lambda b, h, pg, hf, fl: (b, h, 0, 0)),
                      pl.BlockSpec((1, 1, 8, LANES), lambda b, h, pg, hf, fl: (b, h, 0, 0)),
                      pl.BlockSpec((1, win_rows.shape[1], LANES), lambda b, h, pg, hf, fl: (b, 0, 0))]
                     + [blk_spec(r) for r in range(N_SEL)],
            out_specs=pl.BlockSpec((1, 1, 8, LANES), lambda b, h, pg, hf, fl: (b, h, 0, 0)),
        ),
        out_shape=jax.ShapeDtypeStruct((bsz, kvh, 8, LANES), F32),
        compiler_params=_cp(("parallel", "parallel")),
        name="nsa_attend_sample",
    )(pages, halves, flags, q8, new_rows, win_new, gates, o_cmp, win_rows, *([cache_rows] * N_SEL))


def _pad_cols(a, n):
    return jnp.pad(a, [(0, 0)] * (a.ndim - 1) + [(0, n - a.shape[-1])])


def kernel(x_prompt, x_sample, cache_diff_kv, cache_nsa_kv, state_nsa_win, state_ffn_conv, page_table,
           norm_mix_pre, norm_mix_post, w_in, diff_lambda, diff_subln, nsa_w_cmp, w_branch_diff, w_branch_nsa,
           w_out, norm_ffn_pre, norm_ffn_post, ffn_w_up, ffn_conv_w, ffn_conv_b, ffn_w_down):
    b, s, d = x_prompt.shape
    bs, ts, _ = x_sample.shape
    depth = w_in.shape[0]
    assert depth == 1 and ts == 1
    n_pages = page_table.shape[1]
    past_len = n_pages * PAGE_SIZE
    f = ffn_w_down.shape[1]
    t = b * s
    kvh, hd = NSA_KV_HEADS, NSA_HEAD_DIM
    layer = 0
    lam_init = 0.8 - 0.6 * math.exp(-0.3 * layer)

    wq = DIFF_HEADS * 2 * DIFF_QK_DIM
    wv = DIFF_HEADS * DIFF_V_DIM
    wnq = NSA_HEADS * hd
    wkv = 2 * kvh * hd
    o_dq, o_dk, o_dv = 0, wq, 2 * wq
    o_nq = o_dv + wv
    o_cs = o_nq + wnq
    o_win = o_cs + 2 * wkv
    o_ng = o_win + wkv
    o_mg = o_ng + 3 * NSA_HEADS
    w = w_in[layer]
    w_dq = w[:, o_dq:o_dk].astype(BF16)
    w_dkv = w[:, o_dk:o_nq].astype(BF16)
    w_nq = w[:, o_nq:o_cs].astype(BF16)
    w_cs = w[:, o_cs:o_win].astype(BF16)
    w_win = w[:, o_win:o_ng].astype(BF16)
    w_ng = _pad_cols(w[:, o_ng:o_mg], LANES).astype(BF16)
    w_mg = w[:, o_mg:].astype(BF16)
    w_bd = w_branch_diff[layer].astype(BF16)
    w_bn = w_branch_nsa[layer].astype(BF16)
    w_o = w_out[layer].astype(BF16)
    fp = ((f + 511) // 512) * 512
    w_up = ffn_w_up[layer]
    wa = _pad_cols(w_up[:, :f], fp).astype(BF16)
    wg = _pad_cols(w_up[:, f:], fp).astype(BF16)
    cw = ffn_conv_w[layer]
    cwa, cwg = _pad_cols(cw[:, :f], fp), _pad_cols(cw[:, f:], fp)
    cb = ffn_conv_b[layer].reshape(1, 2 * f)
    ba, bg = _pad_cols(cb[:, :f], fp), _pad_cols(cb[:, f:], fp)
    wd = jnp.pad(ffn_w_down[layer], ((0, fp - f), (0, 0))).astype(BF16)
    w_cmp = nsa_w_cmp[layer]

    lam = _lam_call(diff_lambda[layer], lam_init)

    def projections(h, pos, pos_rows, attn_dtype):
        t64 = _rope_tables(pos, DIFF_QK_DIM)
        t128 = _rope_tables(pos, hd)
        r64, r128, nn, sg = MODE_ROPE64, MODE_ROPE128, MODE_NONE, MODE_SIGMOID
        sub = wkv // 2
        dq = _proj_call(h, w_dq, [r64] * (wq // sub), sub, wq, attn_dtype, t64, pos_rows, "proj_diff_q")
        dkv = _proj_call(h, w_dkv, [r64] * (wq // sub) + [nn] * (wv // sub), sub, wq + wv, F32, t64, pos_rows,
                         "proj_diff_kv")
        nq = _proj_call(h, w_nq, [r128] * (wnq // sub), sub, wnq, attn_dtype, t128, pos_rows, "proj_nsa_q")
        cs = _proj_call(h, w_cs, [r128, nn, r128, nn], sub, 2 * wkv, F32, t128, pos_rows, "proj_nsa_kv")
        win = _proj_call(h, w_win, [r128, nn], sub, wkv, F32, t128, pos_rows, "proj_nsa_win")
        ng = _proj_call(h, w_ng, [sg], LANES, LANES, F32, (), None, "proj_nsa_gate")
        return dq, dkv, nq, cs, win, ng

    xp = x_prompt.reshape(t, d)
    hp = _norm_call(xp, norm_mix_pre[layer])
    pos_p = jnp.arange(s, dtype=jnp.int32)
    dq, dkv, nq, cs, win, ng = projections(hp, pos_p, s, BF16)

    xs = x_sample.reshape(bs, d)
    hs = _norm_call(xs, norm_mix_pre[layer])
    pos_s = jnp.full((bs,), past_len, jnp.int32)
    dq_s, dkv_s, nq_s, cs_s, win_s, ng_s = projections(hs, pos_s, bs, F32)

    n_pool = cache_nsa_kv.shape[1]
    cache_d = cache_diff_kv.reshape(cache_diff_kv.shape[1], PAGE_SIZE, 2 * DIFF_HEADS, LANES)
    cache_n = cache_nsa_kv.reshape(n_pool, PAGE_SIZE, 4 * kvh, hd)
    w_cat = jnp.concatenate([w_cmp[0], w_cmp[1]], axis=-1).astype(BF16)

    diff, diff_s = _diff_attn_call(lam, page_table, dq, dkv, diff_subln[layer], dq_s.reshape(bs, DIFF_HEADS, LANES),
                                   dkv_s.reshape(bs, 2 * DIFF_HEADS, LANES), cache_d, b, s, lam_init)
    kcvc = _compress_call(cs, w_cmp)
    cache_n_halves = cache_nsa_kv.reshape(n_pool, PAGE_SIZE, 2, 2 * kvh, hd)
    nsa, kcvc_p = _nsa_attn_call(page_table, nq, kcvc, cs, win, ng, w_cat, cache_n_halves, b, s)
    mix = _merge_call(hp, diff, nsa, w_mg, w_bd, w_bn)
    xp1 = _out_proj_call(mix, w_o, xp, norm_mix_post[layer])
    xp2 = _ffn_prompt_call(xp1, norm_ffn_pre[layer], norm_ffn_post[layer], wa, wg, cwa, cwg, ba, bg, wd, s)

    wbp = min(WINDOW, s)
    new_diff_kv_prompt = dkv.reshape(1, b, s, 2, DIFF_HEADS, 2 * DIFF_QK_DIM)
    new_nsa_kv_prompt = cs.reshape(1, b, s, 4, kvh, hd)
    new_win_prompt = win.reshape(b, s, 2 * kvh * hd)[:, s - wbp:].reshape(1, b, wbp, 2, kvh, hd)

    diff_s = diff_s.reshape(bs, DIFF_HEADS * LANES).astype(BF16)
    q8 = jnp.pad(nq_s.reshape(bs, kvh, NSA_GROUP, hd), ((0, 0), (0, 0), (0, 8 - NSA_GROUP), (0, 0)))
    new_rows = cs_s.reshape(bs, 4 * kvh, hd)
    o_cmp_s, idx_s = _nsa_sample_select_call(q8, kcvc_p, new_rows, w_cmp[:, 0], past_len)
    idx = idx_s[..., 0]
    nbp = past_len // CMP_BLOCK
    bpp = PAGE_SIZE // CMP_BLOCK
    is_past = (idx >= 0) & (idx < nbp)
    safe = jnp.clip(idx, 0, nbp - 1)
    page = jnp.take_along_axis(page_table, (safe // bpp).reshape(bs, -1), axis=1).reshape(idx.shape)
    new_sel = jnp.any(idx == nbp, axis=-1, keepdims=True)
    pad1 = lambda a: jnp.concatenate([a, jnp.zeros_like(a[..., :1])], axis=-1).reshape(-1).astype(jnp.int32)
    pages_sel, halves_sel = pad1(page), pad1(safe % bpp)
    flags = jnp.concatenate([is_past, new_sel], axis=-1).reshape(-1).astype(jnp.int32)
    gates = ng_s[:, :3 * NSA_HEADS].reshape(bs, 3, kvh, NSA_GROUP).transpose(0, 2, 3, 1)
    gates = jnp.pad(gates, ((0, 0), (0, 0), (0, 8 - NSA_GROUP), (0, 0)))
    gates = jnp.broadcast_to(gates[..., None], (bs, kvh, 8, 3, LANES)).reshape(bs, kvh, 8, 3 * LANES)
    wbs = state_nsa_win.shape[2]
    win_rows = state_nsa_win.reshape(bs, wbs * 2 * kvh, hd)
    nsa_s = _nsa_sample_attend_call(pages_sel, halves_sel, flags, q8, new_rows, win_s.reshape(bs, 2 * kvh, hd),
                                    gates, o_cmp_s, win_rows, cache_n, past_len)
    nsa_s = nsa_s[:, :, :NSA_GROUP].reshape(bs, NSA_HEADS * hd).astype(BF16)

    mix_s = _merge_call(hs, diff_s, nsa_s, w_mg, w_bd, w_bn)
    xs1 = _out_proj_call(mix_s, w_o, xs, norm_mix_post[layer])

    tail = xp1.reshape(b, s, d)[:, s - (CONV_W - 1):].reshape(b * (CONV_W - 1), d)
    rows = jnp.concatenate([xs1, tail], axis=0)
    pad_r = (-rows.shape[0]) % 16
    rows = jnp.pad(rows, ((0, pad_r), (0, 0)))
    h_rows = _norm_call(rows, norm_ffn_pre[layer])
    w_up_p = jnp.concatenate([wa, wg], axis=1)
    u_rows = _proj_call(h_rows, w_up_p, [MODE_NONE] * (2 * fp // 512), 512, 512, F32, (), None, "proj_ffn_up_rows")
    u_unpad = jnp.concatenate([u_rows[:, :f], u_rows[:, fp:fp + f]], axis=1)
    buf = state_ffn_conv[layer]
    bufa = _pad_cols(buf[..., :f], fp).transpose(1, 0, 2)
    bufg = _pad_cols(buf[..., f:], fp).transpose(1, 0, 2)
    xs2 = _ffn_sample_call(u_rows, bufa, bufg, cwa, cwg, ba, bg, wd, xs1, norm_ffn_post[layer])

    new_conv_prompt = u_unpad[bs:bs + b * (CONV_W - 1)].reshape(1, b, CONV_W - 1, 2 * f)
    new_conv_sample = jnp.concatenate([buf[:, 1:], u_unpad[:bs, None]], axis=1)[None]
    new_diff_kv_sample = dkv_s.reshape(1, bs, 1, 2, DIFF_HEADS, 2 * DIFF_QK_DIM)
    new_nsa_kv_sample = cs_s.reshape(1, bs, 1, 4, kvh, hd)
    new_win_sample = jnp.concatenate([state_nsa_win[layer][:, 1:], win_s.reshape(bs, 1, 2, kvh, hd)], axis=1)[None]

    return (xp2.reshape(b, s, d), xs2.reshape(bs, 1, d), new_diff_kv_prompt, new_diff_kv_sample,
            new_nsa_kv_prompt, new_nsa_kv_sample, new_win_prompt, new_win_sample, new_conv_prompt, new_conv_sample)
```
